```python
import jax, jax.numpy as jnp
from jax import lax
import numpy as np

D_MODEL = 2048
BATCH = 1
SEQ = 16384
DEPTH = 4
DEC_BATCH = 2
DEC_SEQ = 8192
PAST_LEN = 128

N_MIXERS = 2
N_ATTN_LAYERS = (DEPTH + 1) // 2
N_FNET_LAYERS = DEPTH // 2
N_HEADS = 16
QK_NOPE_DIM = 128
QK_ROPE_DIM = 64
QK_HEAD_DIM = QK_NOPE_DIM + QK_ROPE_DIM
V_HEAD_DIM = 128
Q_LORA_RANK = 512
KV_LORA_RANK = 512
ROPE_THETA = 10000.0
FNET_GROUPS = 8
FNET_GROUP_DIM = D_MODEL // FNET_GROUPS
D_FF = 4 * D_MODEL
Q_BLOCK = 128
EPS = 1e-6

kernel_name = "hybrid_mla_fnet_encoder"


def rmsnorm(x, g):
    xf = x.astype(jnp.float32)
    y = xf * lax.rsqrt(jnp.mean(xf * xf, axis=-1, keepdims=True) + EPS) * g.astype(jnp.float32)
    return y.astype(x.dtype)


def rope_tables(seq_len):
    inv_freq = ROPE_THETA ** (-jnp.arange(0, QK_ROPE_DIM, 2, dtype=jnp.float32) / QK_ROPE_DIM)
    ang = jnp.arange(seq_len, dtype=jnp.float32)[:, None] * inv_freq[None, :]
    return jnp.cos(ang), jnp.sin(ang)


def apply_rope(x, cos, sin):
    xf = x.astype(jnp.float32)
    half = QK_ROPE_DIM // 2
    x1, x2 = xf[..., :half], xf[..., half:]
    c = cos[None, :, None, :]
    s = sin[None, :, None, :]
    return jnp.concatenate([x1 * c - x2 * s, x2 * c + x1 * s], axis=-1).astype(x.dtype)


def mla_mixer(h, w_q_a, q_a_norm, w_q_b, w_kv_a, kv_a_norm, w_kv_b, q_norm, k_norm, w_o):
    B, S, _ = h.shape
    cq = rmsnorm(h @ w_q_a, q_a_norm)
    q = (cq @ w_q_b).reshape(B, S, N_HEADS, QK_HEAD_DIM)
    ckv = h @ w_kv_a
    c_kv, k_pe = ckv[..., :KV_LORA_RANK], ckv[..., KV_LORA_RANK:]
    kv = (rmsnorm(c_kv, kv_a_norm) @ w_kv_b).reshape(B, S, N_HEADS, QK_NOPE_DIM + V_HEAD_DIM)
    k_nope, v = kv[..., :QK_NOPE_DIM], kv[..., QK_NOPE_DIM:]
    k = jnp.concatenate([k_nope, jnp.broadcast_to(k_pe[:, :, None, :], (B, S, N_HEADS, QK_ROPE_DIM))], axis=-1)
    q = rmsnorm(q, q_norm)
    k = rmsnorm(k, k_norm)
    cos, sin = rope_tables(S)
    q = jnp.concatenate([q[..., :QK_NOPE_DIM], apply_rope(q[..., QK_NOPE_DIM:], cos, sin)], axis=-1)
    k = jnp.concatenate([k[..., :QK_NOPE_DIM], apply_rope(k[..., QK_NOPE_DIM:], cos, sin)], axis=-1)
    scale = QK_HEAD_DIM ** -0.5
    nb = S // Q_BLOCK
    qb = q.reshape(B, nb, Q_BLOCK, N_HEADS, QK_HEAD_DIM).transpose(1, 0, 2, 3, 4)

    def attend(q_blk):
        s = jnp.einsum('bqhd,bkhd->bhqk', q_blk, k, preferred_element_type=jnp.float32) * scale
        p = jax.nn.softmax(s, axis=-1)
        return jnp.einsum('bhqk,bkhd->bqhd', p.astype(v.dtype), v)

    o = lax.map(attend, qb)
    o = o.transpose(1, 0, 2, 3, 4).reshape(B, S, N_HEADS * V_HEAD_DIM)
    return o @ w_o


def fnet_mixer(h, w_o):
    B, S, D = h.shape
    hg = h.astype(jnp.float32).reshape(B, S, FNET_GROUPS, FNET_GROUP_DIM)
    f = jnp.fft.fft2(hg, axes=(1, 3), norm="ortho").real
    return f.reshape(B, S, D).astype(h.dtype) @ w_o


def trunk(x, attn_norm, w_q_a, q_a_norm, w_q_b, w_kv_a, kv_a_norm, w_kv_b, q_norm, k_norm,
          w_attn_o, fnet_norm, w_fnet_o, mlp_norm, w_up, w_down):
    for i in range(DEPTH):
        j = i // N_MIXERS
        if i % N_MIXERS == 0:
            x = x + mla_mixer(rmsnorm(x, attn_norm[j]), w_q_a[j], q_a_norm[j], w_q_b[j], w_kv_a[j],
                              kv_a_norm[j], w_kv_b[j], q_norm[j], k_norm[j], w_attn_o[j])
        else:
            x = x + fnet_mixer(rmsnorm(x, fnet_norm[j]), w_fnet_o[j])
        h = rmsnorm(x, mlp_norm[i])
        x = x + jnp.square(jax.nn.relu(h @ w_up[i])) @ w_down[i]
    return x


def setup_inputs(seed: int = 0) -> dict:
    key = jax.random.key(seed)
    ks = jax.random.split(key, 20)
    f32 = jnp.float32

    def w(k, shape, fan_in):
        return jax.random.normal(k, shape, f32) * (fan_in ** -0.5)

    def g(k, shape):
        return 1.0 + 0.02 * jax.random.normal(k, shape, f32)

    A, F = N_ATTN_LAYERS, N_FNET_LAYERS
    return {
        "x_prompt": jax.random.normal(ks[0], (BATCH, SEQ, D_MODEL), f32),
        "x_sample": jax.random.normal(ks[1], (DEC_BATCH, DEC_SEQ, D_MODEL), f32),
        "attn_norm": g(ks[2], (A, D_MODEL)),
        "w_q_a": w(ks[3], (A, D_MODEL, Q_LORA_RANK), D_MODEL),
        "q_a_norm": g(ks[4], (A, Q_LORA_RANK)),
        "w_q_b": w(ks[5], (A, Q_LORA_RANK, N_HEADS * QK_HEAD_DIM), Q_LORA_RANK),
        "w_kv_a": w(ks[6], (A, D_MODEL, KV_LORA_RANK + QK_ROPE_DIM), D_MODEL),
        "kv_a_norm": g(ks[7], (A, KV_LORA_RANK)),
        "w_kv_b": w(ks[8], (A, KV_LORA_RANK, N_HEADS * (QK_NOPE_DIM + V_HEAD_DIM)), KV_LORA_RANK),
        "q_norm": g(ks[9], (A, QK_HEAD_DIM)),
        "k_norm": g(ks[10], (A, QK_HEAD_DIM)),
        "w_attn_o": w(ks[11], (A, N_HEADS * V_HEAD_DIM, D_MODEL), N_HEADS * V_HEAD_DIM),
        "fnet_norm": g(ks[12], (F, D_MODEL)),
        "w_fnet_o": w(ks[13], (F, D_MODEL, D_MODEL), D_MODEL),
        "mlp_norm": g(ks[14], (DEPTH, D_MODEL)),
        "w_up": w(ks[15], (DEPTH, D_MODEL, D_FF), D_MODEL),
        "w_down": w(ks[16], (DEPTH, D_FF, D_MODEL), D_FF),
    }


def reference(x_prompt, x_sample, attn_norm, w_q_a, q_a_norm, w_q_b, w_kv_a, kv_a_norm, w_kv_b,
              q_norm, k_norm, w_attn_o, fnet_norm, w_fnet_o, mlp_norm, w_up, w_down):
    y_prompt = trunk(x_prompt, attn_norm, w_q_a, q_a_norm, w_q_b, w_kv_a, kv_a_norm, w_kv_b, q_norm,
                     k_norm, w_attn_o, fnet_norm, w_fnet_o, mlp_norm, w_up, w_down)
    y_sample = trunk(x_sample, attn_norm, w_q_a, q_a_norm, w_q_b, w_kv_a, kv_a_norm, w_kv_b, q_norm,
                     k_norm, w_attn_o, fnet_norm, w_fnet_o, mlp_norm, w_up, w_down)
    return (y_prompt, y_sample)
```

```python
import functools
import math

import numpy as np
import jax
import jax.numpy as jnp
from jax import lax
from jax.experimental import pallas as pl
from jax.experimental.pallas import tpu as pltpu

F32 = jnp.float32
BF16 = jnp.bfloat16

N_HEADS = 16
QK_NOPE_DIM = 128
QK_ROPE_DIM = 64
QK_HEAD_DIM = QK_NOPE_DIM + QK_ROPE_DIM
V_HEAD_DIM = 128
HEAD_PAD = 256
ROPE_THETA = 10000.0
FNET_GROUPS = 8
EPS = 1e-6
FFT_N2 = 128
LANES = 128
VMEM_CAP = 60 * 1024 * 1024


def _vmem_limit(nbytes):
    return int(min(max(nbytes, 16 * 1024 * 1024), VMEM_CAP))


def _params(sem, nbytes):
    return pltpu.CompilerParams(dimension_semantics=sem, vmem_limit_bytes=_vmem_limit(nbytes))


def _rms(xf, g):
    return xf * lax.rsqrt(jnp.mean(xf * xf, axis=-1, keepdims=True) + EPS) * g


def _split(a):
    hi = a.astype(BF16)
    lo = (a - hi.astype(F32)).astype(BF16)
    return hi, lo


def _dot(a, b):
    return jnp.dot(a, b, preferred_element_type=F32)


def _dot3(a_hi, a_lo, b_hi, b_lo):
    return _dot(a_hi, b_hi) + (_dot(a_lo, b_hi) + _dot(a_hi, b_lo))


def _rope_dup(pe, cos_t, sin_t):
    return pe * cos_t + pltpu.roll(pe, 32, axis=1) * sin_t


def _qproj_kernel(x_ref, ga_ref, wqa_ref, gqa_ref, wqb_ref, gq_ref, cos_ref, sin_ref, q_ref, *, qscale):
    h = _rms(x_ref[...], ga_ref[...]).astype(BF16)
    cq = _rms(_dot(h, wqa_ref[...]), gqa_ref[...]).astype(BF16)
    cos_t = cos_ref[...]
    sin_t = sin_ref[...]
    g_nope = gq_ref[:, :QK_NOPE_DIM]
    g_pe = gq_ref[:, QK_NOPE_DIM:]
    for hh in range(N_HEADS):
        qh = _dot(cq, wqb_ref[:, hh * HEAD_PAD:(hh + 1) * HEAD_PAD])
        nope = qh[:, :QK_NOPE_DIM]
        pe = qh[:, QK_NOPE_DIM:]
        ss = jnp.sum(nope * nope + 0.5 * (pe * pe), axis=-1, keepdims=True)
        inv = lax.rsqrt(ss * (1.0 / QK_HEAD_DIM) + EPS) * qscale
        q_ref[hh, :, :QK_NOPE_DIM] = (nope * inv * g_nope).astype(BF16)
        q_ref[hh, :, QK_NOPE_DIM:] = _rope_dup(pe * inv * g_pe, cos_t, sin_t).astype(BF16)


def _kvproj_kernel(x_ref, ga_ref, wkva_ref, gkva_ref, wkvb_ref, gk_ref, cos_ref, sin_ref, k_ref, v_ref,
                   *, kv_rank):
    h = _rms(x_ref[...], ga_ref[...]).astype(BF16)
    ckv = _dot(h, wkva_ref[...])
    c = _rms(ckv[:, :kv_rank], gkva_ref[...]).astype(BF16)
    pe = ckv[:, kv_rank:]
    pe_ss = 0.5 * jnp.sum(pe * pe, axis=-1, keepdims=True)
    g_nope = gk_ref[:, :QK_NOPE_DIM]
    r = _rope_dup(pe * gk_ref[:, QK_NOPE_DIM:], cos_ref[...], sin_ref[...])
    for hh in range(N_HEADS):
        kvh = _dot(c, wkvb_ref[:, hh * 256:(hh + 1) * 256])
        k_nope = kvh[:, :QK_NOPE_DIM]
        ss = jnp.sum(k_nope * k_nope, axis=-1, keepdims=True) + pe_ss
        inv = lax.rsqrt(ss * (1.0 / QK_HEAD_DIM) + EPS)
        k_ref[hh, :, :QK_NOPE_DIM] = (k_nope * inv * g_nope).astype(BF16)
        k_ref[hh, :, QK_NOPE_DIM:] = (r * inv).astype(BF16)
        v_ref[hh] = kvh[:, QK_NOPE_DIM:].astype(BF16)


def _const_spec(shape):
    return pl.BlockSpec(shape, lambda i: (0,) * len(shape))


def _mla_project(x, ga, wqa, gqa, wqb, gq, wkva, gkva, wkvb, gk, cos_t, sin_t, tm):
    T, D = x.shape
    q_rank = wqa.shape[1]
    kv_rank = gkva.shape[1]
    qscale = QK_HEAD_DIM ** -0.5 * math.log2(math.e)
    row = lambda i: (i, 0)
    head_rows = lambda i: (0, i, 0)
    tab = pl.BlockSpec((tm, LANES), row)
    q = pl.pallas_call(
        functools.partial(_qproj_kernel, qscale=qscale),
        grid=(T // tm,),
        in_specs=[pl.BlockSpec((tm, D), row), _const_spec((1, D)), _const_spec(wqa.shape),
                  _const_spec((1, q_rank)), _const_spec(wqb.shape), _const_spec((1, HEAD_PAD)), tab, tab],
        out_specs=pl.BlockSpec((N_HEADS, tm, HEAD_PAD), head_rows),
        out_shape=jax.ShapeDtypeStruct((N_HEADS, T, HEAD_PAD), BF16),
        compiler_params=_params(("parallel",), 2 * (tm * D * 4 + wqa.size * 2 + wqb.size * 2
                                                    + N_HEADS * tm * HEAD_PAD * 2) + 8 * tm * D * 4),
    )(x, ga, wqa, gqa, wqb, gq, cos_t, sin_t)
    k, v = pl.pallas_call(
        functools.partial(_kvproj_kernel, kv_rank=kv_rank),
        grid=(T // tm,),
        in_specs=[pl.BlockSpec((tm, D), row), _const_spec((1, D)), _const_spec(wkva.shape),
                  _const_spec((1, kv_rank)), _const_spec(wkvb.shape), _const_spec((1, HEAD_PAD)), tab, tab],
        out_specs=[pl.BlockSpec((N_HEADS, tm, HEAD_PAD), head_rows),
                   pl.BlockSpec((N_HEADS, tm, V_HEAD_DIM), head_rows)],
        out_shape=[jax.ShapeDtypeStruct((N_HEADS, T, HEAD_PAD), BF16),
                   jax.ShapeDtypeStruct((N_HEADS, T, V_HEAD_DIM), BF16)],
        compiler_params=_params(("parallel",), 2 * (tm * D * 4 + wkva.size * 2 + wkvb.size * 2
                                                    + N_HEADS * tm * (HEAD_PAD + V_HEAD_DIM) * 2)
                                + 8 * tm * D * 4),
    )(x, ga, wkva, gkva, wkvb, gk, cos_t, sin_t)
    return q, k, v


def _flash_kernel(q_ref, k_ref, v_ref, o_ref, *, tk):
    q = q_ref[0]
    tq = q.shape[0]
    n_chunks = k_ref.shape[1] // tk

    def body(j, carry):
        m, l, acc = carry
        start = pl.multiple_of(j * tk, tk)
        kc = k_ref[0, pl.ds(start, tk), :]
        vc = v_ref[0, pl.ds(start, tk), :]
        s = lax.dot_general(q, kc, (((1,), (1,)), ((), ())), preferred_element_type=F32)
        m_new = jnp.maximum(m, jnp.max(s, axis=-1, keepdims=True))
        alpha = jnp.exp2(m - m_new)
        p = jnp.exp2(s - m_new)
        l = alpha * l + jnp.sum(p, axis=-1, keepdims=True)
        acc = alpha * acc + _dot(p.astype(BF16), vc)
        return m_new, l, acc

    m0 = jnp.full((tq, 1), -jnp.inf, F32)
    l0 = jnp.zeros((tq, 1), F32)
    acc0 = jnp.zeros((tq, V_HEAD_DIM), F32)
    _, l, acc = lax.fori_loop(0, n_chunks, body, (m0, l0, acc0))
    o_ref[...] = (acc / l).astype(BF16)


def _attention(q, k, v, o_shape, seq_len, n_seq, tok_off, tq, tk):
    nq = seq_len // tq
    seq0 = tok_off // seq_len
    q0 = tok_off // tq
    return pl.pallas_call(
        functools.partial(_flash_kernel, tk=tk),
        grid=(n_seq, N_HEADS, nq),
        in_specs=[pl.BlockSpec((1, tq, HEAD_PAD), lambda b, h, i: (h, q0 + b * nq + i, 0)),
                  pl.BlockSpec((1, seq_len, HEAD_PAD), lambda b, h, i: (h, seq0 + b, 0)),
                  pl.BlockSpec((1, seq_len, V_HEAD_DIM), lambda b, h, i: (h, seq0 + b, 0))],
        out_specs=pl.BlockSpec((tq, V_HEAD_DIM), lambda b, h, i: (b * nq + i, h)),
        out_shape=jax.ShapeDtypeStruct(o_shape, BF16),
        compiler_params=_params(("parallel", "parallel", "arbitrary"),
                                2 * seq_len * (HEAD_PAD + V_HEAD_DIM) * 2 + 8 * tq * tk * 4
                                + 4 * tq * HEAD_PAD * 2),
    )(q, k, v)


def _fnet_stage1_kernel(x_ref, g_ref, wch_ref, wcl_ref, w1h_ref, w1l_ref, tw_ref, ar_ref, ai_ref, u_ref,
                        *, n1, gdim):
    h = _rms(x_ref[...], g_ref[...])
    h_hi, h_lo = _split(h)
    wch = wch_ref[...]
    wcl = wcl_ref[...]
    for gi in range(FNET_GROUPS):
        cols = slice(gi * gdim, (gi + 1) * gdim)
        u = _dot3(h_hi[:, cols], h_lo[:, cols], wch, wcl)
        u_ref[:n1, cols] = u[:, :gdim]
        u_ref[n1:, cols] = u[:, gdim:]
    u_hi, u_lo = _split(u_ref[...])
    a = _dot3(w1h_ref[...], w1l_ref[...], u_hi, u_lo)
    a_re = a[:n1]
    a_im = a[n1:]
    tw_c = tw_ref[0, :, 0:1]
    tw_s = tw_ref[0, :, 1:2]
    ar_ref[...] = a_re * tw_c + a_im * tw_s
    ai_ref[...] = a_im * tw_c - a_re * tw_s


def _fnet_stage2_kernel(ar_ref, ai_ref, w2h_ref, w2l_ref, z_ref):
    a = jnp.concatenate([ar_ref[...], ai_ref[...]], axis=0)
    a_hi, a_lo = _split(a)
    z_ref[...] = _dot3(w2h_ref[...], w2l_ref[...], a_hi, a_lo).astype(BF16)


def _dft_tables(n1, n2, gdim):
    def cs(rows, cols, period):
        ang = 2.0 * np.pi * ((np.arange(rows)[:, None] * np.arange(cols)[None, :]) % period) / period
        return np.cos(ang), np.sin(ang)
    cc, sc = cs(gdim, gdim, gdim)
    w_ch = np.concatenate([cc, -sc], axis=1) / np.sqrt(gdim)
    c1, s1 = cs(n1, n1, n1)
    w1 = np.block([[c1, s1], [-s1, c1]]) / np.sqrt(n1)
    c2, s2 = cs(n2, n2, n2)
    w2 = np.concatenate([c2, s2], axis=1) / np.sqrt(n2)
    twc, tws = cs(n2, n1, n1 * n2)
    tw = np.zeros((n2, n1, LANES), np.float64)
    tw[:, :, 0] = twc
    tw[:, :, 1] = tws
    return [jnp.asarray(t, F32) for t in (w_ch, w1, w2, tw)]


def _fnet_mix(x, g, seq_len, n_seq, tok_off):
    T, D = x.shape
    n2 = FFT_N2
    n1 = seq_len // n2
    gdim = D // FNET_GROUPS
    w_ch, w1, w2, tw = _dft_tables(n1, n2, gdim)
    wch, wcl = _split(w_ch)
    w1h, w1l = _split(w1)
    w2h, w2l = _split(w2)
    rows = n_seq * seq_len
    r0 = tok_off // seq_len
    x2 = x.reshape(T // n2, n2 * D)
    blk = pl.BlockSpec((n1, D), lambda b, j: (r0 + b, j))
    ar, ai = pl.pallas_call(
        functools.partial(_fnet_stage1_kernel, n1=n1, gdim=gdim),
        grid=(n_seq, n2),
        in_specs=[blk, pl.BlockSpec((1, D), lambda b, j: (0, 0)),
                  pl.BlockSpec(wch.shape, lambda b, j: (0, 0)), pl.BlockSpec(wcl.shape, lambda b, j: (0, 0)),
                  pl.BlockSpec(w1h.shape, lambda b, j: (0, 0)), pl.BlockSpec(w1l.shape, lambda b, j: (0, 0)),
                  pl.BlockSpec((1, n1, LANES), lambda b, j: (j, 0, 0))],
        out_specs=[pl.BlockSpec((n1, D), lambda b, j: (b, j))] * 2,
        out_shape=[jax.ShapeDtypeStruct((rows // n2, n2 * D), F32)] * 2,
        scratch_shapes=[pltpu.VMEM((2 * n1, D), F32)],
        compiler_params=_params(("parallel", "parallel"), 24 * n1 * D * 4 + 16 * 1024 * 1024),
    )(x2, g, wch, wcl, w1h, w1l, tw)
    ar = ar.reshape(rows, D)
    ai = ai.reshape(rows, D)
    tile = pl.BlockSpec((n2, D), lambda b, j: (b * n1 + j, 0))
    z = pl.pallas_call(
        _fnet_stage2_kernel,
        grid=(n_seq, n1),
        in_specs=[tile, tile, pl.BlockSpec(w2h.shape, lambda b, j: (0, 0)),
                  pl.BlockSpec(w2l.shape, lambda b, j: (0, 0))],
        out_specs=pl.BlockSpec((None, n2, D), lambda b, j: (b, 0, j)),
        out_shape=jax.ShapeDtypeStruct((n_seq, n2, n1 * D), BF16),
        compiler_params=_params(("parallel", "parallel"), 24 * n2 * D * 4 + 16 * 1024 * 1024),
    )(ar, ai, w2h, w2l)
    return z.reshape(rows, D)


def _oproj_kernel(x_ref, a_ref, w_ref, o_ref):
    o_ref[...] = x_ref[...] + _dot(a_ref[...], w_ref[...])


def _oproj_residual(x, a, w, tm):
    T, D = x.shape
    row = lambda i: (i, 0)
    return pl.pallas_call(
        _oproj_kernel,
        grid=(T // tm,),
        in_specs=[pl.BlockSpec((tm, D), row), pl.BlockSpec((tm, a.shape[1]), row), _const_spec(w.shape)],
        out_specs=pl.BlockSpec((tm, D), row),
        out_shape=jax.ShapeDtypeStruct((T, D), F32),
        compiler_params=_params(("parallel",), 2 * (2 * tm * D * 4 + tm * a.shape[1] * 2 + w.size * 2)
                                + 2 * tm * D * 4),
    )(x, a, w)


def _mlp_kernel(x_ref, g_ref, wup_ref, wdn_ref, o_ref, xn_ref):
    @pl.when(pl.program_id(1) == 0)
    def _():
        x = x_ref[...]
        xn_ref[...] = _rms(x, g_ref[...]).astype(BF16)
        o_ref[...] = x

    hcol = jnp.square(jnp.maximum(_dot(xn_ref[...], wup_ref[...]), 0.0)).astype(BF16)
    o_ref[...] += _dot(hcol, wdn_ref[...])


def _mlp(x, g, w_up, w_down, tm, tf):
    T, D = x.shape
    dff = w_up.shape[1]
    return pl.pallas_call(
        _mlp_kernel,
        grid=(T // tm, dff // tf),
        in_specs=[pl.BlockSpec((tm, D), lambda i, f: (i, 0)), pl.BlockSpec((1, D), lambda i, f: (0, 0)),
                  pl.BlockSpec((D, tf), lambda i, f: (0, f)), pl.BlockSpec((tf, D), lambda i, f: (f, 0))],
        out_specs=pl.BlockSpec((tm, D), lambda i, f: (i, 0)),
        out_shape=jax.ShapeDtypeStruct((T, D), F32),
        scratch_shapes=[pltpu.VMEM((tm, D), BF16)],
        compiler_params=_params(("parallel", "arbitrary"),
                                2 * (2 * tm * D * 4 + 2 * D * tf * 2) + tm * D * 2 + 3 * tm * tf * 4),
    )(x, g, w_up, w_down)


def _dup_rope_cols(w):
    return jnp.concatenate([w, w], axis=-1)


def _pad_head_vec(g):
    return jnp.concatenate([g[:QK_NOPE_DIM], _dup_rope_cols(g[QK_NOPE_DIM:])])[None, :].astype(F32)


def _rope_tables(seq_lens):
    inv_freq = ROPE_THETA ** (-jnp.arange(0, QK_ROPE_DIM, 2, dtype=F32) / QK_ROPE_DIM)
    pos = jnp.concatenate([jnp.arange(s, dtype=F32) for s in seq_lens])
    ang = pos[:, None] * inv_freq[None, :]
    c, s = jnp.cos(ang), jnp.sin(ang)
    z = jnp.zeros_like(c)
    return jnp.concatenate([c, c, z, z], axis=1), jnp.concatenate([-s, s, z, z], axis=1)


def _pick_tile(n, target):
    t = min(n, target)
    while n % t:
        t //= 2
    return t


def kernel(x_prompt, x_sample, attn_norm, w_q_a, q_a_norm, w_q_b, w_kv_a, kv_a_norm, w_kv_b, q_norm, k_norm,
           w_attn_o, fnet_norm, w_fnet_o, mlp_norm, w_up, w_down):
    bp, sp, D = x_prompt.shape
    bs, ss, _ = x_sample.shape
    tp, ts = bp * sp, bs * ss
    T = tp + ts
    depth = mlp_norm.shape[0]
    kv_rank = kv_a_norm.shape[1]
    q_rank = q_a_norm.shape[1]
    x = jnp.concatenate([x_prompt.reshape(tp, D), x_sample.reshape(ts, D)], axis=0)
    cos_t, sin_t = _rope_tables([sp] * bp + [ss] * bs)

    tm = _pick_tile(math.gcd(tp, ts), 256)
    tm_mlp = _pick_tile(math.gcd(tp, ts), 512)
    tf = _pick_tile(w_up.shape[2], 1024)
    row2 = lambda v: v[None, :].astype(F32)

    for i in range(depth):
        j = i // 2
        if i % 2 == 0:
            wqb = w_q_b[j].reshape(q_rank, N_HEADS, QK_HEAD_DIM)
            wqb = jnp.concatenate([wqb[..., :QK_NOPE_DIM], _dup_rope_cols(wqb[..., QK_NOPE_DIM:])], axis=-1)
            wqb = wqb.reshape(q_rank, N_HEADS * HEAD_PAD).astype(BF16)
            wkva = jnp.concatenate([w_kv_a[j][:, :kv_rank], _dup_rope_cols(w_kv_a[j][:, kv_rank:])], axis=1)
            q, k, v = _mla_project(
                x, row2(attn_norm[j]), w_q_a[j].astype(BF16), row2(q_a_norm[j]), wqb, _pad_head_vec(q_norm[j]),
                wkva.astype(BF16), row2(kv_a_norm[j]), w_kv_b[j].astype(BF16), _pad_head_vec(k_norm[j]),
                cos_t, sin_t, tm)
            o_p = _attention(q, k, v, (tp, N_HEADS * V_HEAD_DIM), sp, bp, 0,
                             _pick_tile(sp, 256), _pick_tile(sp, 512))
            o_s = _attention(q, k, v, (ts, N_HEADS * V_HEAD_DIM), ss, bs, tp,
                             _pick_tile(ss, 256), _pick_tile(ss, 512))
            a = jnp.concatenate([o_p, o_s], axis=0)
            w_o = w_attn_o[j]
        else:
            g = row2(fnet_norm[j])
            a = jnp.concatenate([_fnet_mix(x, g, sp, bp, 0), _fnet_mix(x, g, ss, bs, tp)], axis=0)
            w_o = w_fnet_o[j]
        x = _oproj_residual(x, a, w_o.astype(BF16), tm_mlp)
        x = _mlp(x, row2(mlp_norm[i]), w_up[i].astype(BF16), w_down[i].astype(BF16), tm_mlp, tf)

    return x[:tp].reshape(bp, sp, D), x[tp:].reshape(bs, ss, D)
```

```python
import functools
import math

import numpy as np
import jax
import jax.numpy as jnp
from jax import lax
from jax.experimental import pallas as pl
from jax.experimental.pallas import tpu as pltpu

F32 = jnp.float32
BF16 = jnp.bfloat16

N_HEADS = 16
QK_NOPE_DIM = 128
QK_ROPE_DIM = 64
QK_HEAD_DIM = QK_NOPE_DIM + QK_ROPE_DIM
V_HEAD_DIM = 128
HEAD_PAD = 256
ROPE_THETA = 10000.0
FNET_GROUPS = 8
EPS = 1e-6
FFT_N2 = 128
LANES = 128
VMEM_CAP = 60 * 1024 * 1024


def _vmem_limit(nbytes):
    return int(min(max(nbytes, 16 * 1024 * 1024), VMEM_CAP))


def _params(sem, nbytes):
    return pltpu.CompilerParams(dimension_semantics=sem, vmem_limit_bytes=_vmem_limit(nbytes))


def _rms(xf, g):
    return xf * lax.rsqrt(jnp.mean(xf * xf, axis=-1, keepdims=True) + EPS) * g


def _split(a):
    hi = a.astype(BF16)
    lo = (a - hi.astype(F32)).astype(BF16)
    return hi, lo


def _dot(a, b):
    return jnp.dot(a, b, preferred_element_type=F32)


def _dot3(a_hi, a_lo, b_hi, b_lo):
    return _dot(a_hi, b_hi) + (_dot(a_lo, b_hi) + _dot(a_hi, b_lo))


def _rope_dup(pe, cos_t, sin_t):
    return pe * cos_t + pltpu.roll(pe, 32, axis=1) * sin_t


def _qproj_kernel(x_ref, ga_ref, wqa_ref, gqa_ref, wqb_ref, gq_ref, cos_ref, sin_ref, q_ref, *, qscale):
    h = _rms(x_ref[...], ga_ref[...]).astype(BF16)
    cq = _rms(_dot(h, wqa_ref[...]), gqa_ref[...]).astype(BF16)
    cos_t = cos_ref[...]
    sin_t = sin_ref[...]
    g_nope = gq_ref[:, :QK_NOPE_DIM]
    g_pe = gq_ref[:, QK_NOPE_DIM:]
    for hh in range(N_HEADS):
        qh = _dot(cq, wqb_ref[:, hh * HEAD_PAD:(hh + 1) * HEAD_PAD])
        nope = qh[:, :QK_NOPE_DIM]
        pe = qh[:, QK_NOPE_DIM:]
        ss = jnp.sum(nope * nope + 0.5 * (pe * pe), axis=-1, keepdims=True)
        inv = lax.rsqrt(ss * (1.0 / QK_HEAD_DIM) + EPS) * qscale
        q_ref[hh, :QK_NOPE_DIM, :] = (nope * inv * g_nope).T.astype(BF16)
        q_ref[hh, QK_NOPE_DIM:, :] = _rope_dup(pe * inv * g_pe, cos_t, sin_t).T.astype(BF16)


def _kvproj_kernel(x_ref, ga_ref, wkva_ref, gkva_ref, wkvb_ref, gk_ref, cos_ref, sin_ref, k_ref, v_ref,
                   *, kv_rank):
    h = _rms(x_ref[...], ga_ref[...]).astype(BF16)
    ckv = _dot(h, wkva_ref[...])
    c = _rms(ckv[:, :kv_rank], gkva_ref[...]).astype(BF16)
    pe = ckv[:, kv_rank:]
    pe_ss = 0.5 * jnp.sum(pe * pe, axis=-1, keepdims=True)
    g_nope = gk_ref[:, :QK_NOPE_DIM]
    r = _rope_dup(pe * gk_ref[:, QK_NOPE_DIM:], cos_ref[...], sin_ref[...])
    for hh in range(N_HEADS):
        kvh = _dot(c, wkvb_ref[:, hh * 256:(hh + 1) * 256])
        k_nope = kvh[:, :QK_NOPE_DIM]
        ss = jnp.sum(k_nope * k_nope, axis=-1, keepdims=True) + pe_ss
        inv = lax.rsqrt(ss * (1.0 / QK_HEAD_DIM) + EPS)
        k_ref[hh, :, :QK_NOPE_DIM] = (k_nope * inv * g_nope).astype(BF16)
        k_ref[hh, :, QK_NOPE_DIM:] = (r * inv).astype(BF16)
        v_ref[hh] = kvh[:, QK_NOPE_DIM:].T.astype(BF16)


def _const_spec(shape):
    return pl.BlockSpec(shape, lambda i: (0,) * len(shape))


def _mla_project(x, ga, wqa, gqa, wqb, gq, wkva, gkva, wkvb, gk, cos_t, sin_t, tm):
    T, D = x.shape
    q_rank = wqa.shape[1]
    kv_rank = gkva.shape[1]
    qscale = QK_HEAD_DIM ** -0.5 * math.log2(math.e)
    row = lambda i: (i, 0)
    head_rows = lambda i: (0, i, 0)
    head_cols = lambda i: (0, 0, i)
    tab = pl.BlockSpec((tm, LANES), row)
    qt = pl.pallas_call(
        functools.partial(_qproj_kernel, qscale=qscale),
        grid=(T // tm,),
        in_specs=[pl.BlockSpec((tm, D), row), _const_spec((1, D)), _const_spec(wqa.shape),
                  _const_spec((1, q_rank)), _const_spec(wqb.shape), _const_spec((1, HEAD_PAD)), tab, tab],
        out_specs=pl.BlockSpec((N_HEADS, HEAD_PAD, tm), head_cols),
        out_shape=jax.ShapeDtypeStruct((N_HEADS, HEAD_PAD, T), BF16),
        compiler_params=_params(("parallel",), 2 * (tm * D * 4 + wqa.size * 2 + wqb.size * 2
                                                    + N_HEADS * tm * HEAD_PAD * 2) + 8 * tm * D * 4),
    )(x, ga, wqa, gqa, wqb, gq, cos_t, sin_t)
    k, vt = pl.pallas_call(
        functools.partial(_kvproj_kernel, kv_rank=kv_rank),
        grid=(T // tm,),
        in_specs=[pl.BlockSpec((tm, D), row), _const_spec((1, D)), _const_spec(wkva.shape),
                  _const_spec((1, kv_rank)), _const_spec(wkvb.shape), _const_spec((1, HEAD_PAD)), tab, tab],
        out_specs=[pl.BlockSpec((N_HEADS, tm, HEAD_PAD), head_rows),
                   pl.BlockSpec((N_HEADS, V_HEAD_DIM, tm), head_cols)],
        out_shape=[jax.ShapeDtypeStruct((N_HEADS, T, HEAD_PAD), BF16),
                   jax.ShapeDtypeStruct((N_HEADS, V_HEAD_DIM, T), BF16)],
        compiler_params=_params(("parallel",), 2 * (tm * D * 4 + wkva.size * 2 + wkvb.size * 2
                                                    + N_HEADS * tm * (HEAD_PAD + V_HEAD_DIM) * 2)
                                + 8 * tm * D * 4),
    )(x, ga, wkva, gkva, wkvb, gk, cos_t, sin_t)
    return qt, k, vt


def _flash_kernel(qt_ref, k_ref, vt_ref, o_ref, s_ref, *, tk, chunks_per_step):
    qt = qt_ref[0]
    tq = qt.shape[1]
    n_chunks = k_ref.shape[1] // tk

    def scores(j, slot):
        s_ref[slot] = _dot(k_ref[0, pl.ds(pl.multiple_of(j * tk, tk), tk), :], qt)

    def absorb(j, slot, m, l, acc):
        vt = vt_ref[0, :, pl.ds(pl.multiple_of(j * tk, tk), tk)]
        m_new = jnp.maximum(m, jnp.max(s_ref[slot], axis=0, keepdims=True))
        p = jnp.exp2(s_ref[slot] - m_new)
        alpha = jnp.exp2(m - m_new)
        l = alpha * l + jnp.sum(p, axis=0, keepdims=True)
        acc = alpha * acc + _dot(vt, p.astype(BF16))
        return m_new, l, acc

    def body(step, carry):
        m, l, acc = carry
        for c in range(chunks_per_step):
            j = step * chunks_per_step + c
            scores(jnp.minimum(j + 1, n_chunks - 1), (c + 1) % 2)
            m, l, acc = absorb(j, c % 2, m, l, acc)
        return m, l, acc

    m0 = jnp.full((1, tq), -jnp.inf, F32)
    l0 = jnp.zeros((1, tq), F32)
    acc0 = jnp.zeros((V_HEAD_DIM, tq), F32)
    scores(0, 0)
    _, l, acc = lax.fori_loop(0, n_chunks // chunks_per_step, body, (m0, l0, acc0))
    o_ref[...] = (acc / l).T.astype(BF16)


def _attention(qt, k, vt, o_shape, seq_len, n_seq, tok_off, tq, tk):
    nq = seq_len // tq
    seq0 = tok_off // seq_len
    q0 = tok_off // tq
    chunks_per_step = 2
    assert seq_len % (tk * chunks_per_step) == 0
    return pl.pallas_call(
        functools.partial(_flash_kernel, tk=tk, chunks_per_step=chunks_per_step),
        grid=(n_seq, N_HEADS, nq),
        in_specs=[pl.BlockSpec((1, HEAD_PAD, tq), lambda b, h, i: (h, 0, q0 + b * nq + i)),
                  pl.BlockSpec((1, seq_len, HEAD_PAD), lambda b, h, i: (h, seq0 + b, 0)),
                  pl.BlockSpec((1, V_HEAD_DIM, seq_len), lambda b, h, i: (h, 0, seq0 + b))],
        out_specs=pl.BlockSpec((tq, V_HEAD_DIM), lambda b, h, i: (b * nq + i, h)),
        out_shape=jax.ShapeDtypeStruct(o_shape, BF16),
        scratch_shapes=[pltpu.VMEM((2, tk, tq), F32)],
        compiler_params=_params(("parallel", "parallel", "arbitrary"),
                                2 * seq_len * (HEAD_PAD + V_HEAD_DIM) * 2 + 8 * tq * tk * 4
                                + 4 * tq * HEAD_PAD * 2),
    )(qt, k, vt)


def _fnet_stage1_kernel(x_ref, g_ref, wch_ref, wcl_ref, w1h_ref, w1l_ref, tw_ref, ar_ref, ai_ref, u_ref,
                        *, n1, gdim):
    h = _rms(x_ref[...], g_ref[...])
    h_hi, h_lo = _split(h)
    wch = wch_ref[...]
    wcl = wcl_ref[...]
    for gi in range(FNET_GROUPS):
        cols = slice(gi * gdim, (gi + 1) * gdim)
        u = _dot3(h_hi[:, cols], h_lo[:, cols], wch, wcl)
        u_ref[:n1, cols] = u[:, :gdim]
        u_ref[n1:, cols] = u[:, gdim:]
    u_hi, u_lo = _split(u_ref[...])
    a = _dot3(w1h_ref[...], w1l_ref[...], u_hi, u_lo)
    a_re = a[:n1]
    a_im = a[n1:]
    tw_c = tw_ref[0, :, 0:1]
    tw_s = tw_ref[0, :, 1:2]
    ar_ref[...] = a_re * tw_c + a_im * tw_s
    ai_ref[...] = a_im * tw_c - a_re * tw_s


def _fnet_stage2_kernel(ar_ref, ai_ref, w2h_ref, w2l_ref, z_ref):
    a = jnp.concatenate([ar_ref[...], ai_ref[...]], axis=0)
    a_hi, a_lo = _split(a)
    z_ref[...] = _dot3(w2h_ref[...], w2l_ref[...], a_hi, a_lo).astype(BF16)


def _dft_tables(n1, n2, gdim):
    def cs(rows, cols, period):
        ang = 2.0 * np.pi * ((np.arange(rows)[:, None] * np.arange(cols)[None, :]) % period) / period
        return np.cos(ang), np.sin(ang)
    cc, sc = cs(gdim, gdim, gdim)
    w_ch = np.concatenate([cc, -sc], axis=1) / np.sqrt(gdim)
    c1, s1 = cs(n1, n1, n1)
    w1 = np.block([[c1, s1], [-s1, c1]]) / np.sqrt(n1)
    c2, s2 = cs(n2, n2, n2)
    w2 = np.concatenate([c2, s2], axis=1) / np.sqrt(n2)
    twc, tws = cs(n2, n1, n1 * n2)
    tw = np.zeros((n2, n1, LANES), np.float64)
    tw[:, :, 0] = twc
    tw[:, :, 1] = tws
    return [jnp.asarray(t, F32) for t in (w_ch, w1, w2, tw)]


def _fnet_mix(x, g, seq_len, n_seq, tok_off):
    T, D = x.shape
    n2 = FFT_N2
    n1 = seq_len // n2
    gdim = D // FNET_GROUPS
    w_ch, w1, w2, tw = _dft_tables(n1, n2, gdim)
    wch, wcl = _split(w_ch)
    w1h, w1l = _split(w1)
    w2h, w2l = _split(w2)
    rows = n_seq * seq_len
    r0 = tok_off // seq_len
    x2 = x.reshape(T // n2, n2 * D)
    blk = pl.BlockSpec((n1, D), lambda b, j: (r0 + b, j))
    ar, ai = pl.pallas_call(
        functools.partial(_fnet_stage1_kernel, n1=n1, gdim=gdim),
        grid=(n_seq, n2),
        in_specs=[blk, pl.BlockSpec((1, D), lambda b, j: (0, 0)),
                  pl.BlockSpec(wch.shape, lambda b, j: (0, 0)), pl.BlockSpec(wcl.shape, lambda b, j: (0, 0)),
                  pl.BlockSpec(w1h.shape, lambda b, j: (0, 0)), pl.BlockSpec(w1l.shape, lambda b, j: (0, 0)),
                  pl.BlockSpec((1, n1, LANES), lambda b, j: (j, 0, 0))],
        out_specs=[pl.BlockSpec((n1, D), lambda b, j: (b, j))] * 2,
        out_shape=[jax.ShapeDtypeStruct((rows // n2, n2 * D), F32)] * 2,
        scratch_shapes=[pltpu.VMEM((2 * n1, D), F32)],
        compiler_params=_params(("parallel", "parallel"), 24 * n1 * D * 4 + 16 * 1024 * 1024),
    )(x2, g, wch, wcl, w1h, w1l, tw)
    ar = ar.reshape(rows, D)
    ai = ai.reshape(rows, D)
    tile = pl.BlockSpec((n2, D), lambda b, j: (b * n1 + j, 0))
    z = pl.pallas_call(
        _fnet_stage2_kernel,
        grid=(n_seq, n1),
        in_specs=[tile, tile, pl.BlockSpec(w2h.shape, lambda b, j: (0, 0)),
                  pl.BlockSpec(w2l.shape, lambda b, j: (0, 0))],
        out_specs=pl.BlockSpec((None, n2, D), lambda b, j: (b, 0, j)),
        out_shape=jax.ShapeDtypeStruct((n_seq, n2, n1 * D), BF16),
        compiler_params=_params(("parallel", "parallel"), 24 * n2 * D * 4 + 16 * 1024 * 1024),
    )(ar, ai, w2h, w2l)
    return z.reshape(rows, D)


def _oproj_kernel(x_ref, a_ref, w_ref, o_ref):
    o_ref[...] = x_ref[...] + _dot(a_ref[...], w_ref[...])


def _oproj_residual(x, a, w, tm):
    T, D = x.shape
    row = lambda i: (i, 0)
    return pl.pallas_call(
        _oproj_kernel,
        grid=(T // tm,),
        in_specs=[pl.BlockSpec((tm, D), row), pl.BlockSpec((tm, a.shape[1]), row), _const_spec(w.shape)],
        out_specs=pl.BlockSpec((tm, D), row),
        out_shape=jax.ShapeDtypeStruct((T, D), F32),
        compiler_params=_params(("parallel",), 2 * (2 * tm * D * 4 + tm * a.shape[1] * 2 + w.size * 2)
                                + 2 * tm * D * 4),
    )(x, a, w)


def _mlp_kernel(x_ref, g_ref, wup_ref, wdn_ref, o_ref, xn_ref):
    @pl.when(pl.program_id(1) == 0)
    def _():
        x = x_ref[...]
        xn_ref[...] = _rms(x, g_ref[...]).astype(BF16)
        o_ref[...] = x

    hcol = jnp.square(jnp.maximum(_dot(xn_ref[...], wup_ref[...]), 0.0)).astype(BF16)
    o_ref[...] += _dot(hcol, wdn_ref[...])


def _mlp(x, g, w_up, w_down, tm, tf):
    T, D = x.shape
    dff = w_up.shape[1]
    return pl.pallas_call(
        _mlp_kernel,
        grid=(T // tm, dff // tf),
        in_specs=[pl.BlockSpec((tm, D), lambda i, f: (i, 0)), pl.BlockSpec((1, D), lambda i, f: (0, 0)),
                  pl.BlockSpec((D, tf), lambda i, f: (0, f)), pl.BlockSpec((tf, D), lambda i, f: (f, 0))],
        out_specs=pl.BlockSpec((tm, D), lambda i, f: (i, 0)),
        out_shape=jax.ShapeDtypeStruct((T, D), F32),
        scratch_shapes=[pltpu.VMEM((tm, D), BF16)],
        compiler_params=_params(("parallel", "arbitrary"),
                                2 * (2 * tm * D * 4 + 2 * D * tf * 2) + tm * D * 2 + 3 * tm * tf * 4),
    )(x, g, w_up, w_down)


def _dup_rope_cols(w):
    return jnp.concatenate([w, w], axis=-1)


def _pad_head_vec(g):
    return jnp.concatenate([g[:QK_NOPE_DIM], _dup_rope_cols(g[QK_NOPE_DIM:])])[None, :].astype(F32)


def _rope_tables(seq_lens):
    inv_freq = ROPE_THETA ** (-jnp.arange(0, QK_ROPE_DIM, 2, dtype=F32) / QK_ROPE_DIM)
    pos = jnp.concatenate([jnp.arange(s, dtype=F32) for s in seq_lens])
    ang = pos[:, None] * inv_freq[None, :]
    c, s = jnp.cos(ang), jnp.sin(ang)
    z = jnp.zeros_like(c)
    return jnp.concatenate([c, c, z, z], axis=1), jnp.concatenate([-s, s, z, z], axis=1)


def _pick_tile(n, target):
    t = min(n, target)
    while n % t:
        t //= 2
    return t


def kernel(x_prompt, x_sample, attn_norm, w_q_a, q_a_norm, w_q_b, w_kv_a, kv_a_norm, w_kv_b, q_norm, k_norm,
           w_attn_o, fnet_norm, w_fnet_o, mlp_norm, w_up, w_down):
    bp, sp, D = x_prompt.shape
    bs, ss, _ = x_sample.shape
    tp, ts = bp * sp, bs * ss
    T = tp + ts
    depth = mlp_norm.shape[0]
    kv_rank = kv_a_norm.shape[1]
    q_rank = q_a_norm.shape[1]
    x = jnp.concatenate([x_prompt.reshape(tp, D), x_sample.reshape(ts, D)], axis=0)
    cos_t, sin_t = _rope_tables([sp] * bp + [ss] * bs)

    tm = _pick_tile(math.gcd(tp, ts), 256)
    tm_mlp = _pick_tile(math.gcd(tp, ts), 512)
    tf = _pick_tile(w_up.shape[2], 1024)
    row2 = lambda v: v[None, :].astype(F32)

    for i in range(depth):
        j = i // 2
        if i % 2 == 0:
            wqb = w_q_b[j].reshape(q_rank, N_HEADS, QK_HEAD_DIM)
            wqb = jnp.concatenate([wqb[..., :QK_NOPE_DIM], _dup_rope_cols(wqb[..., QK_NOPE_DIM:])], axis=-1)
            wqb = wqb.reshape(q_rank, N_HEADS * HEAD_PAD).astype(BF16)
            wkva = jnp.concatenate([w_kv_a[j][:, :kv_rank], _dup_rope_cols(w_kv_a[j][:, kv_rank:])], axis=1)
            q, k, v = _mla_project(
                x, row2(attn_norm[j]), w_q_a[j].astype(BF16), row2(q_a_norm[j]), wqb, _pad_head_vec(q_norm[j]),
                wkva.astype(BF16), row2(kv_a_norm[j]), w_kv_b[j].astype(BF16), _pad_head_vec(k_norm[j]),
                cos_t, sin_t, tm)
            o_p = _attention(q, k, v, (tp, N_HEADS * V_HEAD_DIM), sp, bp, 0,
                             _pick_tile(sp, 512), _pick_tile(sp, 512))
            o_s = _attention(q, k, v, (ts, N_HEADS * V_HEAD_DIM), ss, bs, tp,
                             _pick_tile(ss, 512), _pick_tile(ss, 512))
            a = jnp.concatenate([o_p, o_s], axis=0)
            w_o = w_attn_o[j]
        else:
            g = row2(fnet_norm[j])
            a = jnp.concatenate([_fnet_mix(x, g, sp, bp, 0), _fnet_mix(x, g, ss, bs, tp)], axis=0)
            w_o = w_fnet_o[j]
        x = _oproj_residual(x, a, w_o.astype(BF16), tm_mlp)
        x = _mlp(x, row2(mlp_norm[i]), w_up[i].astype(BF16), w_down[i].astype(BF16), tm_mlp, tf)

    return x[:tp].reshape(bp, sp, D), x[tp:].reshape(bs, ss, D)
```

```python
import functools
import math

import numpy as np
import jax
import jax.numpy as jnp
from jax import lax
from jax.experimental import pallas as pl
from jax.experimental.pallas import tpu as pltpu

F32 = jnp.float32
BF16 = jnp.bfloat16

N_HEADS = 16
QK_NOPE_DIM = 128
QK_ROPE_DIM = 64
QK_HEAD_DIM = QK_NOPE_DIM + QK_ROPE_DIM
V_HEAD_DIM = 128
HEAD_PAD = 256
ROPE_THETA = 10000.0
FNET_GROUPS = 8
EPS = 1e-6
FFT_N2 = 128
LANES = 128
VMEM_CAP = 60 * 1024 * 1024


def _vmem_limit(nbytes):
    return int(min(max(nbytes, 16 * 1024 * 1024), VMEM_CAP))


def _params(sem, nbytes):
    return pltpu.CompilerParams(dimension_semantics=sem, vmem_limit_bytes=_vmem_limit(nbytes))


def _rms(xf, g):
    return xf * lax.rsqrt(jnp.mean(xf * xf, axis=-1, keepdims=True) + EPS) * g


def _split(a):
    hi = a.astype(BF16)
    lo = (a - hi.astype(F32)).astype(BF16)
    return hi, lo


def _dot(a, b):
    return jnp.dot(a, b, preferred_element_type=F32)


def _dot3(a_hi, a_lo, b_hi, b_lo):
    return _dot(a_hi, b_hi) + (_dot(a_lo, b_hi) + _dot(a_hi, b_lo))


def _rope_dup(pe, cos_t, sin_t):
    return pe * cos_t + pltpu.roll(pe, 32, axis=1) * sin_t


def _qproj_kernel(x_ref, ga_ref, wqa_ref, gqa_ref, wqb_ref, gq_ref, cos_ref, sin_ref, q_ref, *, qscale):
    h = _rms(x_ref[...], ga_ref[...]).astype(BF16)
    cq = _rms(_dot(h, wqa_ref[...]), gqa_ref[...]).astype(BF16)
    cos_t = cos_ref[...]
    sin_t = sin_ref[...]
    g_nope = gq_ref[:, :QK_NOPE_DIM]
    g_pe = gq_ref[:, QK_NOPE_DIM:]
    for hh in range(N_HEADS):
        qh = _dot(cq, wqb_ref[:, hh * HEAD_PAD:(hh + 1) * HEAD_PAD])
        nope = qh[:, :QK_NOPE_DIM]
        pe = qh[:, QK_NOPE_DIM:]
        ss = jnp.sum(nope * nope + 0.5 * (pe * pe), axis=-1, keepdims=True)
        inv = lax.rsqrt(ss * (1.0 / QK_HEAD_DIM) + EPS) * qscale
        q_ref[hh, :QK_NOPE_DIM, :] = (nope * inv * g_nope).T.astype(BF16)
        q_ref[hh, QK_NOPE_DIM:, :] = _rope_dup(pe * inv * g_pe, cos_t, sin_t).T.astype(BF16)


def _kvproj_kernel(x_ref, ga_ref, wkva_ref, gkva_ref, wkvb_ref, gk_ref, cos_ref, sin_ref, k_ref, v_ref,
                   *, kv_rank):
    h = _rms(x_ref[...], ga_ref[...]).astype(BF16)
    ckv = _dot(h, wkva_ref[...])
    c = _rms(ckv[:, :kv_rank], gkva_ref[...]).astype(BF16)
    pe = ckv[:, kv_rank:]
    pe_ss = 0.5 * jnp.sum(pe * pe, axis=-1, keepdims=True)
    g_nope = gk_ref[:, :QK_NOPE_DIM]
    r = _rope_dup(pe * gk_ref[:, QK_NOPE_DIM:], cos_ref[...], sin_ref[...])
    for hh in range(N_HEADS):
        kvh = _dot(c, wkvb_ref[:, hh * 256:(hh + 1) * 256])
        k_nope = kvh[:, :QK_NOPE_DIM]
        ss = jnp.sum(k_nope * k_nope, axis=-1, keepdims=True) + pe_ss
        inv = lax.rsqrt(ss * (1.0 / QK_HEAD_DIM) + EPS)
        k_ref[hh, :, :QK_NOPE_DIM] = (k_nope * inv * g_nope).astype(BF16)
        k_ref[hh, :, QK_NOPE_DIM:] = (r * inv).astype(BF16)
        v_ref[hh] = kvh[:, QK_NOPE_DIM:].T.astype(BF16)


def _const_spec(shape):
    return pl.BlockSpec(shape, lambda i: (0,) * len(shape))


def _mla_project(x, ga, wqa, gqa, wqb, gq, wkva, gkva, wkvb, gk, cos_t, sin_t, tm):
    T, D = x.shape
    q_rank = wqa.shape[1]
    kv_rank = gkva.shape[1]
    qscale = QK_HEAD_DIM ** -0.5 * math.log2(math.e)
    row = lambda i: (i, 0)
    head_rows = lambda i: (0, i, 0)
    head_cols = lambda i: (0, 0, i)
    tab = pl.BlockSpec((tm, LANES), row)
    qt = pl.pallas_call(
        functools.partial(_qproj_kernel, qscale=qscale),
        grid=(T // tm,),
        in_specs=[pl.BlockSpec((tm, D), row), _const_spec((1, D)), _const_spec(wqa.shape),
                  _const_spec((1, q_rank)), _const_spec(wqb.shape), _const_spec((1, HEAD_PAD)), tab, tab],
        out_specs=pl.BlockSpec((N_HEADS, HEAD_PAD, tm), head_cols),
        out_shape=jax.ShapeDtypeStruct((N_HEADS, HEAD_PAD, T), BF16),
        compiler_params=_params(("parallel",), 2 * (tm * D * 4 + wqa.size * 2 + wqb.size * 2
                                                    + N_HEADS * tm * HEAD_PAD * 2) + 8 * tm * D * 4),
    )(x, ga, wqa, gqa, wqb, gq, cos_t, sin_t)
    k, vt = pl.pallas_call(
        functools.partial(_kvproj_kernel, kv_rank=kv_rank),
        grid=(T // tm,),
        in_specs=[pl.BlockSpec((tm, D), row), _const_spec((1, D)), _const_spec(wkva.shape),
                  _const_spec((1, kv_rank)), _const_spec(wkvb.shape), _const_spec((1, HEAD_PAD)), tab, tab],
        out_specs=[pl.BlockSpec((N_HEADS, tm, HEAD_PAD), head_rows),
                   pl.BlockSpec((N_HEADS, V_HEAD_DIM, tm), head_cols)],
        out_shape=[jax.ShapeDtypeStruct((N_HEADS, T, HEAD_PAD), BF16),
                   jax.ShapeDtypeStruct((N_HEADS, V_HEAD_DIM, T), BF16)],
        compiler_params=_params(("parallel",), 2 * (tm * D * 4 + wkva.size * 2 + wkvb.size * 2
                                                    + N_HEADS * tm * (HEAD_PAD + V_HEAD_DIM) * 2)
                                + 8 * tm * D * 4),
    )(x, ga, wkva, gkva, wkvb, gk, cos_t, sin_t)
    return qt, k, vt


def _flash_kernel(qt_ref, k_ref, vt_ref, o_ref, s_ref, *, tk, chunks_per_step):
    qt = qt_ref[0]
    tq = qt.shape[1]
    n_chunks = k_ref.shape[1] // tk

    def scores(j, slot):
        s_ref[slot] = _dot(k_ref[0, pl.ds(pl.multiple_of(j * tk, tk), tk), :], qt)

    def absorb(j, slot, m, l, acc):
        vt = vt_ref[0, :, pl.ds(pl.multiple_of(j * tk, tk), tk)]
        m_new = jnp.maximum(m, jnp.max(s_ref[slot], axis=0, keepdims=True))
        p = jnp.exp2(s_ref[slot] - m_new)
        alpha = jnp.exp2(m - m_new)
        l = alpha * l + jnp.sum(p, axis=0, keepdims=True)
        acc = alpha * acc + _dot(vt, p.astype(BF16))
        return m_new, l, acc

    def body(step, carry):
        m, l, acc = carry
        for c in range(chunks_per_step):
            j = step * chunks_per_step + c
            scores(jnp.minimum(j + 1, n_chunks - 1), (c + 1) % 2)
            m, l, acc = absorb(j, c % 2, m, l, acc)
        return m, l, acc

    m0 = jnp.full((1, tq), -jnp.inf, F32)
    l0 = jnp.zeros((1, tq), F32)
    acc0 = jnp.zeros((V_HEAD_DIM, tq), F32)
    scores(0, 0)
    _, l, acc = lax.fori_loop(0, n_chunks // chunks_per_step, body, (m0, l0, acc0))
    o_ref[...] = (acc / l).T.astype(BF16)


def _attention(qt, k, vt, o_shape, seq_len, n_seq, tok_off, tq, tk):
    nq = seq_len // tq
    seq0 = tok_off // seq_len
    q0 = tok_off // tq
    chunks_per_step = math.gcd(seq_len // tk, 8)
    assert chunks_per_step % 2 == 0
    return pl.pallas_call(
        functools.partial(_flash_kernel, tk=tk, chunks_per_step=chunks_per_step),
        grid=(n_seq, N_HEADS, nq),
        in_specs=[pl.BlockSpec((1, HEAD_PAD, tq), lambda b, h, i: (h, 0, q0 + b * nq + i)),
                  pl.BlockSpec((1, seq_len, HEAD_PAD), lambda b, h, i: (h, seq0 + b, 0)),
                  pl.BlockSpec((1, V_HEAD_DIM, seq_len), lambda b, h, i: (h, 0, seq0 + b))],
        out_specs=pl.BlockSpec((tq, V_HEAD_DIM), lambda b, h, i: (b * nq + i, h)),
        out_shape=jax.ShapeDtypeStruct(o_shape, BF16),
        scratch_shapes=[pltpu.VMEM((2, tk, tq), F32)],
        compiler_params=_params(("parallel", "parallel", "arbitrary"),
                                2 * seq_len * (HEAD_PAD + V_HEAD_DIM) * 2 + 8 * tq * tk * 4
                                + 4 * tq * HEAD_PAD * 2),
    )(qt, k, vt)


def _fnet_stage1_kernel(x_ref, g_ref, wch_ref, wcl_ref, w1h_ref, w1l_ref, tw_ref, ar_ref, ai_ref, u_ref,
                        *, n1, gdim):
    h = _rms(x_ref[...], g_ref[...])
    h_hi, h_lo = _split(h)
    wch = wch_ref[...]
    wcl = wcl_ref[...]
    for gi in range(FNET_GROUPS):
        cols = slice(gi * gdim, (gi + 1) * gdim)
        u = _dot3(h_hi[:, cols], h_lo[:, cols], wch, wcl)
        u_ref[:n1, cols] = u[:, :gdim]
        u_ref[n1:, cols] = u[:, gdim:]
    u_hi, u_lo = _split(u_ref[...])
    a = _dot3(w1h_ref[...], w1l_ref[...], u_hi, u_lo)
    a_re = a[:n1]
    a_im = a[n1:]
    tw_c = tw_ref[0, :, 0:1]
    tw_s = tw_ref[0, :, 1:2]
    ar_ref[...] = a_re * tw_c + a_im * tw_s
    ai_ref[...] = a_im * tw_c - a_re * tw_s


def _fnet_stage2_kernel(ar_ref, ai_ref, w2h_ref, w2l_ref, z_ref):
    a = jnp.concatenate([ar_ref[...], ai_ref[...]], axis=0)
    a_hi, a_lo = _split(a)
    z_ref[...] = _dot3(w2h_ref[...], w2l_ref[...], a_hi, a_lo).astype(BF16)


def _dft_tables(n1, n2, gdim):
    def cs(rows, cols, period):
        ang = 2.0 * np.pi * ((np.arange(rows)[:, None] * np.arange(cols)[None, :]) % period) / period
        return np.cos(ang), np.sin(ang)
    cc, sc = cs(gdim, gdim, gdim)
    w_ch = np.concatenate([cc, -sc], axis=1) / np.sqrt(gdim)
    c1, s1 = cs(n1, n1, n1)
    w1 = np.block([[c1, s1], [-s1, c1]]) / np.sqrt(n1)
    c2, s2 = cs(n2, n2, n2)
    w2 = np.concatenate([c2, s2], axis=1) / np.sqrt(n2)
    twc, tws = cs(n2, n1, n1 * n2)
    tw = np.zeros((n2, n1, LANES), np.float64)
    tw[:, :, 0] = twc
    tw[:, :, 1] = tws
    return [jnp.asarray(t, F32) for t in (w_ch, w1, w2, tw)]


def _fnet_mix(x, g, seq_len, n_seq, tok_off):
    T, D = x.shape
    n2 = FFT_N2
    n1 = seq_len // n2
    gdim = D // FNET_GROUPS
    w_ch, w1, w2, tw = _dft_tables(n1, n2, gdim)
    wch, wcl = _split(w_ch)
    w1h, w1l = _split(w1)
    w2h, w2l = _split(w2)
    rows = n_seq * seq_len
    r0 = tok_off // seq_len
    x2 = x.reshape(T // n2, n2 * D)
    blk = pl.BlockSpec((n1, D), lambda b, j: (r0 + b, j))
    ar, ai = pl.pallas_call(
        functools.partial(_fnet_stage1_kernel, n1=n1, gdim=gdim),
        grid=(n_seq, n2),
        in_specs=[blk, pl.BlockSpec((1, D), lambda b, j: (0, 0)),
                  pl.BlockSpec(wch.shape, lambda b, j: (0, 0)), pl.BlockSpec(wcl.shape, lambda b, j: (0, 0)),
                  pl.BlockSpec(w1h.shape, lambda b, j: (0, 0)), pl.BlockSpec(w1l.shape, lambda b, j: (0, 0)),
                  pl.BlockSpec((1, n1, LANES), lambda b, j: (j, 0, 0))],
        out_specs=[pl.BlockSpec((n1, D), lambda b, j: (b, j))] * 2,
        out_shape=[jax.ShapeDtypeStruct((rows // n2, n2 * D), F32)] * 2,
        scratch_shapes=[pltpu.VMEM((2 * n1, D), F32)],
        compiler_params=_params(("parallel", "parallel"), 24 * n1 * D * 4 + 16 * 1024 * 1024),
    )(x2, g, wch, wcl, w1h, w1l, tw)
    ar = ar.reshape(rows, D)
    ai = ai.reshape(rows, D)
    tile = pl.BlockSpec((n2, D), lambda b, j: (b * n1 + j, 0))
    z = pl.pallas_call(
        _fnet_stage2_kernel,
        grid=(n_seq, n1),
        in_specs=[tile, tile, pl.BlockSpec(w2h.shape, lambda b, j: (0, 0)),
                  pl.BlockSpec(w2l.shape, lambda b, j: (0, 0))],
        out_specs=pl.BlockSpec((None, n2, D), lambda b, j: (b, 0, j)),
        out_shape=jax.ShapeDtypeStruct((n_seq, n2, n1 * D), BF16),
        compiler_params=_params(("parallel", "parallel"), 24 * n2 * D * 4 + 16 * 1024 * 1024),
    )(ar, ai, w2h, w2l)
    return z.reshape(rows, D)


def _oproj_kernel(x_ref, a_ref, w_ref, o_ref):
    o_ref[...] = x_ref[...] + _dot(a_ref[...], w_ref[...])


def _oproj_residual(x, a, w, tm):
    T, D = x.shape
    row = lambda i: (i, 0)
    return pl.pallas_call(
        _oproj_kernel,
        grid=(T // tm,),
        in_specs=[pl.BlockSpec((tm, D), row), pl.BlockSpec((tm, a.shape[1]), row), _const_spec(w.shape)],
        out_specs=pl.BlockSpec((tm, D), row),
        out_shape=jax.ShapeDtypeStruct((T, D), F32),
        compiler_params=_params(("parallel",), 2 * (2 * tm * D * 4 + tm * a.shape[1] * 2 + w.size * 2)
                                + 2 * tm * D * 4),
    )(x, a, w)


def _mlp_kernel(x_ref, g_ref, wup_ref, wdn_ref, o_ref, xn_ref):
    @pl.when(pl.program_id(1) == 0)
    def _():
        x = x_ref[...]
        xn_ref[...] = _rms(x, g_ref[...]).astype(BF16)
        o_ref[...] = x

    hcol = jnp.square(jnp.maximum(_dot(xn_ref[...], wup_ref[...]), 0.0)).astype(BF16)
    o_ref[...] += _dot(hcol, wdn_ref[...])


def _mlp(x, g, w_up, w_down, tm, tf):
    T, D = x.shape
    dff = w_up.shape[1]
    return pl.pallas_call(
        _mlp_kernel,
        grid=(T // tm, dff // tf),
        in_specs=[pl.BlockSpec((tm, D), lambda i, f: (i, 0)), pl.BlockSpec((1, D), lambda i, f: (0, 0)),
                  pl.BlockSpec((D, tf), lambda i, f: (0, f)), pl.BlockSpec((tf, D), lambda i, f: (f, 0))],
        out_specs=pl.BlockSpec((tm, D), lambda i, f: (i, 0)),
        out_shape=jax.ShapeDtypeStruct((T, D), F32),
        scratch_shapes=[pltpu.VMEM((tm, D), BF16)],
        compiler_params=_params(("parallel", "arbitrary"),
                                2 * (2 * tm * D * 4 + 2 * D * tf * 2) + tm * D * 2 + 3 * tm * tf * 4),
    )(x, g, w_up, w_down)


def _dup_rope_cols(w):
    return jnp.concatenate([w, w], axis=-1)


def _pad_head_vec(g):
    return jnp.concatenate([g[:QK_NOPE_DIM], _dup_rope_cols(g[QK_NOPE_DIM:])])[None, :].astype(F32)


def _rope_tables(seq_lens):
    inv_freq = ROPE_THETA ** (-jnp.arange(0, QK_ROPE_DIM, 2, dtype=F32) / QK_ROPE_DIM)
    pos = jnp.concatenate([jnp.arange(s, dtype=F32) for s in seq_lens])
    ang = pos[:, None] * inv_freq[None, :]
    c, s = jnp.cos(ang), jnp.sin(ang)
    z = jnp.zeros_like(c)
    return jnp.concatenate([c, c, z, z], axis=1), jnp.concatenate([-s, s, z, z], axis=1)


def _pick_tile(n, target):
    t = min(n, target)
    while n % t:
        t //= 2
    return t


def kernel(x_prompt, x_sample, attn_norm, w_q_a, q_a_norm, w_q_b, w_kv_a, kv_a_norm, w_kv_b, q_norm, k_norm,
           w_attn_o, fnet_norm, w_fnet_o, mlp_norm, w_up, w_down):
    bp, sp, D = x_prompt.shape
    bs, ss, _ = x_sample.shape
    tp, ts = bp * sp, bs * ss
    T = tp + ts
    depth = mlp_norm.shape[0]
    kv_rank = kv_a_norm.shape[1]
    q_rank = q_a_norm.shape[1]
    x = jnp.concatenate([x_prompt.reshape(tp, D), x_sample.reshape(ts, D)], axis=0)
    cos_t, sin_t = _rope_tables([sp] * bp + [ss] * bs)

    tm = _pick_tile(math.gcd(tp, ts), 256)
    tm_mlp = _pick_tile(math.gcd(tp, ts), 512)
    tf = _pick_tile(w_up.shape[2], 1024)
    row2 = lambda v: v[None, :].astype(F32)

    for i in range(depth):
        j = i // 2
        if i % 2 == 0:
            wqb = w_q_b[j].reshape(q_rank, N_HEADS, QK_HEAD_DIM)
            wqb = jnp.concatenate([wqb[..., :QK_NOPE_DIM], _dup_rope_cols(wqb[..., QK_NOPE_DIM:])], axis=-1)
            wqb = wqb.reshape(q_rank, N_HEADS * HEAD_PAD).astype(BF16)
            wkva = jnp.concatenate([w_kv_a[j][:, :kv_rank], _dup_rope_cols(w_kv_a[j][:, kv_rank:])], axis=1)
            q, k, v = _mla_project(
                x, row2(attn_norm[j]), w_q_a[j].astype(BF16), row2(q_a_norm[j]), wqb, _pad_head_vec(q_norm[j]),
                wkva.astype(BF16), row2(kv_a_norm[j]), w_kv_b[j].astype(BF16), _pad_head_vec(k_norm[j]),
                cos_t, sin_t, tm)
            o_p = _attention(q, k, v, (tp, N_HEADS * V_HEAD_DIM), sp, bp, 0,
                             _pick_tile(sp, 512), _pick_tile(sp, 512))
            o_s = _attention(q, k, v, (ts, N_HEADS * V_HEAD_DIM), ss, bs, tp,
                             _pick_tile(ss, 512), _pick_tile(ss, 512))
            a = jnp.concatenate([o_p, o_s], axis=0)
            w_o = w_attn_o[j]
        else:
            g = row2(fnet_norm[j])
            a = jnp.concatenate([_fnet_mix(x, g, sp, bp, 0), _fnet_mix(x, g, ss, bs, tp)], axis=0)
            w_o = w_fnet_o[j]
        x = _oproj_residual(x, a, w_o.astype(BF16), tm_mlp)
        x = _mlp(x, row2(mlp_norm[i]), w_up[i].astype(BF16), w_down[i].astype(BF16), tm_mlp, tf)

    return x[:tp].reshape(bp, sp, D), x[tp:].reshape(bs, ss, D)
```

```python
import functools
import math

import numpy as np
import jax
import jax.numpy as jnp
from jax import lax
from jax.experimental import pallas as pl
from jax.experimental.pallas import tpu as pltpu

F32 = jnp.float32
BF16 = jnp.bfloat16

N_HEADS = 16
QK_NOPE_DIM = 128
QK_ROPE_DIM = 64
QK_HEAD_DIM = QK_NOPE_DIM + QK_ROPE_DIM
V_HEAD_DIM = 128
HEAD_PAD = 256
ROPE_THETA = 10000.0
FNET_GROUPS = 8
EPS = 1e-6
FFT_N2 = 128
LANES = 128
VMEM_CAP = 60 * 1024 * 1024


def _vmem_limit(nbytes):
    return int(min(max(nbytes, 16 * 1024 * 1024), VMEM_CAP))


def _params(sem, nbytes):
    return pltpu.CompilerParams(dimension_semantics=sem, vmem_limit_bytes=_vmem_limit(nbytes))


def _rms(xf, g):
    return xf * lax.rsqrt(jnp.mean(xf * xf, axis=-1, keepdims=True) + EPS) * g


def _split(a):
    hi = a.astype(BF16)
    lo = (a - hi.astype(F32)).astype(BF16)
    return hi, lo


def _dot(a, b):
    return jnp.dot(a, b, preferred_element_type=F32)


def _dot3(a_hi, a_lo, b_hi, b_lo):
    return _dot(a_hi, b_hi) + (_dot(a_lo, b_hi) + _dot(a_hi, b_lo))


def _rope_dup(pe, cos_t, sin_t):
    return pe * cos_t + pltpu.roll(pe, 32, axis=1) * sin_t


def _qproj_kernel(x_ref, ga_ref, wqa_ref, gqa_ref, wqb_ref, gq_ref, cos_ref, sin_ref, q_ref, *, qscale):
    h = _rms(x_ref[...], ga_ref[...]).astype(BF16)
    cq = _rms(_dot(h, wqa_ref[...]), gqa_ref[...]).astype(BF16)
    cos_t = cos_ref[...]
    sin_t = sin_ref[...]
    g_nope = gq_ref[:, :QK_NOPE_DIM]
    g_pe = gq_ref[:, QK_NOPE_DIM:]
    for hh in range(N_HEADS):
        qh = _dot(cq, wqb_ref[:, hh * HEAD_PAD:(hh + 1) * HEAD_PAD])
        nope = qh[:, :QK_NOPE_DIM]
        pe = qh[:, QK_NOPE_DIM:]
        ss = jnp.sum(nope * nope + 0.5 * (pe * pe), axis=-1, keepdims=True)
        inv = lax.rsqrt(ss * (1.0 / QK_HEAD_DIM) + EPS) * qscale
        q_ref[hh, :QK_NOPE_DIM, :] = (nope * inv * g_nope).T.astype(BF16)
        q_ref[hh, QK_NOPE_DIM:, :] = _rope_dup(pe * inv * g_pe, cos_t, sin_t).T.astype(BF16)


def _kvproj_kernel(x_ref, ga_ref, wkva_ref, gkva_ref, wkvb_ref, gk_ref, cos_ref, sin_ref, k_ref, v_ref,
                   *, kv_rank):
    h = _rms(x_ref[...], ga_ref[...]).astype(BF16)
    ckv = _dot(h, wkva_ref[...])
    c = _rms(ckv[:, :kv_rank], gkva_ref[...]).astype(BF16)
    pe = ckv[:, kv_rank:]
    pe_ss = 0.5 * jnp.sum(pe * pe, axis=-1, keepdims=True)
    g_nope = gk_ref[:, :QK_NOPE_DIM]
    r = _rope_dup(pe * gk_ref[:, QK_NOPE_DIM:], cos_ref[...], sin_ref[...])
    for hh in range(N_HEADS):
        kvh = _dot(c, wkvb_ref[:, hh * 256:(hh + 1) * 256])
        k_nope = kvh[:, :QK_NOPE_DIM]
        ss = jnp.sum(k_nope * k_nope, axis=-1, keepdims=True) + pe_ss
        inv = lax.rsqrt(ss * (1.0 / QK_HEAD_DIM) + EPS)
        k_ref[hh, :, :QK_NOPE_DIM] = (k_nope * inv * g_nope).astype(BF16)
        k_ref[hh, :, QK_NOPE_DIM:] = (r * inv).astype(BF16)
        v_ref[hh] = kvh[:, QK_NOPE_DIM:].T.astype(BF16)


def _const_spec(shape):
    return pl.BlockSpec(shape, lambda i: (0,) * len(shape))


def _mla_project(x, ga, wqa, gqa, wqb, gq, wkva, gkva, wkvb, gk, cos_t, sin_t, tm):
    T, D = x.shape
    q_rank = wqa.shape[1]
    kv_rank = gkva.shape[1]
    qscale = QK_HEAD_DIM ** -0.5 * math.log2(math.e)
    row = lambda i: (i, 0)
    head_rows = lambda i: (0, i, 0)
    head_cols = lambda i: (0, 0, i)
    tab = pl.BlockSpec((tm, LANES), row)
    qt = pl.pallas_call(
        functools.partial(_qproj_kernel, qscale=qscale),
        grid=(T // tm,),
        in_specs=[pl.BlockSpec((tm, D), row), _const_spec((1, D)), _const_spec(wqa.shape),
                  _const_spec((1, q_rank)), _const_spec(wqb.shape), _const_spec((1, HEAD_PAD)), tab, tab],
        out_specs=pl.BlockSpec((N_HEADS, HEAD_PAD, tm), head_cols),
        out_shape=jax.ShapeDtypeStruct((N_HEADS, HEAD_PAD, T), BF16),
        compiler_params=_params(("parallel",), 2 * (tm * D * 4 + wqa.size * 2 + wqb.size * 2
                                                    + N_HEADS * tm * HEAD_PAD * 2) + 8 * tm * D * 4),
    )(x, ga, wqa, gqa, wqb, gq, cos_t, sin_t)
    k, vt = pl.pallas_call(
        functools.partial(_kvproj_kernel, kv_rank=kv_rank),
        grid=(T // tm,),
        in_specs=[pl.BlockSpec((tm, D), row), _const_spec((1, D)), _const_spec(wkva.shape),
                  _const_spec((1, kv_rank)), _const_spec(wkvb.shape), _const_spec((1, HEAD_PAD)), tab, tab],
        out_specs=[pl.BlockSpec((N_HEADS, tm, HEAD_PAD), head_rows),
                   pl.BlockSpec((N_HEADS, V_HEAD_DIM, tm), head_cols)],
        out_shape=[jax.ShapeDtypeStruct((N_HEADS, T, HEAD_PAD), BF16),
                   jax.ShapeDtypeStruct((N_HEADS, V_HEAD_DIM, T), BF16)],
        compiler_params=_params(("parallel",), 2 * (tm * D * 4 + wkva.size * 2 + wkvb.size * 2
                                                    + N_HEADS * tm * (HEAD_PAD + V_HEAD_DIM) * 2)
                                + 8 * tm * D * 4),
    )(x, ga, wkva, gkva, wkvb, gk, cos_t, sin_t)
    return qt, k, vt


def _flash_kernel(qt_ref, k_ref, vt_ref, o_ref, s_ref, *, tk, chunks_per_step):
    qt = qt_ref[0]
    tq = qt.shape[1]
    n_chunks = k_ref.shape[1] // tk

    def scores(j, slot):
        s_ref[slot] = _dot(k_ref[0, pl.ds(pl.multiple_of(j * tk, tk), tk), :], qt)

    def absorb(j, slot, m, l, acc):
        vt = vt_ref[0, :, pl.ds(pl.multiple_of(j * tk, tk), tk)]
        m_new = jnp.maximum(m, jnp.max(s_ref[slot], axis=0, keepdims=True))
        p = jnp.exp2(s_ref[slot] - m_new)
        alpha = jnp.exp2(m - m_new)
        l = alpha * l + jnp.sum(p, axis=0, keepdims=True)
        acc = alpha * acc + _dot(vt, p.astype(BF16))
        return m_new, l, acc

    def body(step, carry):
        m, l, acc = carry
        for c in range(chunks_per_step):
            j = step * chunks_per_step + c
            scores(jnp.minimum(j + 1, n_chunks - 1), (c + 1) % 2)
            m, l, acc = absorb(j, c % 2, m, l, acc)
        return m, l, acc

    m0 = jnp.full((1, tq), -jnp.inf, F32)
    l0 = jnp.zeros((1, tq), F32)
    acc0 = jnp.zeros((V_HEAD_DIM, tq), F32)
    scores(0, 0)
    _, l, acc = lax.fori_loop(0, n_chunks // chunks_per_step, body, (m0, l0, acc0))
    o_ref[...] = (acc / l).T.astype(BF16)


def _attention(qt, k, vt, seq_len, n_seq, tok_off, tq, tk):
    nq = seq_len // tq
    seq0 = tok_off // seq_len
    q0 = tok_off // tq
    chunks_per_step = math.gcd(seq_len // tk, 16)
    assert chunks_per_step % 2 == 0
    return pl.pallas_call(
        functools.partial(_flash_kernel, tk=tk, chunks_per_step=chunks_per_step),
        grid=(n_seq, N_HEADS, nq),
        in_specs=[pl.BlockSpec((1, HEAD_PAD, tq), lambda b, h, i: (h, 0, q0 + b * nq + i)),
                  pl.BlockSpec((1, seq_len, HEAD_PAD), lambda b, h, i: (h, seq0 + b, 0)),
                  pl.BlockSpec((1, V_HEAD_DIM, seq_len), lambda b, h, i: (h, 0, seq0 + b))],
        out_specs=pl.BlockSpec((tq, V_HEAD_DIM), lambda b, h, i: (b * nq + i, h)),
        out_shape=jax.ShapeDtypeStruct((n_seq * seq_len, N_HEADS * V_HEAD_DIM), BF16),
        scratch_shapes=[pltpu.VMEM((2, tk, tq), F32)],
        compiler_params=_params(("parallel", "parallel", "arbitrary"),
                                2 * seq_len * (HEAD_PAD + V_HEAD_DIM) * 2 + 8 * tq * tk * 4
                                + 4 * tq * HEAD_PAD * 2),
    )(qt, k, vt)


def _fnet_stage1_kernel(x_hbm, g_ref, wch_ref, wcl_ref, w1h_ref, w1l_ref, tw_ref, ar_ref, ai_ref,
                        xbuf, sem, u_ref, *, n1, gdim, row0):
    i = pl.program_id(0)
    slot = i % 2

    def gather(step, to_slot):
        rows = pl.ds(row0 + (step // FFT_N2) * n1, n1)
        return pltpu.make_async_copy(x_hbm.at[rows, step % FFT_N2, :], xbuf.at[to_slot], sem.at[to_slot])

    @pl.when(i == 0)
    def _():
        gather(0, 0).start()

    @pl.when(i + 1 < pl.num_programs(0))
    def _():
        gather(i + 1, 1 - slot).start()

    gather(i, slot).wait()
    h = _rms(xbuf[slot], g_ref[...])
    h_hi, h_lo = _split(h)
    wch = wch_ref[...]
    wcl = wcl_ref[...]
    for gi in range(FNET_GROUPS):
        cols = slice(gi * gdim, (gi + 1) * gdim)
        u = _dot3(h_hi[:, cols], h_lo[:, cols], wch, wcl)
        u_ref[:n1, cols] = u[:, :gdim]
        u_ref[n1:, cols] = u[:, gdim:]
    u_hi, u_lo = _split(u_ref[...])
    a = _dot3(w1h_ref[...], w1l_ref[...], u_hi, u_lo)
    a_re = a[:n1]
    a_im = a[n1:]
    tw_c = tw_ref[0, :, 0:1]
    tw_s = tw_ref[0, :, 1:2]
    ar_ref[...] = a_re * tw_c + a_im * tw_s
    ai_ref[...] = a_im * tw_c - a_re * tw_s


def _fnet_stage2_kernel(ar_hbm, ai_hbm, w2h_ref, w2l_ref, z_hbm, abuf, zbuf, in_sem, out_sem, *, n1):
    n2 = FFT_N2
    i = pl.program_id(0)
    n_steps = pl.num_programs(0)
    slot = i % 2

    def gathers(step, to_slot):
        rows = pl.ds((step // n1) * n2, n2)
        k1 = step % n1
        return (pltpu.make_async_copy(ar_hbm.at[rows, k1, :], abuf.at[to_slot, pl.ds(0, n2)],
                                      in_sem.at[to_slot, 0]),
                pltpu.make_async_copy(ai_hbm.at[rows, k1, :], abuf.at[to_slot, pl.ds(n2, n2)],
                                      in_sem.at[to_slot, 1]))

    def scatter(step, from_slot):
        rows = pl.ds((step // n1) * n2, n2)
        return pltpu.make_async_copy(zbuf.at[from_slot], z_hbm.at[rows, step % n1, :], out_sem.at[from_slot])

    @pl.when(i == 0)
    def _():
        for cp in gathers(0, 0):
            cp.start()

    @pl.when(i + 1 < n_steps)
    def _():
        for cp in gathers(i + 1, 1 - slot):
            cp.start()

    for cp in gathers(i, slot):
        cp.wait()
    a_hi, a_lo = _split(abuf[slot])
    z = _dot3(w2h_ref[...], w2l_ref[...], a_hi, a_lo)

    @pl.when(i >= 2)
    def _():
        scatter(i - 2, slot).wait()

    zbuf[slot] = z
    scatter(i, slot).start()

    @pl.when(i == n_steps - 1)
    def _():
        scatter(i - 1, 1 - slot).wait()
        scatter(i, slot).wait()


def _dft_tables(n1, n2, gdim):
    def cs(rows, cols, period):
        ang = 2.0 * np.pi * ((np.arange(rows)[:, None] * np.arange(cols)[None, :]) % period) / period
        return np.cos(ang), np.sin(ang)
    cc, sc = cs(gdim, gdim, gdim)
    w_ch = np.concatenate([cc, -sc], axis=1) / np.sqrt(gdim)
    c1, s1 = cs(n1, n1, n1)
    w1 = np.block([[c1, s1], [-s1, c1]]) / np.sqrt(n1)
    c2, s2 = cs(n2, n2, n2)
    w2 = np.concatenate([c2, s2], axis=1) / np.sqrt(n2)
    twc, tws = cs(n2, n1, n1 * n2)
    tw = np.zeros((n2, n1, LANES), np.float64)
    tw[:, :, 0] = twc
    tw[:, :, 1] = tws
    return [jnp.asarray(t, F32) for t in (w_ch, w1, w2, tw)]


def _fnet_mix(x, g, seq_len, n_seq, tok_off):
    T, D = x.shape
    n2 = FFT_N2
    n1 = seq_len // n2
    gdim = D // FNET_GROUPS
    w_ch, w1, w2, tw = _dft_tables(n1, n2, gdim)
    wch, wcl = _split(w_ch)
    w1h, w1l = _split(w1)
    w2h, w2l = _split(w2)
    const2 = lambda shape: pl.BlockSpec(shape, lambda i: (0, 0))
    any_spec = pl.BlockSpec(memory_space=pl.ANY)
    slab = pl.BlockSpec((None, n1, D), lambda i: (i, 0, 0))
    ar, ai = pl.pallas_call(
        functools.partial(_fnet_stage1_kernel, n1=n1, gdim=gdim, row0=tok_off // n2),
        grid=(n_seq * n2,),
        in_specs=[any_spec, const2((1, D)), const2(wch.shape), const2(wcl.shape), const2(w1h.shape),
                  const2(w1l.shape), pl.BlockSpec((1, n1, LANES), lambda i: (i % n2, 0, 0))],
        out_specs=[slab, slab],
        out_shape=[jax.ShapeDtypeStruct((n_seq * n2, n1, D), F32)] * 2,
        scratch_shapes=[pltpu.VMEM((2, n1, D), F32), pltpu.SemaphoreType.DMA((2,)),
                        pltpu.VMEM((2 * n1, D), F32)],
        compiler_params=_params(("arbitrary",), 24 * n1 * D * 4 + 16 * 1024 * 1024),
    )(x.reshape(T // n2, n2, D), g, wch, wcl, w1h, w1l, tw)
    z = pl.pallas_call(
        functools.partial(_fnet_stage2_kernel, n1=n1),
        grid=(n_seq * n1,),
        in_specs=[any_spec, any_spec, const2(w2h.shape), const2(w2l.shape)],
        out_specs=any_spec,
        out_shape=jax.ShapeDtypeStruct((n_seq * n2, n1, D), F32),
        scratch_shapes=[pltpu.VMEM((2, 2 * n2, D), F32), pltpu.VMEM((2, n2, D), F32),
                        pltpu.SemaphoreType.DMA((2, 2)), pltpu.SemaphoreType.DMA((2,))],
        compiler_params=_params(("arbitrary",), 24 * n2 * D * 4 + 16 * 1024 * 1024),
    )(ar, ai, w2h, w2l)
    return z.reshape(n_seq * seq_len, D)


def _oproj_kernel(x_ref, ap_ref, as_ref, w_ref, o_ref, *, n_prompt_tiles):
    def project(a_ref):
        o_ref[...] = x_ref[...] + _dot(a_ref[...].astype(BF16), w_ref[...])

    pl.when(pl.program_id(0) < n_prompt_tiles)(lambda: project(ap_ref))
    pl.when(pl.program_id(0) >= n_prompt_tiles)(lambda: project(as_ref))


def _oproj_residual(x, a_p, a_s, w, tm):
    T, D = x.shape
    np_tiles = a_p.shape[0] // tm
    width = a_p.shape[1]
    row = lambda i: (i, 0)
    return pl.pallas_call(
        functools.partial(_oproj_kernel, n_prompt_tiles=np_tiles),
        grid=(T // tm,),
        in_specs=[pl.BlockSpec((tm, D), row),
                  pl.BlockSpec((tm, width), lambda i: (jnp.minimum(i, np_tiles - 1), 0)),
                  pl.BlockSpec((tm, width), lambda i: (jnp.maximum(i - np_tiles, 0), 0)),
                  _const_spec(w.shape)],
        out_specs=pl.BlockSpec((tm, D), row),
        out_shape=jax.ShapeDtypeStruct((T, D), F32),
        compiler_params=_params(("arbitrary",), 2 * (2 * tm * D * 4 + 2 * tm * width * a_p.dtype.itemsize
                                                     + w.size * 2) + 2 * tm * D * 4),
    )(x, a_p, a_s, w)


def _mlp_kernel(x_ref, g_ref, wup_ref, wdn_ref, o_ref, xn_ref):
    @pl.when(pl.program_id(1) == 0)
    def _():
        x = x_ref[...]
        xn_ref[...] = _rms(x, g_ref[...]).astype(BF16)
        o_ref[...] = x

    hcol = jnp.square(jnp.maximum(_dot(xn_ref[...], wup_ref[...]), 0.0)).astype(BF16)
    o_ref[...] += _dot(hcol, wdn_ref[...])


def _mlp(x, g, w_up, w_down, tm, tf):
    T, D = x.shape
    dff = w_up.shape[1]
    return pl.pallas_call(
        _mlp_kernel,
        grid=(T // tm, dff // tf),
        in_specs=[pl.BlockSpec((tm, D), lambda i, f: (i, 0)), pl.BlockSpec((1, D), lambda i, f: (0, 0)),
                  pl.BlockSpec((D, tf), lambda i, f: (0, f)), pl.BlockSpec((tf, D), lambda i, f: (f, 0))],
        out_specs=pl.BlockSpec((tm, D), lambda i, f: (i, 0)),
        out_shape=jax.ShapeDtypeStruct((T, D), F32),
        scratch_shapes=[pltpu.VMEM((tm, D), BF16)],
        compiler_params=_params(("parallel", "arbitrary"),
                                2 * (2 * tm * D * 4 + 2 * D * tf * 2) + tm * D * 2 + 3 * tm * tf * 4),
    )(x, g, w_up, w_down)


def _dup_rope_cols(w):
    return jnp.concatenate([w, w], axis=-1)


def _pad_head_vec(g):
    return jnp.concatenate([g[:QK_NOPE_DIM], _dup_rope_cols(g[QK_NOPE_DIM:])])[None, :].astype(F32)


def _rope_tables(seq_lens):
    inv_freq = ROPE_THETA ** (-jnp.arange(0, QK_ROPE_DIM, 2, dtype=F32) / QK_ROPE_DIM)
    pos = jnp.concatenate([jnp.arange(s, dtype=F32) for s in seq_lens])
    ang = pos[:, None] * inv_freq[None, :]
    c, s = jnp.cos(ang), jnp.sin(ang)
    z = jnp.zeros_like(c)
    return jnp.concatenate([c, c, z, z], axis=1), jnp.concatenate([-s, s, z, z], axis=1)


def _pick_tile(n, target):
    t = min(n, target)
    while n % t:
        t //= 2
    return t


def kernel(x_prompt, x_sample, attn_norm, w_q_a, q_a_norm, w_q_b, w_kv_a, kv_a_norm, w_kv_b, q_norm, k_norm,
           w_attn_o, fnet_norm, w_fnet_o, mlp_norm, w_up, w_down):
    bp, sp, D = x_prompt.shape
    bs, ss, _ = x_sample.shape
    tp, ts = bp * sp, bs * ss
    T = tp + ts
    depth = mlp_norm.shape[0]
    kv_rank = kv_a_norm.shape[1]
    q_rank = q_a_norm.shape[1]
    x = jnp.concatenate([x_prompt.reshape(tp, D), x_sample.reshape(ts, D)], axis=0)
    cos_t, sin_t = _rope_tables([sp] * bp + [ss] * bs)

    tm = _pick_tile(math.gcd(tp, ts), 256)
    tm_mlp = _pick_tile(math.gcd(tp, ts), 512)
    tf = _pick_tile(w_up.shape[2], 1024)
    row2 = lambda v: v[None, :].astype(F32)

    for i in range(depth):
        j = i // 2
        if i % 2 == 0:
            wqb = w_q_b[j].reshape(q_rank, N_HEADS, QK_HEAD_DIM)
            wqb = jnp.concatenate([wqb[..., :QK_NOPE_DIM], _dup_rope_cols(wqb[..., QK_NOPE_DIM:])], axis=-1)
            wqb = wqb.reshape(q_rank, N_HEADS * HEAD_PAD).astype(BF16)
            wkva = jnp.concatenate([w_kv_a[j][:, :kv_rank], _dup_rope_cols(w_kv_a[j][:, kv_rank:])], axis=1)
            qt, k, vt = _mla_project(
                x, row2(attn_norm[j]), w_q_a[j].astype(BF16), row2(q_a_norm[j]), wqb, _pad_head_vec(q_norm[j]),
                wkva.astype(BF16), row2(kv_a_norm[j]), w_kv_b[j].astype(BF16), _pad_head_vec(k_norm[j]),
                cos_t, sin_t, tm)
            a_p = _attention(qt, k, vt, sp, bp, 0, _pick_tile(sp, 512), _pick_tile(sp, 512))
            a_s = _attention(qt, k, vt, ss, bs, tp, _pick_tile(ss, 512), _pick_tile(ss, 512))
            w_o = w_attn_o[j]
        else:
            g = row2(fnet_norm[j])
            a_p = _fnet_mix(x, g, sp, bp, 0)
            a_s = _fnet_mix(x, g, ss, bs, tp)
            w_o = w_fnet_o[j]
        x = _oproj_residual(x, a_p, a_s, w_o.astype(BF16), tm_mlp)
        x = _mlp(x, row2(mlp_norm[i]), w_up[i].astype(BF16), w_down[i].astype(BF16), tm_mlp, tf)

    return x[:tp].reshape(bp, sp, D), x[tp:].reshape(bs, ss, D)
```

```python
import functools
import math

import numpy as np
import jax
import jax.numpy as jnp
from jax import lax
from jax.experimental import pallas as pl
from jax.experimental.pallas import tpu as pltpu

F32 = jnp.float32
BF16 = jnp.bfloat16

N_HEADS = 16
QK_NOPE_DIM = 128
QK_ROPE_DIM = 64
QK_HEAD_DIM = QK_NOPE_DIM + QK_ROPE_DIM
V_HEAD_DIM = 128
HEAD_PAD = 256
ROPE_THETA = 10000.0
FNET_GROUPS = 8
EPS = 1e-6
FFT_N2 = 128
LANES = 128
VMEM_CAP = 60 * 1024 * 1024


def _vmem_limit(nbytes):
    return int(min(max(nbytes, 16 * 1024 * 1024), VMEM_CAP))


def _params(sem, nbytes):
    return pltpu.CompilerParams(dimension_semantics=sem, vmem_limit_bytes=_vmem_limit(nbytes))


def _rms(xf, g):
    return xf * lax.rsqrt(jnp.mean(xf * xf, axis=-1, keepdims=True) + EPS) * g


def _dot(a, b):
    return jnp.dot(a, b, preferred_element_type=F32)


def _rope_dup(pe, cos_t, sin_t):
    return pe * cos_t + pltpu.roll(pe, 32, axis=1) * sin_t


def _qproj_kernel(x_ref, ga_ref, wqa_ref, gqa_ref, wqb_ref, gq_ref, cos_ref, sin_ref, q_ref, *, qscale):
    h = _rms(x_ref[...], ga_ref[...]).astype(BF16)
    cq = _rms(_dot(h, wqa_ref[...]), gqa_ref[...]).astype(BF16)
    cos_t = cos_ref[...]
    sin_t = sin_ref[...]
    g_nope = gq_ref[:, :QK_NOPE_DIM]
    g_pe = gq_ref[:, QK_NOPE_DIM:]
    for hh in range(N_HEADS):
        qh = _dot(cq, wqb_ref[:, hh * HEAD_PAD:(hh + 1) * HEAD_PAD])
        nope = qh[:, :QK_NOPE_DIM]
        pe = qh[:, QK_NOPE_DIM:]
        ss = jnp.sum(nope * nope + 0.5 * (pe * pe), axis=-1, keepdims=True)
        inv = lax.rsqrt(ss * (1.0 / QK_HEAD_DIM) + EPS) * qscale
        q_ref[hh, :QK_NOPE_DIM, :] = (nope * inv * g_nope).T.astype(BF16)
        q_ref[hh, QK_NOPE_DIM:, :] = _rope_dup(pe * inv * g_pe, cos_t, sin_t).T.astype(BF16)


def _kvproj_kernel(x_ref, ga_ref, wkva_ref, gkva_ref, wkvb_ref, gk_ref, cos_ref, sin_ref, k_ref, v_ref,
                   *, kv_rank):
    h = _rms(x_ref[...], ga_ref[...]).astype(BF16)
    ckv = _dot(h, wkva_ref[...])
    c = _rms(ckv[:, :kv_rank], gkva_ref[...]).astype(BF16)
    pe = ckv[:, kv_rank:]
    pe_ss = 0.5 * jnp.sum(pe * pe, axis=-1, keepdims=True)
    g_nope = gk_ref[:, :QK_NOPE_DIM]
    r = _rope_dup(pe * gk_ref[:, QK_NOPE_DIM:], cos_ref[...], sin_ref[...])
    for hh in range(N_HEADS):
        kvh = _dot(c, wkvb_ref[:, hh * 256:(hh + 1) * 256])
        k_nope = kvh[:, :QK_NOPE_DIM]
        ss = jnp.sum(k_nope * k_nope, axis=-1, keepdims=True) + pe_ss
        inv = lax.rsqrt(ss * (1.0 / QK_HEAD_DIM) + EPS)
        k_ref[hh, :, :QK_NOPE_DIM] = (k_nope * inv * g_nope).astype(BF16)
        k_ref[hh, :, QK_NOPE_DIM:] = (r * inv).astype(BF16)
        v_ref[hh] = kvh[:, QK_NOPE_DIM:].T.astype(BF16)


def _const_spec(shape):
    return pl.BlockSpec(shape, lambda i: (0,) * len(shape))


def _mla_project(x, ga, wqa, gqa, wqb, gq, wkva, gkva, wkvb, gk, cos_t, sin_t, tm):
    T, D = x.shape
    q_rank = wqa.shape[1]
    kv_rank = gkva.shape[1]
    qscale = QK_HEAD_DIM ** -0.5 * math.log2(math.e)
    row = lambda i: (i, 0)
    head_rows = lambda i: (0, i, 0)
    head_cols = lambda i: (0, 0, i)
    tab = pl.BlockSpec((tm, LANES), row)
    qt = pl.pallas_call(
        functools.partial(_qproj_kernel, qscale=qscale),
        grid=(T // tm,),
        in_specs=[pl.BlockSpec((tm, D), row), _const_spec((1, D)), _const_spec(wqa.shape),
                  _const_spec((1, q_rank)), _const_spec(wqb.shape), _const_spec((1, HEAD_PAD)), tab, tab],
        out_specs=pl.BlockSpec((N_HEADS, HEAD_PAD, tm), head_cols),
        out_shape=jax.ShapeDtypeStruct((N_HEADS, HEAD_PAD, T), BF16),
        compiler_params=_params(("parallel",), 2 * (tm * D * 4 + wqa.size * 2 + wqb.size * 2
                                                    + N_HEADS * tm * HEAD_PAD * 2) + 8 * tm * D * 4),
    )(x, ga, wqa, gqa, wqb, gq, cos_t, sin_t)
    k, vt = pl.pallas_call(
        functools.partial(_kvproj_kernel, kv_rank=kv_rank),
        grid=(T // tm,),
        in_specs=[pl.BlockSpec((tm, D), row), _const_spec((1, D)), _const_spec(wkva.shape),
                  _const_spec((1, kv_rank)), _const_spec(wkvb.shape), _const_spec((1, HEAD_PAD)), tab, tab],
        out_specs=[pl.BlockSpec((N_HEADS, tm, HEAD_PAD), head_rows),
                   pl.BlockSpec((N_HEADS, V_HEAD_DIM, tm), head_cols)],
        out_shape=[jax.ShapeDtypeStruct((N_HEADS, T, HEAD_PAD), BF16),
                   jax.ShapeDtypeStruct((N_HEADS, V_HEAD_DIM, T), BF16)],
        compiler_params=_params(("parallel",), 2 * (tm * D * 4 + wkva.size * 2 + wkvb.size * 2
                                                    + N_HEADS * tm * (HEAD_PAD + V_HEAD_DIM) * 2)
                                + 8 * tm * D * 4),
    )(x, ga, wkva, gkva, wkvb, gk, cos_t, sin_t)
    return qt, k, vt


def _flash_kernel(qt_ref, k_ref, vt_ref, o_ref, s_ref, *, tk, chunks_per_step):
    qt = qt_ref[0]
    tq = qt.shape[1]
    n_chunks = k_ref.shape[1] // tk

    def scores(j, slot):
        s_ref[slot] = _dot(k_ref[0, pl.ds(pl.multiple_of(j * tk, tk), tk), :], qt)

    def absorb(j, slot, m, l, acc):
        vt = vt_ref[0, :, pl.ds(pl.multiple_of(j * tk, tk), tk)]
        m_new = jnp.maximum(m, jnp.max(s_ref[slot], axis=0, keepdims=True))
        p = jnp.exp2(s_ref[slot] - m_new)
        alpha = jnp.exp2(m - m_new)
        l = alpha * l + jnp.sum(p, axis=0, keepdims=True)
        acc = alpha * acc + _dot(vt, p.astype(BF16))
        return m_new, l, acc

    def body(step, carry):
        m, l, acc = carry
        for c in range(chunks_per_step):
            j = step * chunks_per_step + c
            scores(jnp.minimum(j + 1, n_chunks - 1), (c + 1) % 2)
            m, l, acc = absorb(j, c % 2, m, l, acc)
        return m, l, acc

    m0 = jnp.full((1, tq), -jnp.inf, F32)
    l0 = jnp.zeros((1, tq), F32)
    acc0 = jnp.zeros((V_HEAD_DIM, tq), F32)
    scores(0, 0)
    _, l, acc = lax.fori_loop(0, n_chunks // chunks_per_step, body, (m0, l0, acc0))
    o_ref[...] = (acc / l).T.astype(BF16)


def _attention(qt, k, vt, seq_len, tq, tk):
    n_seq = k.shape[1] // seq_len
    nq = seq_len // tq
    n_chunks = seq_len // tk
    chunks_per_step = math.gcd(n_chunks, 16)
    if chunks_per_step == n_chunks and n_chunks % 4 == 0:
        chunks_per_step //= 2
    assert chunks_per_step % 2 == 0
    return pl.pallas_call(
        functools.partial(_flash_kernel, tk=tk, chunks_per_step=chunks_per_step),
        grid=(n_seq, N_HEADS, nq),
        in_specs=[pl.BlockSpec((1, HEAD_PAD, tq), lambda b, h, i: (h, 0, b * nq + i)),
                  pl.BlockSpec((1, seq_len, HEAD_PAD), lambda b, h, i: (h, b, 0)),
                  pl.BlockSpec((1, V_HEAD_DIM, seq_len), lambda b, h, i: (h, 0, b))],
        out_specs=pl.BlockSpec((tq, V_HEAD_DIM), lambda b, h, i: (b * nq + i, h)),
        out_shape=jax.ShapeDtypeStruct((n_seq * seq_len, N_HEADS * V_HEAD_DIM), BF16),
        scratch_shapes=[pltpu.VMEM((2, tk, tq), F32)],
        compiler_params=_params(("parallel", "parallel", "arbitrary"),
                                2 * seq_len * (HEAD_PAD + V_HEAD_DIM) * 2 + 8 * tq * tk * 4
                                + 4 * tq * HEAD_PAD * 2),
    )(qt, k, vt)


def _fnet_stage1_kernel(x_hbm, g_ref, wc_ref, w1_ref, tw_ref, ar_ref, ai_ref, xbuf, sem, u_ref,
                        *, n1, gdim):
    i = pl.program_id(0)
    slot = i % 2

    def gather(step, to_slot):
        rows = pl.ds((step // FFT_N2) * n1, n1)
        return pltpu.make_async_copy(x_hbm.at[rows, step % FFT_N2, :], xbuf.at[to_slot], sem.at[to_slot])

    @pl.when(i == 0)
    def _():
        gather(0, 0).start()

    @pl.when(i + 1 < pl.num_programs(0))
    def _():
        gather(i + 1, 1 - slot).start()

    gather(i, slot).wait()
    h = _rms(xbuf[slot], g_ref[...]).astype(BF16)
    wc = wc_ref[...]
    for gi in range(FNET_GROUPS):
        cols = slice(gi * gdim, (gi + 1) * gdim)
        u = _dot(h[:, cols], wc)
        u_ref[:n1, cols] = u[:, :gdim]
        u_ref[n1:, cols] = u[:, gdim:]
    a = _dot(w1_ref[...], u_ref[...].astype(BF16))
    a_re = a[:n1]
    a_im = a[n1:]
    tw_c = tw_ref[0, :, 0:1]
    tw_s = tw_ref[0, :, 1:2]
    ar_ref[...] = a_re * tw_c + a_im * tw_s
    ai_ref[...] = a_im * tw_c - a_re * tw_s


def _fnet_stage2_kernel(ar_hbm, ai_hbm, w2_ref, z_hbm, abuf, zbuf, in_sem, out_sem, *, n1):
    n2 = FFT_N2
    i = pl.program_id(0)
    n_steps = pl.num_programs(0)
    slot = i % 2

    def gathers(step, to_slot):
        rows = pl.ds((step // n1) * n2, n2)
        k1 = step % n1
        return (pltpu.make_async_copy(ar_hbm.at[rows, k1, :], abuf.at[to_slot, pl.ds(0, n2)],
                                      in_sem.at[to_slot, 0]),
                pltpu.make_async_copy(ai_hbm.at[rows, k1, :], abuf.at[to_slot, pl.ds(n2, n2)],
                                      in_sem.at[to_slot, 1]))

    def scatter(step, from_slot):
        rows = pl.ds((step // n1) * n2, n2)
        return pltpu.make_async_copy(zbuf.at[from_slot], z_hbm.at[rows, step % n1, :], out_sem.at[from_slot])

    @pl.when(i == 0)
    def _():
        for cp in gathers(0, 0):
            cp.start()

    @pl.when(i + 1 < n_steps)
    def _():
        for cp in gathers(i + 1, 1 - slot):
            cp.start()

    for cp in gathers(i, slot):
        cp.wait()
    z = _dot(w2_ref[...], abuf[slot].astype(BF16))

    @pl.when(i >= 2)
    def _():
        scatter(i - 2, slot).wait()

    zbuf[slot] = z
    scatter(i, slot).start()

    @pl.when(i == n_steps - 1)
    def _():
        scatter(i - 1, 1 - slot).wait()
        scatter(i, slot).wait()


def _dft_tables(n1, n2, gdim):
    def cs(rows, cols, period):
        ang = 2.0 * np.pi * ((np.arange(rows)[:, None] * np.arange(cols)[None, :]) % period) / period
        return np.cos(ang), np.sin(ang)
    cc, sc = cs(gdim, gdim, gdim)
    w_ch = np.concatenate([cc, -sc], axis=1) / np.sqrt(gdim)
    c1, s1 = cs(n1, n1, n1)
    w1 = np.block([[c1, s1], [-s1, c1]]) / np.sqrt(n1)
    c2, s2 = cs(n2, n2, n2)
    w2 = np.concatenate([c2, s2], axis=1) / np.sqrt(n2)
    twc, tws = cs(n2, n1, n1 * n2)
    tw = np.zeros((n2, n1, LANES), np.float64)
    tw[:, :, 0] = twc
    tw[:, :, 1] = tws
    return [jnp.asarray(t, F32) for t in (w_ch, w1, w2, tw)]


def _fnet_mix(x, g, seq_len):
    T, D = x.shape
    n_seq = T // seq_len
    n2 = FFT_N2
    n1 = seq_len // n2
    gdim = D // FNET_GROUPS
    w_ch, w1, w2, tw = _dft_tables(n1, n2, gdim)
    w_ch, w1, w2 = (t.astype(BF16) for t in (w_ch, w1, w2))
    const2 = lambda shape: pl.BlockSpec(shape, lambda i: (0, 0))
    any_spec = pl.BlockSpec(memory_space=pl.ANY)
    slab = pl.BlockSpec((None, n1, D), lambda i: (i, 0, 0))
    ar, ai = pl.pallas_call(
        functools.partial(_fnet_stage1_kernel, n1=n1, gdim=gdim),
        grid=(n_seq * n2,),
        in_specs=[any_spec, const2((1, D)), const2(w_ch.shape), const2(w1.shape),
                  pl.BlockSpec((1, n1, LANES), lambda i: (i % n2, 0, 0))],
        out_specs=[slab, slab],
        out_shape=[jax.ShapeDtypeStruct((n_seq * n2, n1, D), F32)] * 2,
        scratch_shapes=[pltpu.VMEM((2, n1, D), F32), pltpu.SemaphoreType.DMA((2,)),
                        pltpu.VMEM((2 * n1, D), F32)],
        compiler_params=_params(("arbitrary",), 24 * n1 * D * 4 + 16 * 1024 * 1024),
    )(x.reshape(T // n2, n2, D), g, w_ch, w1, tw)
    z = pl.pallas_call(
        functools.partial(_fnet_stage2_kernel, n1=n1),
        grid=(n_seq * n1,),
        in_specs=[any_spec, any_spec, const2(w2.shape)],
        out_specs=any_spec,
        out_shape=jax.ShapeDtypeStruct((n_seq * n2, n1, D), F32),
        scratch_shapes=[pltpu.VMEM((2, 2 * n2, D), F32), pltpu.VMEM((2, n2, D), F32),
                        pltpu.SemaphoreType.DMA((2, 2)), pltpu.SemaphoreType.DMA((2,))],
        compiler_params=_params(("arbitrary",), 24 * n2 * D * 4 + 16 * 1024 * 1024),
    )(ar, ai, w2)
    return z.reshape(T, D)


def _oproj_kernel(x_ref, a_ref, w_ref, o_ref):
    o_ref[...] = x_ref[...] + _dot(a_ref[...].astype(BF16), w_ref[...])


def _oproj_residual(x, a, w, tm):
    T, D = x.shape
    row = lambda i: (i, 0)
    return pl.pallas_call(
        _oproj_kernel,
        grid=(T // tm,),
        in_specs=[pl.BlockSpec((tm, D), row), pl.BlockSpec((tm, a.shape[1]), row), _const_spec(w.shape)],
        out_specs=pl.BlockSpec((tm, D), row),
        out_shape=jax.ShapeDtypeStruct((T, D), F32),
        compiler_params=_params(("parallel",), 2 * (2 * tm * D * 4 + tm * a.shape[1] * a.dtype.itemsize
                                                    + w.size * 2) + 2 * tm * D * 4),
    )(x, a, w)


def _mlp_kernel(x_ref, g_ref, wup_ref, wdn_ref, o_ref, xn_ref):
    @pl.when(pl.program_id(1) == 0)
    def _():
        x = x_ref[...]
        xn_ref[...] = _rms(x, g_ref[...]).astype(BF16)
        o_ref[...] = x

    hcol = jnp.square(jnp.maximum(_dot(xn_ref[...], wup_ref[...]), 0.0)).astype(BF16)
    o_ref[...] += _dot(hcol, wdn_ref[...])


def _mlp(x, g, w_up, w_down, tm, tf):
    T, D = x.shape
    dff = w_up.shape[1]
    return pl.pallas_call(
        _mlp_kernel,
        grid=(T // tm, dff // tf),
        in_specs=[pl.BlockSpec((tm, D), lambda i, f: (i, 0)), pl.BlockSpec((1, D), lambda i, f: (0, 0)),
                  pl.BlockSpec((D, tf), lambda i, f: (0, f)), pl.BlockSpec((tf, D), lambda i, f: (f, 0))],
        out_specs=pl.BlockSpec((tm, D), lambda i, f: (i, 0)),
        out_shape=jax.ShapeDtypeStruct((T, D), F32),
        scratch_shapes=[pltpu.VMEM((tm, D), BF16)],
        compiler_params=_params(("parallel", "arbitrary"),
                                2 * (2 * tm * D * 4 + 2 * D * tf * 2) + tm * D * 2 + 3 * tm * tf * 4),
    )(x, g, w_up, w_down)


def _dup_rope_cols(w):
    return jnp.concatenate([w, w], axis=-1)


def _pad_head_vec(g):
    return jnp.concatenate([g[:QK_NOPE_DIM], _dup_rope_cols(g[QK_NOPE_DIM:])])[None, :].astype(F32)


def _rope_tables(n_seq, seq_len):
    inv_freq = ROPE_THETA ** (-jnp.arange(0, QK_ROPE_DIM, 2, dtype=F32) / QK_ROPE_DIM)
    ang = jnp.arange(seq_len, dtype=F32)[:, None] * inv_freq[None, :]
    c, s = jnp.cos(ang), jnp.sin(ang)
    z = jnp.zeros_like(c)
    tile = lambda t: jnp.tile(t, (n_seq, 1))
    return tile(jnp.concatenate([c, c, z, z], axis=1)), tile(jnp.concatenate([-s, s, z, z], axis=1))


def _pick_tile(n, target):
    t = min(n, target)
    while n % t:
        t //= 2
    return t


def _trunk(x, seq_len, p):
    T = x.shape[0]
    cos_t, sin_t = _rope_tables(T // seq_len, seq_len)
    tm = _pick_tile(T, 256)
    tm_mlp = _pick_tile(T, 512)
    tf = _pick_tile(p["w_up"][0].shape[1], 1024)
    ts = _pick_tile(seq_len, 512)
    for i in range(len(p["w_up"])):
        j = i // 2
        if i % 2 == 0:
            qt, k, vt = _mla_project(x, p["attn_norm"][j], p["w_q_a"][j], p["q_a_norm"][j], p["w_q_b"][j],
                                     p["q_norm"][j], p["w_kv_a"][j], p["kv_a_norm"][j], p["w_kv_b"][j],
                                     p["k_norm"][j], cos_t, sin_t, tm)
            a = _attention(qt, k, vt, seq_len, ts, ts)
        else:
            a = _fnet_mix(x, p["fnet_norm"][j], seq_len)
        x = _oproj_residual(x, a, p["w_mix_o"][i], tm_mlp)
        x = _mlp(x, p["mlp_norm"][i], p["w_up"][i], p["w_down"][i], tm_mlp, tf)
    return x


def kernel(x_prompt, x_sample, attn_norm, w_q_a, q_a_norm, w_q_b, w_kv_a, kv_a_norm, w_kv_b, q_norm, k_norm,
           w_attn_o, fnet_norm, w_fnet_o, mlp_norm, w_up, w_down):
    depth = mlp_norm.shape[0]
    n_attn = attn_norm.shape[0]
    kv_rank = kv_a_norm.shape[1]
    q_rank = q_a_norm.shape[1]
    row2 = lambda v: v[None, :].astype(F32)
    wqb = w_q_b.reshape(n_attn, q_rank, N_HEADS, QK_HEAD_DIM)
    wqb = jnp.concatenate([wqb[..., :QK_NOPE_DIM], _dup_rope_cols(wqb[..., QK_NOPE_DIM:])], axis=-1)
    wqb = wqb.reshape(n_attn, q_rank, N_HEADS * HEAD_PAD)
    wkva = jnp.concatenate([w_kv_a[..., :kv_rank], _dup_rope_cols(w_kv_a[..., kv_rank:])], axis=-1)
    layers = lambda w: [w[j].astype(BF16) for j in range(w.shape[0])]
    p = {
        "attn_norm": [row2(g) for g in attn_norm], "w_q_a": layers(w_q_a),
        "q_a_norm": [row2(g) for g in q_a_norm], "w_q_b": layers(wqb),
        "q_norm": [_pad_head_vec(g) for g in q_norm], "w_kv_a": layers(wkva),
        "kv_a_norm": [row2(g) for g in kv_a_norm], "w_kv_b": layers(w_kv_b),
        "k_norm": [_pad_head_vec(g) for g in k_norm], "fnet_norm": [row2(g) for g in fnet_norm],
        "w_mix_o": [(w_attn_o if i % 2 == 0 else w_fnet_o)[i // 2].astype(BF16) for i in range(depth)],
        "mlp_norm": [row2(g) for g in mlp_norm], "w_up": layers(w_up), "w_down": layers(w_down),
    }
    outs = []
    for xg in (x_prompt, x_sample):
        b, s, d = xg.shape
        outs.append(_trunk(xg.reshape(b * s, d), s, p).reshape(b, s, d))
    return tuple(outs)
```

```python
import functools
import math

import numpy as np
import jax
import jax.numpy as jnp
from jax import lax
from jax.experimental import pallas as pl
from jax.experimental.pallas import tpu as pltpu

F32 = jnp.float32
BF16 = jnp.bfloat16

N_HEADS = 16
QK_NOPE_DIM = 128
QK_ROPE_DIM = 64
QK_HEAD_DIM = QK_NOPE_DIM + QK_ROPE_DIM
V_HEAD_DIM = 128
HEAD_PAD = 256
BF16_ROWS = 16
V_ROWS = V_HEAD_DIM + BF16_ROWS
ROPE_THETA = 10000.0
FNET_GROUPS = 8
EPS = 1e-6
FFT_N2 = 128
LANES = 128
VMEM_CAP = 60 * 1024 * 1024


def _vmem_limit(nbytes):
    return int(min(max(nbytes, 16 * 1024 * 1024), VMEM_CAP))


def _params(sem, nbytes):
    return pltpu.CompilerParams(dimension_semantics=sem, vmem_limit_bytes=_vmem_limit(nbytes))


def _rms(xf, g):
    return xf * lax.rsqrt(jnp.mean(xf * xf, axis=-1, keepdims=True) + EPS) * g


def _dot(a, b):
    return jnp.dot(a, b, preferred_element_type=F32)


def _rope_dup(pe, cos_t, sin_t):
    return pe * cos_t + pltpu.roll(pe, 32, axis=1) * sin_t


def _qproj_kernel(x_ref, ga_ref, wqa_ref, gqa_ref, wqb_ref, gq_ref, cos_ref, sin_ref, q_ref, *, qscale):
    h = _rms(x_ref[...], ga_ref[...]).astype(BF16)
    cq = _rms(_dot(h, wqa_ref[...]), gqa_ref[...]).astype(BF16)
    cos_t = cos_ref[...]
    sin_t = sin_ref[...]
    g_nope = gq_ref[:, :QK_NOPE_DIM]
    g_pe = gq_ref[:, QK_NOPE_DIM:]
    for hh in range(N_HEADS):
        qh = _dot(cq, wqb_ref[:, hh * HEAD_PAD:(hh + 1) * HEAD_PAD])
        nope = qh[:, :QK_NOPE_DIM]
        pe = qh[:, QK_NOPE_DIM:]
        ss = jnp.sum(nope * nope + 0.5 * (pe * pe), axis=-1, keepdims=True)
        inv = lax.rsqrt(ss * (1.0 / QK_HEAD_DIM) + EPS) * qscale
        q_ref[hh, :QK_NOPE_DIM, :] = (nope * inv * g_nope).T.astype(BF16)
        q_ref[hh, QK_NOPE_DIM:, :] = _rope_dup(pe * inv * g_pe, cos_t, sin_t).T.astype(BF16)


def _kvproj_kernel(x_ref, ga_ref, wkva_ref, gkva_ref, wkvb_ref, gk_ref, cos_ref, sin_ref, k_ref, v_ref,
                   *, kv_rank):
    h = _rms(x_ref[...], ga_ref[...]).astype(BF16)
    ckv = _dot(h, wkva_ref[...])
    c = _rms(ckv[:, :kv_rank], gkva_ref[...]).astype(BF16)
    pe = ckv[:, kv_rank:]
    pe_ss = 0.5 * jnp.sum(pe * pe, axis=-1, keepdims=True)
    g_nope = gk_ref[:, :QK_NOPE_DIM]
    r = _rope_dup(pe * gk_ref[:, QK_NOPE_DIM:], cos_ref[...], sin_ref[...])
    for hh in range(N_HEADS):
        kvh = _dot(c, wkvb_ref[:, hh * 256:(hh + 1) * 256])
        k_nope = kvh[:, :QK_NOPE_DIM]
        ss = jnp.sum(k_nope * k_nope, axis=-1, keepdims=True) + pe_ss
        inv = lax.rsqrt(ss * (1.0 / QK_HEAD_DIM) + EPS)
        k_ref[hh, :, :QK_NOPE_DIM] = (k_nope * inv * g_nope).astype(BF16)
        k_ref[hh, :, QK_NOPE_DIM:] = (r * inv).astype(BF16)
        v_ref[hh, :V_HEAD_DIM, :] = kvh[:, QK_NOPE_DIM:].T.astype(BF16)
        v_ref[hh, V_HEAD_DIM:, :] = (lax.broadcasted_iota(jnp.int32, (BF16_ROWS, kvh.shape[0]), 0) == 0
                                     ).astype(BF16)


def _const_spec(shape):
    return pl.BlockSpec(shape, lambda i: (0,) * len(shape))


def _mla_project(x, ga, wqa, gqa, wqb, gq, wkva, gkva, wkvb, gk, cos_t, sin_t, tm):
    T, D = x.shape
    q_rank = wqa.shape[1]
    kv_rank = gkva.shape[1]
    qscale = QK_HEAD_DIM ** -0.5 * math.log2(math.e)
    row = lambda i: (i, 0)
    head_rows = lambda i: (0, i, 0)
    head_cols = lambda i: (0, 0, i)
    tab = pl.BlockSpec((tm, LANES), row)
    qt = pl.pallas_call(
        functools.partial(_qproj_kernel, qscale=qscale),
        grid=(T // tm,),
        in_specs=[pl.BlockSpec((tm, D), row), _const_spec((1, D)), _const_spec(wqa.shape),
                  _const_spec((1, q_rank)), _const_spec(wqb.shape), _const_spec((1, HEAD_PAD)), tab, tab],
        out_specs=pl.BlockSpec((N_HEADS, HEAD_PAD, tm), head_cols),
        out_shape=jax.ShapeDtypeStruct((N_HEADS, HEAD_PAD, T), BF16),
        compiler_params=_params(("parallel",), 2 * (tm * D * 4 + wqa.size * 2 + wqb.size * 2
                                                    + N_HEADS * tm * HEAD_PAD * 2) + 8 * tm * D * 4),
    )(x, ga, wqa, gqa, wqb, gq, cos_t, sin_t)
    k, vt = pl.pallas_call(
        functools.partial(_kvproj_kernel, kv_rank=kv_rank),
        grid=(T // tm,),
        in_specs=[pl.BlockSpec((tm, D), row), _const_spec((1, D)), _const_spec(wkva.shape),
                  _const_spec((1, kv_rank)), _const_spec(wkvb.shape), _const_spec((1, HEAD_PAD)), tab, tab],
        out_specs=[pl.BlockSpec((N_HEADS, tm, HEAD_PAD), head_rows),
                   pl.BlockSpec((N_HEADS, V_ROWS, tm), head_cols)],
        out_shape=[jax.ShapeDtypeStruct((N_HEADS, T, HEAD_PAD), BF16),
                   jax.ShapeDtypeStruct((N_HEADS, V_ROWS, T), BF16)],
        compiler_params=_params(("parallel",), 2 * (tm * D * 4 + wkva.size * 2 + wkvb.size * 2
                                                    + N_HEADS * tm * (HEAD_PAD + V_ROWS) * 2)
                                + 8 * tm * D * 4),
    )(x, ga, wkva, gkva, wkvb, gk, cos_t, sin_t)
    return qt, k, vt


def _flash_kernel(qt_ref, k_ref, vt_ref, o_ref, s_ref, *, tk, chunks_per_step):
    qt = qt_ref[0]
    tq = qt.shape[1]
    n_chunks = k_ref.shape[1] // tk

    def scores(j, slot):
        s = _dot(k_ref[0, pl.ds(pl.multiple_of(j * tk, tk), tk), :], qt)
        s_ref[slot] = s
        return jnp.max(s, axis=0, keepdims=True)

    def absorb(j, slot, s_max, m, acc):
        vt = vt_ref[0, :, pl.ds(pl.multiple_of(j * tk, tk), tk)]
        m_new = jnp.maximum(m, s_max)
        p = jnp.exp2(s_ref[slot] - m_new)
        acc = jnp.exp2(m - m_new) * acc + _dot(vt, p.astype(BF16))
        return m_new, acc

    def body(step, carry):
        s_max, m, acc = carry
        for c in range(chunks_per_step):
            j = step * chunks_per_step + c
            s_max_next = scores(jnp.minimum(j + 1, n_chunks - 1), (c + 1) % 2)
            m, acc = absorb(j, c % 2, s_max, m, acc)
            s_max = s_max_next
        return s_max, m, acc

    m0 = jnp.full((1, tq), -jnp.inf, F32)
    acc0 = jnp.zeros((V_ROWS, tq), F32)
    _, _, acc = lax.fori_loop(0, n_chunks // chunks_per_step, body, (scores(0, 0), m0, acc0))
    o_ref[...] = (acc[:V_HEAD_DIM] / acc[V_HEAD_DIM:V_HEAD_DIM + 1]).T.astype(BF16)


def _attention(qt, k, vt, seq_len, tq, tk):
    n_seq = k.shape[1] // seq_len
    nq = seq_len // tq
    n_chunks = seq_len // tk
    chunks_per_step = math.gcd(n_chunks, 16)
    if chunks_per_step == n_chunks and n_chunks % 4 == 0:
        chunks_per_step //= 2
    assert chunks_per_step % 2 == 0
    return pl.pallas_call(
        functools.partial(_flash_kernel, tk=tk, chunks_per_step=chunks_per_step),
        grid=(n_seq, N_HEADS, nq),
        in_specs=[pl.BlockSpec((1, HEAD_PAD, tq), lambda b, h, i: (h, 0, b * nq + i)),
                  pl.BlockSpec((1, seq_len, HEAD_PAD), lambda b, h, i: (h, b, 0)),
                  pl.BlockSpec((1, V_ROWS, seq_len), lambda b, h, i: (h, 0, b))],
        out_specs=pl.BlockSpec((tq, V_HEAD_DIM), lambda b, h, i: (b * nq + i, h)),
        out_shape=jax.ShapeDtypeStruct((n_seq * seq_len, N_HEADS * V_HEAD_DIM), BF16),
        scratch_shapes=[pltpu.VMEM((2, tk, tq), F32)],
        compiler_params=_params(("parallel", "parallel", "arbitrary"),
                                2 * seq_len * (HEAD_PAD + V_ROWS) * 2 + 8 * tq * tk * 4
                                + 4 * tq * HEAD_PAD * 2),
    )(qt, k, vt)


def _fnet_stage1_kernel(x_hbm, g_ref, wc_ref, w1_ref, tw_ref, ar_ref, ai_ref, xbuf, sem, u_ref,
                        *, n1, gdim):
    i = pl.program_id(0)
    slot = i % 2

    def gather(step, to_slot):
        rows = pl.ds((step // FFT_N2) * n1, n1)
        return pltpu.make_async_copy(x_hbm.at[rows, step % FFT_N2, :], xbuf.at[to_slot], sem.at[to_slot])

    @pl.when(i == 0)
    def _():
        gather(0, 0).start()

    @pl.when(i + 1 < pl.num_programs(0))
    def _():
        gather(i + 1, 1 - slot).start()

    gather(i, slot).wait()
    h = _rms(xbuf[slot], g_ref[...]).astype(BF16)
    wc = wc_ref[...]
    for gi in range(FNET_GROUPS):
        cols = slice(gi * gdim, (gi + 1) * gdim)
        u = _dot(h[:, cols], wc)
        u_ref[:n1, cols] = u[:, :gdim]
        u_ref[n1:, cols] = u[:, gdim:]
    a = _dot(w1_ref[...], u_ref[...].astype(BF16))
    a_re = a[:n1]
    a_im = a[n1:]
    tw_c = tw_ref[0, :, 0:1]
    tw_s = tw_ref[0, :, 1:2]
    ar_ref[...] = a_re * tw_c + a_im * tw_s
    ai_ref[...] = a_im * tw_c - a_re * tw_s


def _fnet_stage2_kernel(ar_hbm, ai_hbm, w2_ref, z_hbm, abuf, zbuf, in_sem, out_sem, *, n1):
    n2 = FFT_N2
    i = pl.program_id(0)
    n_steps = pl.num_programs(0)
    slot = i % 2

    def gathers(step, to_slot):
        rows = pl.ds((step // n1) * n2, n2)
        k1 = step % n1
        return (pltpu.make_async_copy(ar_hbm.at[rows, k1, :], abuf.at[to_slot, pl.ds(0, n2)],
                                      in_sem.at[to_slot, 0]),
                pltpu.make_async_copy(ai_hbm.at[rows, k1, :], abuf.at[to_slot, pl.ds(n2, n2)],
                                      in_sem.at[to_slot, 1]))

    def scatter(step, from_slot):
        rows = pl.ds((step // n1) * n2, n2)
        return pltpu.make_async_copy(zbuf.at[from_slot], z_hbm.at[rows, step % n1, :], out_sem.at[from_slot])

    @pl.when(i == 0)
    def _():
        for cp in gathers(0, 0):
            cp.start()

    @pl.when(i + 1 < n_steps)
    def _():
        for cp in gathers(i + 1, 1 - slot):
            cp.start()

    for cp in gathers(i, slot):
        cp.wait()
    z = _dot(w2_ref[...], abuf[slot].astype(BF16))

    @pl.when(i >= 2)
    def _():
        scatter(i - 2, slot).wait()

    zbuf[slot] = z
    scatter(i, slot).start()

    @pl.when(i == n_steps - 1)
    def _():
        scatter(i - 1, 1 - slot).wait()
        scatter(i, slot).wait()


def _dft_tables(n1, n2, gdim):
    def cs(rows, cols, period):
        ang = 2.0 * np.pi * ((np.arange(rows)[:, None] * np.arange(cols)[None, :]) % period) / period
        return np.cos(ang), np.sin(ang)
    cc, sc = cs(gdim, gdim, gdim)
    w_ch = np.concatenate([cc, -sc], axis=1) / np.sqrt(gdim)
    c1, s1 = cs(n1, n1, n1)
    w1 = np.block([[c1, s1], [-s1, c1]]) / np.sqrt(n1)
    c2, s2 = cs(n2, n2, n2)
    w2 = np.concatenate([c2, s2], axis=1) / np.sqrt(n2)
    twc, tws = cs(n2, n1, n1 * n2)
    tw = np.zeros((n2, n1, LANES), np.float64)
    tw[:, :, 0] = twc
    tw[:, :, 1] = tws
    return [jnp.asarray(t, F32) for t in (w_ch, w1, w2, tw)]


def _fnet_mix(x, g, seq_len):
    T, D = x.shape
    n_seq = T // seq_len
    n2 = FFT_N2
    n1 = seq_len // n2
    gdim = D // FNET_GROUPS
    w_ch, w1, w2, tw = _dft_tables(n1, n2, gdim)
    w_ch, w1, w2 = (t.astype(BF16) for t in (w_ch, w1, w2))
    const2 = lambda shape: pl.BlockSpec(shape, lambda i: (0, 0))
    any_spec = pl.BlockSpec(memory_space=pl.ANY)
    slab = pl.BlockSpec((None, n1, D), lambda i: (i, 0, 0))
    ar, ai = pl.pallas_call(
        functools.partial(_fnet_stage1_kernel, n1=n1, gdim=gdim),
        grid=(n_seq * n2,),
        in_specs=[any_spec, const2((1, D)), const2(w_ch.shape), const2(w1.shape),
                  pl.BlockSpec((1, n1, LANES), lambda i: (i % n2, 0, 0))],
        out_specs=[slab, slab],
        out_shape=[jax.ShapeDtypeStruct((n_seq * n2, n1, D), F32)] * 2,
        scratch_shapes=[pltpu.VMEM((2, n1, D), F32), pltpu.SemaphoreType.DMA((2,)),
                        pltpu.VMEM((2 * n1, D), F32)],
        compiler_params=_params(("arbitrary",), 24 * n1 * D * 4 + 16 * 1024 * 1024),
    )(x.reshape(T // n2, n2, D), g, w_ch, w1, tw)
    z = pl.pallas_call(
        functools.partial(_fnet_stage2_kernel, n1=n1),
        grid=(n_seq * n1,),
        in_specs=[any_spec, any_spec, const2(w2.shape)],
        out_specs=any_spec,
        out_shape=jax.ShapeDtypeStruct((n_seq * n2, n1, D), F32),
        scratch_shapes=[pltpu.VMEM((2, 2 * n2, D), F32), pltpu.VMEM((2, n2, D), F32),
                        pltpu.SemaphoreType.DMA((2, 2)), pltpu.SemaphoreType.DMA((2,))],
        compiler_params=_params(("arbitrary",), 24 * n2 * D * 4 + 16 * 1024 * 1024),
    )(ar, ai, w2)
    return z.reshape(T, D)


def _oproj_kernel(x_ref, a_ref, w_ref, o_ref):
    o_ref[...] = x_ref[...] + _dot(a_ref[...].astype(BF16), w_ref[...])


def _oproj_residual(x, a, w, tm):
    T, D = x.shape
    row = lambda i: (i, 0)
    return pl.pallas_call(
        _oproj_kernel,
        grid=(T // tm,),
        in_specs=[pl.BlockSpec((tm, D), row), pl.BlockSpec((tm, a.shape[1]), row), _const_spec(w.shape)],
        out_specs=pl.BlockSpec((tm, D), row),
        out_shape=jax.ShapeDtypeStruct((T, D), F32),
        compiler_params=_params(("parallel",), 2 * (2 * tm * D * 4 + tm * a.shape[1] * a.dtype.itemsize
                                                    + w.size * 2) + 2 * tm * D * 4),
    )(x, a, w)


def _mlp_kernel(x_ref, g_ref, wup_ref, wdn_ref, o_ref, xn_ref):
    @pl.when(pl.program_id(1) == 0)
    def _():
        x = x_ref[...]
        xn_ref[...] = _rms(x, g_ref[...]).astype(BF16)
        o_ref[...] = x

    hcol = jnp.square(jnp.maximum(_dot(xn_ref[...], wup_ref[...]), 0.0)).astype(BF16)
    o_ref[...] += _dot(hcol, wdn_ref[...])


def _mlp(x, g, w_up, w_down, tm, tf):
    T, D = x.shape
    dff = w_up.shape[1]
    return pl.pallas_call(
        _mlp_kernel,
        grid=(T // tm, dff // tf),
        in_specs=[pl.BlockSpec((tm, D), lambda i, f: (i, 0)), pl.BlockSpec((1, D), lambda i, f: (0, 0)),
                  pl.BlockSpec((D, tf), lambda i, f: (0, f)), pl.BlockSpec((tf, D), lambda i, f: (f, 0))],
        out_specs=pl.BlockSpec((tm, D), lambda i, f: (i, 0)),
        out_shape=jax.ShapeDtypeStruct((T, D), F32),
        scratch_shapes=[pltpu.VMEM((tm, D), BF16)],
        compiler_params=_params(("parallel", "arbitrary"),
                                2 * (2 * tm * D * 4 + 2 * D * tf * 2) + tm * D * 2 + 3 * tm * tf * 4),
    )(x, g, w_up, w_down)


def _dup_rope_cols(w):
    return jnp.concatenate([w, w], axis=-1)


def _pad_head_vec(g):
    return jnp.concatenate([g[:QK_NOPE_DIM], _dup_rope_cols(g[QK_NOPE_DIM:])])[None, :].astype(F32)


def _rope_tables(n_seq, seq_len):
    inv_freq = ROPE_THETA ** (-jnp.arange(0, QK_ROPE_DIM, 2, dtype=F32) / QK_ROPE_DIM)
    ang = jnp.arange(seq_len, dtype=F32)[:, None] * inv_freq[None, :]
    c, s = jnp.cos(ang), jnp.sin(ang)
    z = jnp.zeros_like(c)
    tile = lambda t: jnp.tile(t, (n_seq, 1))
    return tile(jnp.concatenate([c, c, z, z], axis=1)), tile(jnp.concatenate([-s, s, z, z], axis=1))


def _pick_tile(n, target):
    t = min(n, target)
    while n % t:
        t //= 2
    return t


def _trunk(x, seq_len, p):
    T = x.shape[0]
    cos_t, sin_t = _rope_tables(T // seq_len, seq_len)
    tm = _pick_tile(T, 256)
    tm_mlp = _pick_tile(T, 512)
    tf = _pick_tile(p["w_up"][0].shape[1], 1024)
    ts = _pick_tile(seq_len, 512)
    for i in range(len(p["w_up"])):
        j = i // 2
        if i % 2 == 0:
            qt, k, vt = _mla_project(x, p["attn_norm"][j], p["w_q_a"][j], p["q_a_norm"][j], p["w_q_b"][j],
                                     p["q_norm"][j], p["w_kv_a"][j], p["kv_a_norm"][j], p["w_kv_b"][j],
                                     p["k_norm"][j], cos_t, sin_t, tm)
            a = _attention(qt, k, vt, seq_len, ts, ts)
        else:
            a = _fnet_mix(x, p["fnet_norm"][j], seq_len)
        x = _oproj_residual(x, a, p["w_mix_o"][i], tm_mlp)
        x = _mlp(x, p["mlp_norm"][i], p["w_up"][i], p["w_down"][i], tm_mlp, tf)
    return x


def kernel(x_prompt, x_sample, attn_norm, w_q_a, q_a_norm, w_q_b, w_kv_a, kv_a_norm, w_kv_b, q_norm, k_norm,
           w_attn_o, fnet_norm, w_fnet_o, mlp_norm, w_up, w_down):
    depth = mlp_norm.shape[0]
    n_attn = attn_norm.shape[0]
    kv_rank = kv_a_norm.shape[1]
    q_rank = q_a_norm.shape[1]
    row2 = lambda v: v[None, :].astype(F32)
    wqb = w_q_b.reshape(n_attn, q_rank, N_HEADS, QK_HEAD_DIM)
    wqb = jnp.concatenate([wqb[..., :QK_NOPE_DIM], _dup_rope_cols(wqb[..., QK_NOPE_DIM:])], axis=-1)
    wqb = wqb.reshape(n_attn, q_rank, N_HEADS * HEAD_PAD)
    wkva = jnp.concatenate([w_kv_a[..., :kv_rank], _dup_rope_cols(w_kv_a[..., kv_rank:])], axis=-1)
    layers = lambda w: [w[j].astype(BF16) for j in range(w.shape[0])]
    p = {
        "attn_norm": [row2(g) for g in attn_norm], "w_q_a": layers(w_q_a),
        "q_a_norm": [row2(g) for g in q_a_norm], "w_q_b": layers(wqb),
        "q_norm": [_pad_head_vec(g) for g in q_norm], "w_kv_a": layers(wkva),
        "kv_a_norm": [row2(g) for g in kv_a_norm], "w_kv_b": layers(w_kv_b),
        "k_norm": [_pad_head_vec(g) for g in k_norm], "fnet_norm": [row2(g) for g in fnet_norm],
        "w_mix_o": [(w_attn_o if i % 2 == 0 else w_fnet_o)[i // 2].astype(BF16) for i in range(depth)],
        "mlp_norm": [row2(g) for g in mlp_norm], "w_up": layers(w_up), "w_down": layers(w_down),
    }
    outs = []
    for xg in (x_prompt, x_sample):
        b, s, d = xg.shape
        outs.append(_trunk(xg.reshape(b * s, d), s, p).reshape(b, s, d))
    return tuple(outs)
```

```python
import functools
import math

import numpy as np
import jax
import jax.numpy as jnp
from jax import lax
from jax.experimental import pallas as pl
from jax.experimental.pallas import tpu as pltpu

F32 = jnp.float32
BF16 = jnp.bfloat16

N_HEADS = 16
QK_NOPE_DIM = 128
QK_ROPE_DIM = 64
QK_HEAD_DIM = QK_NOPE_DIM + QK_ROPE_DIM
V_HEAD_DIM = 128
HEAD_PAD = 256
BF16_ROWS = 16
V_ROWS = V_HEAD_DIM + BF16_ROWS
ROPE_THETA = 10000.0
FNET_GROUPS = 8
EPS = 1e-6
FFT_N2 = 128
LANES = 128
SCORE_PAD_ROWS = 8
VMEM_CAP = 60 * 1024 * 1024


def _vmem_limit(nbytes):
    return int(min(max(nbytes, 16 * 1024 * 1024), VMEM_CAP))


def _params(sem, nbytes):
    return pltpu.CompilerParams(dimension_semantics=sem, vmem_limit_bytes=_vmem_limit(nbytes))


def _rms(xf, g):
    return xf * lax.rsqrt(jnp.mean(xf * xf, axis=-1, keepdims=True) + EPS) * g


def _dot(a, b):
    return jnp.dot(a, b, preferred_element_type=F32)


def _rope_dup(pe, cos_t, sin_t):
    return pe * cos_t + pltpu.roll(pe, 32, axis=1) * sin_t


def _qproj_kernel(x_ref, ga_ref, wqa_ref, gqa_ref, wqbt_ref, gq_ref, cos_ref, sin_ref, q_ref, *, qscale):
    h = _rms(x_ref[...], ga_ref[...]).astype(BF16)
    cq_t = _rms(_dot(h, wqa_ref[...]), gqa_ref[...]).T.astype(BF16)
    cos_t = cos_ref[...]
    sin_t = sin_ref[...]
    half = QK_ROPE_DIM // 2
    for hh in range(N_HEADS):
        qh = _dot(wqbt_ref[hh * HEAD_PAD:(hh + 1) * HEAD_PAD, :], cq_t)
        ss = jnp.sum(qh * qh, axis=0, keepdims=True)
        qh = qh * (lax.rsqrt(ss * (1.0 / QK_HEAD_DIM) + EPS) * qscale) * gq_ref[...]
        x1 = qh[QK_NOPE_DIM:QK_NOPE_DIM + half]
        x2 = qh[QK_NOPE_DIM + half:QK_HEAD_DIM]
        q_ref[hh, :QK_NOPE_DIM, :] = qh[:QK_NOPE_DIM].astype(BF16)
        q_ref[hh, QK_NOPE_DIM:QK_NOPE_DIM + half, :] = (x1 * cos_t - x2 * sin_t).astype(BF16)
        q_ref[hh, QK_NOPE_DIM + half:QK_HEAD_DIM, :] = (x2 * cos_t + x1 * sin_t).astype(BF16)
        q_ref[hh, QK_HEAD_DIM:, :] = jnp.zeros((HEAD_PAD - QK_HEAD_DIM, qh.shape[1]), BF16)


def _kvproj_kernel(x_ref, ga_ref, wkva_ref, gkva_ref, wkvb_ref, gk_ref, cos_ref, sin_ref, k_ref, v_ref,
                   *, kv_rank):
    h = _rms(x_ref[...], ga_ref[...]).astype(BF16)
    ckv = _dot(h, wkva_ref[...])
    c = _rms(ckv[:, :kv_rank], gkva_ref[...]).astype(BF16)
    pe = ckv[:, kv_rank:]
    pe_ss = 0.5 * jnp.sum(pe * pe, axis=-1, keepdims=True)
    g_nope = gk_ref[:, :QK_NOPE_DIM]
    r = _rope_dup(pe * gk_ref[:, QK_NOPE_DIM:], cos_ref[...], sin_ref[...])
    for hh in range(N_HEADS):
        kvh = _dot(c, wkvb_ref[:, hh * 256:(hh + 1) * 256])
        k_nope = kvh[:, :QK_NOPE_DIM]
        ss = jnp.sum(k_nope * k_nope, axis=-1, keepdims=True) + pe_ss
        inv = lax.rsqrt(ss * (1.0 / QK_HEAD_DIM) + EPS)
        k_ref[hh, :, :QK_NOPE_DIM] = (k_nope * inv * g_nope).astype(BF16)
        k_ref[hh, :, QK_NOPE_DIM:] = (r * inv).astype(BF16)
        v_ref[hh, :V_HEAD_DIM, :] = kvh[:, QK_NOPE_DIM:].T.astype(BF16)
        v_ref[hh, V_HEAD_DIM:, :] = (lax.broadcasted_iota(jnp.int32, (BF16_ROWS, kvh.shape[0]), 0) == 0
                                     ).astype(BF16)


def _const_spec(shape):
    return pl.BlockSpec(shape, lambda i: (0,) * len(shape))


def _mla_project(x, ga, wqa, gqa, wqbt, gq, wkva, gkva, wkvb, gk, rope, tm):
    T, D = x.shape
    q_rank = wqa.shape[1]
    kv_rank = gkva.shape[1]
    qscale = QK_HEAD_DIM ** -0.5 * math.log2(math.e)
    row = lambda i: (i, 0)
    head_rows = lambda i: (0, i, 0)
    head_cols = lambda i: (0, 0, i)
    tab = pl.BlockSpec((tm, LANES), row)
    tab_t = pl.BlockSpec((QK_ROPE_DIM // 2, tm), lambda i: (0, i))
    cos_t, sin_t, cos_dup, sin_dup = rope
    qt = pl.pallas_call(
        functools.partial(_qproj_kernel, qscale=qscale),
        grid=(T // tm,),
        in_specs=[pl.BlockSpec((tm, D), row), _const_spec((1, D)), _const_spec(wqa.shape),
                  _const_spec((1, q_rank)), _const_spec(wqbt.shape), _const_spec((HEAD_PAD, 1)), tab_t, tab_t],
        out_specs=pl.BlockSpec((N_HEADS, HEAD_PAD, tm), head_cols),
        out_shape=jax.ShapeDtypeStruct((N_HEADS, HEAD_PAD, T), BF16),
        compiler_params=_params(("parallel",), 2 * (tm * D * 4 + wqa.size * 2 + wqbt.size * 2
                                                    + N_HEADS * tm * HEAD_PAD * 2) + 8 * tm * D * 4),
    )(x, ga, wqa, gqa, wqbt, gq, cos_t, sin_t)
    k, vt = pl.pallas_call(
        functools.partial(_kvproj_kernel, kv_rank=kv_rank),
        grid=(T // tm,),
        in_specs=[pl.BlockSpec((tm, D), row), _const_spec((1, D)), _const_spec(wkva.shape),
                  _const_spec((1, kv_rank)), _const_spec(wkvb.shape), _const_spec((1, HEAD_PAD)), tab, tab],
        out_specs=[pl.BlockSpec((N_HEADS, tm, HEAD_PAD), head_rows),
                   pl.BlockSpec((N_HEADS, V_ROWS, tm), head_cols)],
        out_shape=[jax.ShapeDtypeStruct((N_HEADS, T, HEAD_PAD), BF16),
                   jax.ShapeDtypeStruct((N_HEADS, V_ROWS, T), BF16)],
        compiler_params=_params(("parallel",), 2 * (tm * D * 4 + wkva.size * 2 + wkvb.size * 2
                                                    + N_HEADS * tm * (HEAD_PAD + V_ROWS) * 2)
                                + 8 * tm * D * 4),
    )(x, ga, wkva, gkva, wkvb, gk, cos_dup, sin_dup)
    return qt, k, vt


def _flash_kernel(qt_ref, k_ref, vt_ref, o_ref, s_ref, *, tk, chunks_per_step):
    qt = qt_ref[0]
    tq = qt.shape[1]
    n_chunks = k_ref.shape[1] // tk

    def scores(j, slot):
        s = _dot(k_ref[0, pl.ds(pl.multiple_of(j * tk, tk), tk), :], qt)
        s_ref[slot, :tk] = s
        return jnp.max(s, axis=0, keepdims=True)

    def absorb(j, slot, s_max, m, acc):
        vt = vt_ref[0, :, pl.ds(pl.multiple_of(j * tk, tk), tk)]
        m_new = jnp.maximum(m, s_max)
        p = jnp.exp2(s_ref[slot, :tk] - m_new)
        acc = jnp.exp2(m - m_new) * acc + _dot(vt, p.astype(BF16))
        return m_new, acc

    def body(step, carry):
        s_max, m, acc = carry
        for c in range(chunks_per_step):
            j = step * chunks_per_step + c
            s_max_next = scores(jnp.minimum(j + 1, n_chunks - 1), (c + 1) % 2)
            m, acc = absorb(j, c % 2, s_max, m, acc)
            s_max = s_max_next
        return s_max, m, acc

    m0 = jnp.full((1, tq), -jnp.inf, F32)
    acc0 = jnp.zeros((V_ROWS, tq), F32)
    _, _, acc = lax.fori_loop(0, n_chunks // chunks_per_step, body, (scores(0, 0), m0, acc0))
    o_ref[...] = (acc[:V_HEAD_DIM] / acc[V_HEAD_DIM:V_HEAD_DIM + 1]).T.astype(BF16)


def _attention(qt, k, vt, seq_len, tq, tk):
    n_seq = k.shape[1] // seq_len
    nq = seq_len // tq
    n_chunks = seq_len // tk
    chunks_per_step = math.gcd(n_chunks, 16)
    if chunks_per_step == n_chunks and n_chunks % 4 == 0:
        chunks_per_step //= 2
    assert chunks_per_step % 2 == 0
    return pl.pallas_call(
        functools.partial(_flash_kernel, tk=tk, chunks_per_step=chunks_per_step),
        grid=(n_seq, N_HEADS, nq),
        in_specs=[pl.BlockSpec((1, HEAD_PAD, tq), lambda b, h, i: (h, 0, b * nq + i)),
                  pl.BlockSpec((1, seq_len, HEAD_PAD), lambda b, h, i: (h, b, 0)),
                  pl.BlockSpec((1, V_ROWS, seq_len), lambda b, h, i: (h, 0, b))],
        out_specs=pl.BlockSpec((tq, V_HEAD_DIM), lambda b, h, i: (b * nq + i, h)),
        out_shape=jax.ShapeDtypeStruct((n_seq * seq_len, N_HEADS * V_HEAD_DIM), BF16),
        scratch_shapes=[pltpu.VMEM((2, tk + SCORE_PAD_ROWS, tq), F32)],
        compiler_params=_params(("parallel", "parallel", "arbitrary"),
                                2 * seq_len * (HEAD_PAD + V_ROWS) * 2 + 8 * tq * tk * 4
                                + 4 * tq * HEAD_PAD * 2),
    )(qt, k, vt)


def _fnet_stage1_kernel(x_hbm, g_ref, wc_ref, w1_ref, tw_ref, ar_ref, ai_ref, xbuf, sem, u_ref,
                        *, n1, gdim):
    i = pl.program_id(0)
    slot = i % 2

    def gather(step, to_slot):
        rows = pl.ds((step // FFT_N2) * n1, n1)
        return pltpu.make_async_copy(x_hbm.at[rows, step % FFT_N2, :], xbuf.at[to_slot], sem.at[to_slot])

    @pl.when(i == 0)
    def _():
        gather(0, 0).start()

    @pl.when(i + 1 < pl.num_programs(0))
    def _():
        gather(i + 1, 1 - slot).start()

    gather(i, slot).wait()
    h = _rms(xbuf[slot], g_ref[...]).astype(BF16)
    wc = wc_ref[...]
    for gi in range(FNET_GROUPS):
        cols = slice(gi * gdim, (gi + 1) * gdim)
        u = _dot(h[:, cols], wc)
        u_ref[:n1, cols] = u[:, :gdim]
        u_ref[n1:, cols] = u[:, gdim:]
    a = _dot(w1_ref[...], u_ref[...].astype(BF16))
    a_re = a[:n1]
    a_im = a[n1:]
    tw_c = tw_ref[0, :, 0:1]
    tw_s = tw_ref[0, :, 1:2]
    ar_ref[...] = a_re * tw_c + a_im * tw_s
    ai_ref[...] = a_im * tw_c - a_re * tw_s


def _fnet_stage2_kernel(ar_hbm, ai_hbm, w2_ref, z_hbm, abuf, zbuf, in_sem, out_sem, *, n1):
    n2 = FFT_N2
    i = pl.program_id(0)
    n_steps = pl.num_programs(0)
    slot = i % 2

    def gathers(step, to_slot):
        rows = pl.ds((step // n1) * n2, n2)
        k1 = step % n1
        return (pltpu.make_async_copy(ar_hbm.at[rows, k1, :], abuf.at[to_slot, pl.ds(0, n2)],
                                      in_sem.at[to_slot, 0]),
                pltpu.make_async_copy(ai_hbm.at[rows, k1, :], abuf.at[to_slot, pl.ds(n2, n2)],
                                      in_sem.at[to_slot, 1]))

    def scatter(step, from_slot):
        rows = pl.ds((step // n1) * n2, n2)
        return pltpu.make_async_copy(zbuf.at[from_slot], z_hbm.at[rows, step % n1, :], out_sem.at[from_slot])

    @pl.when(i == 0)
    def _():
        for cp in gathers(0, 0):
            cp.start()

    @pl.when(i + 1 < n_steps)
    def _():
        for cp in gathers(i + 1, 1 - slot):
            cp.start()

    for cp in gathers(i, slot):
        cp.wait()
    z = _dot(w2_ref[...], abuf[slot].astype(BF16))

    @pl.when(i >= 2)
    def _():
        scatter(i - 2, slot).wait()

    zbuf[slot] = z
    scatter(i, slot).start()

    @pl.when(i == n_steps - 1)
    def _():
        scatter(i - 1, 1 - slot).wait()
        scatter(i, slot).wait()


def _dft_tables(n1, n2, gdim):
    def cs(rows, cols, period):
        ang = 2.0 * np.pi * ((np.arange(rows)[:, None] * np.arange(cols)[None, :]) % period) / period
        return np.cos(ang), np.sin(ang)
    cc, sc = cs(gdim, gdim, gdim)
    w_ch = np.concatenate([cc, -sc], axis=1) / np.sqrt(gdim)
    c1, s1 = cs(n1, n1, n1)
    w1 = np.block([[c1, s1], [-s1, c1]]) / np.sqrt(n1)
    c2, s2 = cs(n2, n2, n2)
    w2 = np.concatenate([c2, s2], axis=1) / np.sqrt(n2)
    twc, tws = cs(n2, n1, n1 * n2)
    tw = np.zeros((n2, n1, LANES), np.float64)
    tw[:, :, 0] = twc
    tw[:, :, 1] = tws
    return [jnp.asarray(t, F32) for t in (w_ch, w1, w2, tw)]


def _fnet_mix(x, g, seq_len):
    T, D = x.shape
    n_seq = T // seq_len
    n2 = FFT_N2
    n1 = seq_len // n2
    gdim = D // FNET_GROUPS
    w_ch, w1, w2, tw = _dft_tables(n1, n2, gdim)
    w_ch, w1, w2 = (t.astype(BF16) for t in (w_ch, w1, w2))
    const2 = lambda shape: pl.BlockSpec(shape, lambda i: (0, 0))
    any_spec = pl.BlockSpec(memory_space=pl.ANY)
    slab = pl.BlockSpec((None, n1, D), lambda i: (i, 0, 0))
    ar, ai = pl.pallas_call(
        functools.partial(_fnet_stage1_kernel, n1=n1, gdim=gdim),
        grid=(n_seq * n2,),
        in_specs=[any_spec, const2((1, D)), const2(w_ch.shape), const2(w1.shape),
                  pl.BlockSpec((1, n1, LANES), lambda i: (i % n2, 0, 0))],
        out_specs=[slab, slab],
        out_shape=[jax.ShapeDtypeStruct((n_seq * n2, n1, D), F32)] * 2,
        scratch_shapes=[pltpu.VMEM((2, n1, D), F32), pltpu.SemaphoreType.DMA((2,)),
                        pltpu.VMEM((2 * n1, D), F32)],
        compiler_params=_params(("arbitrary",), 24 * n1 * D * 4 + 16 * 1024 * 1024),
    )(x.reshape(T // n2, n2, D), g, w_ch, w1, tw)
    z = pl.pallas_call(
        functools.partial(_fnet_stage2_kernel, n1=n1),
        grid=(n_seq * n1,),
        in_specs=[any_spec, any_spec, const2(w2.shape)],
        out_specs=any_spec,
        out_shape=jax.ShapeDtypeStruct((n_seq * n2, n1, D), F32),
        scratch_shapes=[pltpu.VMEM((2, 2 * n2, D), F32), pltpu.VMEM((2, n2, D), F32),
                        pltpu.SemaphoreType.DMA((2, 2)), pltpu.SemaphoreType.DMA((2,))],
        compiler_params=_params(("arbitrary",), 24 * n2 * D * 4 + 16 * 1024 * 1024),
    )(ar, ai, w2)
    return z.reshape(T, D)


def _oproj_kernel(x_ref, a_ref, w_ref, o_ref):
    o_ref[...] = x_ref[...] + _dot(a_ref[...].astype(BF16), w_ref[...])


def _oproj_residual(x, a, w, tm):
    T, D = x.shape
    row = lambda i: (i, 0)
    return pl.pallas_call(
        _oproj_kernel,
        grid=(T // tm,),
        in_specs=[pl.BlockSpec((tm, D), row), pl.BlockSpec((tm, a.shape[1]), row), _const_spec(w.shape)],
        out_specs=pl.BlockSpec((tm, D), row),
        out_shape=jax.ShapeDtypeStruct((T, D), F32),
        compiler_params=_params(("parallel",), 2 * (2 * tm * D * 4 + tm * a.shape[1] * a.dtype.itemsize
                                                    + w.size * 2) + 2 * tm * D * 4),
    )(x, a, w)


def _mlp_kernel(x_ref, g_ref, wup_ref, wdn_ref, o_ref, xn_ref):
    @pl.when(pl.program_id(1) == 0)
    def _():
        x = x_ref[...]
        xn_ref[...] = _rms(x, g_ref[...]).astype(BF16)
        o_ref[...] = x

    hcol = jnp.square(jnp.maximum(_dot(xn_ref[...], wup_ref[...]), 0.0)).astype(BF16)
    o_ref[...] += _dot(hcol, wdn_ref[...])


def _mlp(x, g, w_up, w_down, tm, tf):
    T, D = x.shape
    dff = w_up.shape[1]
    return pl.pallas_call(
        _mlp_kernel,
        grid=(T // tm, dff // tf),
        in_specs=[pl.BlockSpec((tm, D), lambda i, f: (i, 0)), pl.BlockSpec((1, D), lambda i, f: (0, 0)),
                  pl.BlockSpec((D, tf), lambda i, f: (0, f)), pl.BlockSpec((tf, D), lambda i, f: (f, 0))],
        out_specs=pl.BlockSpec((tm, D), lambda i, f: (i, 0)),
        out_shape=jax.ShapeDtypeStruct((T, D), F32),
        scratch_shapes=[pltpu.VMEM((tm, D), BF16)],
        compiler_params=_params(("parallel", "arbitrary"),
                                2 * (2 * tm * D * 4 + 2 * D * tf * 2) + tm * D * 2 + 3 * tm * tf * 4),
    )(x, g, w_up, w_down)


def _dup_rope_cols(w):
    return jnp.concatenate([w, w], axis=-1)


def _pad_head_vec(g):
    return jnp.concatenate([g[:QK_NOPE_DIM], _dup_rope_cols(g[QK_NOPE_DIM:])])[None, :].astype(F32)


def _rope_tables(n_seq, seq_len):
    inv_freq = ROPE_THETA ** (-jnp.arange(0, QK_ROPE_DIM, 2, dtype=F32) / QK_ROPE_DIM)
    ang = jnp.arange(seq_len, dtype=F32)[:, None] * inv_freq[None, :]
    c, s = jnp.cos(ang), jnp.sin(ang)
    z = jnp.zeros_like(c)
    tile = lambda t: jnp.tile(t, (n_seq, 1))
    return (tile(c).T, tile(s).T,
            tile(jnp.concatenate([c, c, z, z], axis=1)), tile(jnp.concatenate([-s, s, z, z], axis=1)))


def _pick_tile(n, target):
    t = min(n, target)
    while n % t:
        t //= 2
    return t


def _trunk(x, seq_len, p):
    T = x.shape[0]
    rope = _rope_tables(T // seq_len, seq_len)
    tm = _pick_tile(T, 512)
    tf = _pick_tile(p["w_up"][0].shape[1], 1024)
    ts = _pick_tile(seq_len, 512)
    for i in range(len(p["w_up"])):
        j = i // 2
        if i % 2 == 0:
            qt, k, vt = _mla_project(x, p["attn_norm"][j], p["w_q_a"][j], p["q_a_norm"][j], p["w_q_b"][j],
                                     p["q_norm"][j], p["w_kv_a"][j], p["kv_a_norm"][j], p["w_kv_b"][j],
                                     p["k_norm"][j], rope, tm)
            a = _attention(qt, k, vt, seq_len, ts, ts)
        else:
            a = _fnet_mix(x, p["fnet_norm"][j], seq_len)
        x = _oproj_residual(x, a, p["w_mix_o"][i], tm)
        x = _mlp(x, p["mlp_norm"][i], p["w_up"][i], p["w_down"][i], tm, tf)
    return x


def kernel(x_prompt, x_sample, attn_norm, w_q_a, q_a_norm, w_q_b, w_kv_a, kv_a_norm, w_kv_b, q_norm, k_norm,
           w_attn_o, fnet_norm, w_fnet_o, mlp_norm, w_up, w_down):
    depth = mlp_norm.shape[0]
    n_attn = attn_norm.shape[0]
    kv_rank = kv_a_norm.shape[1]
    q_rank = q_a_norm.shape[1]
    row2 = lambda v: v[None, :].astype(F32)
    wqbt = jnp.swapaxes(w_q_b.reshape(n_attn, q_rank, N_HEADS, QK_HEAD_DIM), 1, 3)
    wqbt = jnp.pad(jnp.swapaxes(wqbt, 1, 2), ((0, 0), (0, 0), (0, HEAD_PAD - QK_HEAD_DIM), (0, 0)))
    wqbt = wqbt.reshape(n_attn, N_HEADS * HEAD_PAD, q_rank)
    q_gain = lambda g: jnp.pad(g, (0, HEAD_PAD - QK_HEAD_DIM))[:, None].astype(F32)
    wkva = jnp.concatenate([w_kv_a[..., :kv_rank], _dup_rope_cols(w_kv_a[..., kv_rank:])], axis=-1)
    layers = lambda w: [w[j].astype(BF16) for j in range(w.shape[0])]
    p = {
        "attn_norm": [row2(g) for g in attn_norm], "w_q_a": layers(w_q_a),
        "q_a_norm": [row2(g) for g in q_a_norm], "w_q_b": layers(wqbt),
        "q_norm": [q_gain(g) for g in q_norm], "w_kv_a": layers(wkva),
        "kv_a_norm": [row2(g) for g in kv_a_norm], "w_kv_b": layers(w_kv_b),
        "k_norm": [_pad_head_vec(g) for g in k_norm], "fnet_norm": [row2(g) for g in fnet_norm],
        "w_mix_o": [(w_attn_o if i % 2 == 0 else w_fnet_o)[i // 2].astype(BF16) for i in range(depth)],
        "mlp_norm": [row2(g) for g in mlp_norm], "w_up": layers(w_up), "w_down": layers(w_down),
    }
    outs = []
    for xg in (x_prompt, x_sample):
        b, s, d = xg.shape
        outs.append(_trunk(xg.reshape(b * s, d), s, p).reshape(b, s, d))
    return tuple(outs)
```

```python
import functools
import math

import numpy as np
import jax
import jax.numpy as jnp
from jax import lax
from jax.experimental import pallas as pl
from jax.experimental.pallas import tpu as pltpu

F32 = jnp.float32
BF16 = jnp.bfloat16

N_HEADS = 16
QK_NOPE_DIM = 128
QK_ROPE_DIM = 64
QK_HEAD_DIM = QK_NOPE_DIM + QK_ROPE_DIM
V_HEAD_DIM = 128
HEAD_PAD = 256
BF16_ROWS = 16
V_ROWS = V_HEAD_DIM + BF16_ROWS
ROPE_THETA = 10000.0
FNET_GROUPS = 8
EPS = 1e-6
FFT_N2 = 128
LANES = 128
SCORE_PAD_ROWS = 8
VMEM_CAP = 60 * 1024 * 1024


def _vmem_limit(nbytes):
    return int(min(max(nbytes, 16 * 1024 * 1024), VMEM_CAP))


def _params(sem, nbytes):
    return pltpu.CompilerParams(dimension_semantics=sem, vmem_limit_bytes=_vmem_limit(nbytes))


def _rms(xf, g):
    return xf * lax.rsqrt(jnp.mean(xf * xf, axis=-1, keepdims=True) + EPS) * g


def _dot(a, b):
    return jnp.dot(a, b, preferred_element_type=F32)


def _rope_dup(pe, cos_t, sin_t):
    return pe * cos_t + pltpu.roll(pe, 32, axis=1) * sin_t


def _qproj_kernel(x_ref, ga_ref, wqa_ref, gqa_ref, wqbt_ref, gq_ref, cos_ref, sin_ref, q_ref, *, qscale):
    h = _rms(x_ref[...], ga_ref[...]).astype(BF16)
    cq_t = _rms(_dot(h, wqa_ref[...]), gqa_ref[...]).T.astype(BF16)
    cos_t = cos_ref[...]
    sin_t = sin_ref[...]
    half = QK_ROPE_DIM // 2
    for hh in range(N_HEADS):
        qh = _dot(wqbt_ref[hh * HEAD_PAD:(hh + 1) * HEAD_PAD, :], cq_t)
        ss = jnp.sum(qh * qh, axis=0, keepdims=True)
        qh = qh * (lax.rsqrt(ss * (1.0 / QK_HEAD_DIM) + EPS) * qscale) * gq_ref[...]
        x1 = qh[QK_NOPE_DIM:QK_NOPE_DIM + half]
        x2 = qh[QK_NOPE_DIM + half:QK_HEAD_DIM]
        q_ref[hh, :QK_NOPE_DIM, :] = qh[:QK_NOPE_DIM].astype(BF16)
        q_ref[hh, QK_NOPE_DIM:QK_NOPE_DIM + half, :] = (x1 * cos_t - x2 * sin_t).astype(BF16)
        q_ref[hh, QK_NOPE_DIM + half:QK_HEAD_DIM, :] = (x2 * cos_t + x1 * sin_t).astype(BF16)
        q_ref[hh, QK_HEAD_DIM:, :] = jnp.zeros((HEAD_PAD - QK_HEAD_DIM, qh.shape[1]), BF16)


def _kvproj_kernel(x_ref, ga_ref, wkva_ref, gkva_ref, wkvb_ref, gk_ref, cos_ref, sin_ref, k_ref, v_ref,
                   *, kv_rank):
    h = _rms(x_ref[...], ga_ref[...]).astype(BF16)
    ckv = _dot(h, wkva_ref[...])
    c = _rms(ckv[:, :kv_rank], gkva_ref[...]).astype(BF16)
    pe = ckv[:, kv_rank:]
    pe_ss = 0.5 * jnp.sum(pe * pe, axis=-1, keepdims=True)
    g_nope = gk_ref[:, :QK_NOPE_DIM]
    r = _rope_dup(pe * gk_ref[:, QK_NOPE_DIM:], cos_ref[...], sin_ref[...])
    for hh in range(N_HEADS):
        kvh = _dot(c, wkvb_ref[:, hh * 256:(hh + 1) * 256])
        k_nope = kvh[:, :QK_NOPE_DIM]
        ss = jnp.sum(k_nope * k_nope, axis=-1, keepdims=True) + pe_ss
        inv = lax.rsqrt(ss * (1.0 / QK_HEAD_DIM) + EPS)
        k_ref[hh, :, :QK_NOPE_DIM] = (k_nope * inv * g_nope).astype(BF16)
        k_ref[hh, :, QK_NOPE_DIM:] = (r * inv).astype(BF16)
        v_ref[hh, :V_HEAD_DIM, :] = kvh[:, QK_NOPE_DIM:].T.astype(BF16)
        v_ref[hh, V_HEAD_DIM:, :] = (lax.broadcasted_iota(jnp.int32, (BF16_ROWS, kvh.shape[0]), 0) == 0
                                     ).astype(BF16)


def _const_spec(shape):
    return pl.BlockSpec(shape, lambda i: (0,) * len(shape))


def _mla_project(x, ga, wqa, gqa, wqbt, gq, wkva, gkva, wkvb, gk, rope, tm):
    T, D = x.shape
    q_rank = wqa.shape[1]
    kv_rank = gkva.shape[1]
    qscale = QK_HEAD_DIM ** -0.5 * math.log2(math.e)
    row = lambda i: (i, 0)
    head_rows = lambda i: (0, i, 0)
    head_cols = lambda i: (0, 0, i)
    tab = pl.BlockSpec((tm, LANES), row)
    tab_t = pl.BlockSpec((QK_ROPE_DIM // 2, tm), lambda i: (0, i))
    cos_t, sin_t, cos_dup, sin_dup = rope
    qt = pl.pallas_call(
        functools.partial(_qproj_kernel, qscale=qscale),
        grid=(T // tm,),
        in_specs=[pl.BlockSpec((tm, D), row), _const_spec((1, D)), _const_spec(wqa.shape),
                  _const_spec((1, q_rank)), _const_spec(wqbt.shape), _const_spec((HEAD_PAD, 1)), tab_t, tab_t],
        out_specs=pl.BlockSpec((N_HEADS, HEAD_PAD, tm), head_cols),
        out_shape=jax.ShapeDtypeStruct((N_HEADS, HEAD_PAD, T), BF16),
        compiler_params=_params(("parallel",), 2 * (tm * D * 4 + wqa.size * 2 + wqbt.size * 2
                                                    + N_HEADS * tm * HEAD_PAD * 2) + 8 * tm * D * 4),
    )(x, ga, wqa, gqa, wqbt, gq, cos_t, sin_t)
    k, vt = pl.pallas_call(
        functools.partial(_kvproj_kernel, kv_rank=kv_rank),
        grid=(T // tm,),
        in_specs=[pl.BlockSpec((tm, D), row), _const_spec((1, D)), _const_spec(wkva.shape),
                  _const_spec((1, kv_rank)), _const_spec(wkvb.shape), _const_spec((1, HEAD_PAD)), tab, tab],
        out_specs=[pl.BlockSpec((N_HEADS, tm, HEAD_PAD), head_rows),
                   pl.BlockSpec((N_HEADS, V_ROWS, tm), head_cols)],
        out_shape=[jax.ShapeDtypeStruct((N_HEADS, T, HEAD_PAD), BF16),
                   jax.ShapeDtypeStruct((N_HEADS, V_ROWS, T), BF16)],
        compiler_params=_params(("parallel",), 2 * (tm * D * 4 + wkva.size * 2 + wkvb.size * 2
                                                    + N_HEADS * tm * (HEAD_PAD + V_ROWS) * 2)
                                + 8 * tm * D * 4),
    )(x, ga, wkva, gkva, wkvb, gk, cos_dup, sin_dup)
    return qt, k, vt


def _flash_kernel(qt_ref, k_ref, vt_ref, o_ref, s_ref, *, tk, chunks_per_step):
    qt = qt_ref[0]
    tq = qt.shape[1]
    n_chunks = k_ref.shape[1] // tk

    def scores(j, slot):
        s = _dot(k_ref[0, pl.ds(pl.multiple_of(j * tk, tk), tk), :], qt)
        s_ref[slot, :tk] = s
        return jnp.max(s, axis=0, keepdims=True)

    def absorb(j, slot, s_max, m, acc):
        vt = vt_ref[0, :, pl.ds(pl.multiple_of(j * tk, tk), tk)]
        m_new = jnp.maximum(m, s_max)
        p = jnp.exp2(s_ref[slot, :tk] - m_new)
        acc = jnp.exp2(m - m_new) * acc + _dot(vt, p.astype(BF16))
        return m_new, acc

    def body(step, carry):
        s_max, m, acc = carry
        for c in range(chunks_per_step):
            j = step * chunks_per_step + c
            s_max_next = scores(jnp.minimum(j + 1, n_chunks - 1), (c + 1) % 2)
            m, acc = absorb(j, c % 2, s_max, m, acc)
            s_max = s_max_next
        return s_max, m, acc

    m0 = jnp.full((1, tq), -jnp.inf, F32)
    acc0 = jnp.zeros((V_ROWS, tq), F32)
    _, _, acc = lax.fori_loop(0, n_chunks // chunks_per_step, body, (scores(0, 0), m0, acc0))
    o_ref[...] = (acc[:V_HEAD_DIM] / acc[V_HEAD_DIM:V_HEAD_DIM + 1]).T.astype(BF16)


def _attention(qt, k, vt, seq_len, tq, tk):
    n_seq = k.shape[1] // seq_len
    nq = seq_len // tq
    n_chunks = seq_len // tk
    chunks_per_step = math.gcd(n_chunks, 16)
    if chunks_per_step == n_chunks and n_chunks % 4 == 0:
        chunks_per_step //= 2
    assert chunks_per_step % 2 == 0
    return pl.pallas_call(
        functools.partial(_flash_kernel, tk=tk, chunks_per_step=chunks_per_step),
        grid=(n_seq, N_HEADS, nq),
        in_specs=[pl.BlockSpec((1, HEAD_PAD, tq), lambda b, h, i: (h, 0, b * nq + i)),
                  pl.BlockSpec((1, seq_len, HEAD_PAD), lambda b, h, i: (h, b, 0)),
                  pl.BlockSpec((1, V_ROWS, seq_len), lambda b, h, i: (h, 0, b))],
        out_specs=pl.BlockSpec((tq, V_HEAD_DIM), lambda b, h, i: (b * nq + i, h)),
        out_shape=jax.ShapeDtypeStruct((n_seq * seq_len, N_HEADS * V_HEAD_DIM), BF16),
        scratch_shapes=[pltpu.VMEM((2, tk + SCORE_PAD_ROWS, tq), F32)],
        compiler_params=_params(("parallel", "parallel", "arbitrary"),
                                2 * seq_len * (HEAD_PAD + V_ROWS) * 2 + 8 * tq * tk * 4
                                + 4 * tq * HEAD_PAD * 2),
    )(qt, k, vt)


def _pack_bf16_pair(hi, lo):
    bits = lambda v: lax.bitcast_convert_type(v.astype(BF16).astype(F32), jnp.uint32)
    return bits(hi) | (bits(lo) >> 16)


def _unpack_bf16_pair(w):
    hi = lax.bitcast_convert_type(w & jnp.uint32(0xFFFF0000), F32)
    lo = lax.bitcast_convert_type(w << 16, F32)
    return hi.astype(BF16), lo.astype(BF16)


def _fnet_stage1_kernel(x_hbm, g_ref, wc_ref, w1_ref, tw_ref, a_ref, xbuf, sem, u_ref, *, n1, gdim):
    i = pl.program_id(0)
    slot = i % 2

    def gather(step, to_slot):
        rows = pl.ds((step // FFT_N2) * n1, n1)
        return pltpu.make_async_copy(x_hbm.at[rows, step % FFT_N2, :], xbuf.at[to_slot], sem.at[to_slot])

    @pl.when(i == 0)
    def _():
        gather(0, 0).start()

    @pl.when(i + 1 < pl.num_programs(0))
    def _():
        gather(i + 1, 1 - slot).start()

    gather(i, slot).wait()
    h = _rms(xbuf[slot], g_ref[...]).astype(BF16)
    wc = wc_ref[...]
    for gi in range(FNET_GROUPS):
        cols = slice(gi * gdim, (gi + 1) * gdim)
        u = _dot(h[:, cols], wc)
        u_ref[:n1, cols] = u[:, :gdim]
        u_ref[n1:, cols] = u[:, gdim:]
    a = _dot(w1_ref[...], u_ref[...].astype(BF16))
    a_re = a[:n1]
    a_im = a[n1:]
    tw_c = tw_ref[0, :, 0:1]
    tw_s = tw_ref[0, :, 1:2]
    a_ref[...] = _pack_bf16_pair(a_re * tw_c + a_im * tw_s, a_im * tw_c - a_re * tw_s)


def _fnet_stage2_kernel(a_hbm, w2_ref, z_hbm, abuf, zbuf, in_sem, out_sem, *, n1):
    n2 = FFT_N2
    i = pl.program_id(0)
    n_steps = pl.num_programs(0)
    slot = i % 2

    def gather(step, to_slot):
        rows = pl.ds((step // n1) * n2, n2)
        return pltpu.make_async_copy(a_hbm.at[rows, step % n1, :], abuf.at[to_slot], in_sem.at[to_slot])

    def scatter(step, from_slot):
        rows = pl.ds((step // n1) * n2, n2)
        return pltpu.make_async_copy(zbuf.at[from_slot], z_hbm.at[rows, step % n1, :], out_sem.at[from_slot])

    @pl.when(i == 0)
    def _():
        gather(0, 0).start()

    @pl.when(i + 1 < n_steps)
    def _():
        gather(i + 1, 1 - slot).start()

    gather(i, slot).wait()
    a_re, a_im = _unpack_bf16_pair(abuf[slot])
    z = _dot(w2_ref[...], jnp.concatenate([a_re, a_im], axis=0))

    @pl.when(i >= 2)
    def _():
        scatter(i - 2, slot).wait()

    zbuf[slot] = z
    scatter(i, slot).start()

    @pl.when(i == n_steps - 1)
    def _():
        scatter(i - 1, 1 - slot).wait()
        scatter(i, slot).wait()


def _dft_tables(n1, n2, gdim):
    def cs(rows, cols, period):
        ang = 2.0 * np.pi * ((np.arange(rows)[:, None] * np.arange(cols)[None, :]) % period) / period
        return np.cos(ang), np.sin(ang)
    cc, sc = cs(gdim, gdim, gdim)
    w_ch = np.concatenate([cc, -sc], axis=1) / np.sqrt(gdim)
    c1, s1 = cs(n1, n1, n1)
    w1 = np.block([[c1, s1], [-s1, c1]]) / np.sqrt(n1)
    c2, s2 = cs(n2, n2, n2)
    w2 = np.concatenate([c2, s2], axis=1) / np.sqrt(n2)
    twc, tws = cs(n2, n1, n1 * n2)
    tw = np.zeros((n2, n1, LANES), np.float64)
    tw[:, :, 0] = twc
    tw[:, :, 1] = tws
    return [jnp.asarray(t, F32) for t in (w_ch, w1, w2, tw)]


def _fnet_mix(x, g, seq_len):
    T, D = x.shape
    n_seq = T // seq_len
    n2 = FFT_N2
    n1 = seq_len // n2
    gdim = D // FNET_GROUPS
    w_ch, w1, w2, tw = _dft_tables(n1, n2, gdim)
    w_ch, w1, w2 = (t.astype(BF16) for t in (w_ch, w1, w2))
    const2 = lambda shape: pl.BlockSpec(shape, lambda i: (0, 0))
    any_spec = pl.BlockSpec(memory_space=pl.ANY)
    slab = pl.BlockSpec((None, n1, D), lambda i: (i, 0, 0))
    a = pl.pallas_call(
        functools.partial(_fnet_stage1_kernel, n1=n1, gdim=gdim),
        grid=(n_seq * n2,),
        in_specs=[any_spec, const2((1, D)), const2(w_ch.shape), const2(w1.shape),
                  pl.BlockSpec((1, n1, LANES), lambda i: (i % n2, 0, 0))],
        out_specs=slab,
        out_shape=jax.ShapeDtypeStruct((n_seq * n2, n1, D), jnp.uint32),
        scratch_shapes=[pltpu.VMEM((2, n1, D), F32), pltpu.SemaphoreType.DMA((2,)),
                        pltpu.VMEM((2 * n1, D), F32)],
        compiler_params=_params(("arbitrary",), 24 * n1 * D * 4 + 16 * 1024 * 1024),
    )(x.reshape(T // n2, n2, D), g, w_ch, w1, tw)
    z = pl.pallas_call(
        functools.partial(_fnet_stage2_kernel, n1=n1),
        grid=(n_seq * n1,),
        in_specs=[any_spec, const2(w2.shape)],
        out_specs=any_spec,
        out_shape=jax.ShapeDtypeStruct((n_seq * n2, n1, D), F32),
        scratch_shapes=[pltpu.VMEM((2, n2, D), jnp.uint32), pltpu.VMEM((2, n2, D), F32),
                        pltpu.SemaphoreType.DMA((2,)), pltpu.SemaphoreType.DMA((2,))],
        compiler_params=_params(("arbitrary",), 24 * n2 * D * 4 + 16 * 1024 * 1024),
    )(a, w2)
    return z.reshape(T, D)


def _oproj_kernel(x_ref, a_ref, w_ref, o_ref):
    o_ref[...] = x_ref[...] + _dot(a_ref[...].astype(BF16), w_ref[...])


def _oproj_residual(x, a, w, tm):
    T, D = x.shape
    row = lambda i: (i, 0)
    return pl.pallas_call(
        _oproj_kernel,
        grid=(T // tm,),
        in_specs=[pl.BlockSpec((tm, D), row), pl.BlockSpec((tm, a.shape[1]), row), _const_spec(w.shape)],
        out_specs=pl.BlockSpec((tm, D), row),
        out_shape=jax.ShapeDtypeStruct((T, D), F32),
        compiler_params=_params(("parallel",), 2 * (2 * tm * D * 4 + tm * a.shape[1] * a.dtype.itemsize
                                                    + w.size * 2) + 2 * tm * D * 4),
    )(x, a, w)


def _mlp_kernel(x_ref, g_ref, wup_ref, wdn_ref, o_ref, xn_ref):
    @pl.when(pl.program_id(1) == 0)
    def _():
        x = x_ref[...]
        xn_ref[...] = _rms(x, g_ref[...]).astype(BF16)
        o_ref[...] = x

    hcol = jnp.square(jnp.maximum(_dot(xn_ref[...], wup_ref[...]), 0.0)).astype(BF16)
    o_ref[...] += _dot(hcol, wdn_ref[...])


def _mlp(x, g, w_up, w_down, tm, tf):
    T, D = x.shape
    dff = w_up.shape[1]
    return pl.pallas_call(
        _mlp_kernel,
        grid=(T // tm, dff // tf),
        in_specs=[pl.BlockSpec((tm, D), lambda i, f: (i, 0)), pl.BlockSpec((1, D), lambda i, f: (0, 0)),
                  pl.BlockSpec((D, tf), lambda i, f: (0, f)), pl.BlockSpec((tf, D), lambda i, f: (f, 0))],
        out_specs=pl.BlockSpec((tm, D), lambda i, f: (i, 0)),
        out_shape=jax.ShapeDtypeStruct((T, D), F32),
        scratch_shapes=[pltpu.VMEM((tm, D), BF16)],
        compiler_params=_params(("parallel", "arbitrary"),
                                2 * (2 * tm * D * 4 + 2 * D * tf * 2) + tm * D * 2 + 3 * tm * tf * 4),
    )(x, g, w_up, w_down)


def _dup_rope_cols(w):
    return jnp.concatenate([w, w], axis=-1)


def _pad_head_vec(g):
    return jnp.concatenate([g[:QK_NOPE_DIM], _dup_rope_cols(g[QK_NOPE_DIM:])])[None, :].astype(F32)


def _rope_tables(n_seq, seq_len):
    inv_freq = ROPE_THETA ** (-jnp.arange(0, QK_ROPE_DIM, 2, dtype=F32) / QK_ROPE_DIM)
    ang = jnp.arange(seq_len, dtype=F32)[:, None] * inv_freq[None, :]
    c, s = jnp.cos(ang), jnp.sin(ang)
    z = jnp.zeros_like(c)
    tile = lambda t: jnp.tile(t, (n_seq, 1))
    return (tile(c).T, tile(s).T,
            tile(jnp.concatenate([c, c, z, z], axis=1)), tile(jnp.concatenate([-s, s, z, z], axis=1)))


def _pick_tile(n, target):
    t = min(n, target)
    while n % t:
        t //= 2
    return t


def _trunk(x, seq_len, p):
    T = x.shape[0]
    rope = _rope_tables(T // seq_len, seq_len)
    tm = _pick_tile(T, 512)
    tf = _pick_tile(p["w_up"][0].shape[1], 1024)
    ts = _pick_tile(seq_len, 512)
    for i in range(len(p["w_up"])):
        j = i // 2
        if i % 2 == 0:
            qt, k, vt = _mla_project(x, p["attn_norm"][j], p["w_q_a"][j], p["q_a_norm"][j], p["w_q_b"][j],
                                     p["q_norm"][j], p["w_kv_a"][j], p["kv_a_norm"][j], p["w_kv_b"][j],
                                     p["k_norm"][j], rope, tm)
            a = _attention(qt, k, vt, seq_len, ts, ts)
        else:
            a = _fnet_mix(x, p["fnet_norm"][j], seq_len)
        x = _oproj_residual(x, a, p["w_mix_o"][i], tm)
        x = _mlp(x, p["mlp_norm"][i], p["w_up"][i], p["w_down"][i], tm, tf)
    return x


def kernel(x_prompt, x_sample, attn_norm, w_q_a, q_a_norm, w_q_b, w_kv_a, kv_a_norm, w_kv_b, q_norm, k_norm,
           w_attn_o, fnet_norm, w_fnet_o, mlp_norm, w_up, w_down):
    depth = mlp_norm.shape[0]
    n_attn = attn_norm.shape[0]
    kv_rank = kv_a_norm.shape[1]
    q_rank = q_a_norm.shape[1]
    row2 = lambda v: v[None, :].astype(F32)
    wqbt = jnp.swapaxes(w_q_b.reshape(n_attn, q_rank, N_HEADS, QK_HEAD_DIM), 1, 3)
    wqbt = jnp.pad(jnp.swapaxes(wqbt, 1, 2), ((0, 0), (0, 0), (0, HEAD_PAD - QK_HEAD_DIM), (0, 0)))
    wqbt = wqbt.reshape(n_attn, N_HEADS * HEAD_PAD, q_rank)
    q_gain = lambda g: jnp.pad(g, (0, HEAD_PAD - QK_HEAD_DIM))[:, None].astype(F32)
    wkva = jnp.concatenate([w_kv_a[..., :kv_rank], _dup_rope_cols(w_kv_a[..., kv_rank:])], axis=-1)
    layers = lambda w: [w[j].astype(BF16) for j in range(w.shape[0])]
    p = {
        "attn_norm": [row2(g) for g in attn_norm], "w_q_a": layers(w_q_a),
        "q_a_norm": [row2(g) for g in q_a_norm], "w_q_b": layers(wqbt),
        "q_norm": [q_gain(g) for g in q_norm], "w_kv_a": layers(wkva),
        "kv_a_norm": [row2(g) for g in kv_a_norm], "w_kv_b": layers(w_kv_b),
        "k_norm": [_pad_head_vec(g) for g in k_norm], "fnet_norm": [row2(g) for g in fnet_norm],
        "w_mix_o": [(w_attn_o if i % 2 == 0 else w_fnet_o)[i // 2].astype(BF16) for i in range(depth)],
        "mlp_norm": [row2(g) for g in mlp_norm], "w_up": layers(w_up), "w_down": layers(w_down),
    }
    outs = []
    for xg in (x_prompt, x_sample):
        b, s, d = xg.shape
        outs.append(_trunk(xg.reshape(b * s, d), s, p).reshape(b, s, d))
    return tuple(outs)
```

```python
import functools
import math

import numpy as np
import jax
import jax.numpy as jnp
from jax import lax
from jax.experimental import pallas as pl
from jax.experimental.pallas import tpu as pltpu

F32 = jnp.float32
BF16 = jnp.bfloat16

N_HEADS = 16
QK_NOPE_DIM = 128
QK_ROPE_DIM = 64
QK_HEAD_DIM = QK_NOPE_DIM + QK_ROPE_DIM
V_HEAD_DIM = 128
HEAD_PAD = 256
BF16_ROWS = 16
V_ROWS = V_HEAD_DIM + BF16_ROWS
ROPE_THETA = 10000.0
FNET_GROUPS = 8
EPS = 1e-6
FFT_N2 = 128
LANES = 128
SCORE_PAD_ROWS = 8
VMEM_CAP = 60 * 1024 * 1024


def _vmem_limit(nbytes):
    return int(min(max(nbytes, 16 * 1024 * 1024), VMEM_CAP))


def _params(sem, nbytes):
    return pltpu.CompilerParams(dimension_semantics=sem, vmem_limit_bytes=_vmem_limit(nbytes))


def _rms(xf, g):
    return xf * lax.rsqrt(jnp.mean(xf * xf, axis=-1, keepdims=True) + EPS) * g


def _dot(a, b):
    return jnp.dot(a, b, preferred_element_type=F32)


def _rope_dup(pe, cos_t, sin_t):
    return pe * cos_t + pltpu.roll(pe, 32, axis=1) * sin_t


def _qproj_kernel(x_ref, ga_ref, wqa_ref, gqa_ref, wqbt_ref, gq_ref, cos_ref, sin_ref, q_ref, *, qscale):
    h = _rms(x_ref[...], ga_ref[...]).astype(BF16)
    cq_t = _rms(_dot(h, wqa_ref[...]), gqa_ref[...]).T.astype(BF16)
    cos_t = cos_ref[...]
    sin_t = sin_ref[...]
    half = QK_ROPE_DIM // 2
    for hh in range(N_HEADS):
        qh = _dot(wqbt_ref[hh * HEAD_PAD:(hh + 1) * HEAD_PAD, :], cq_t)
        ss = jnp.sum(qh * qh, axis=0, keepdims=True)
        qh = qh * (lax.rsqrt(ss * (1.0 / QK_HEAD_DIM) + EPS) * qscale) * gq_ref[...]
        x1 = qh[QK_NOPE_DIM:QK_NOPE_DIM + half]
        x2 = qh[QK_NOPE_DIM + half:QK_HEAD_DIM]
        q_ref[hh, :QK_NOPE_DIM, :] = qh[:QK_NOPE_DIM].astype(BF16)
        q_ref[hh, QK_NOPE_DIM:QK_NOPE_DIM + half, :] = (x1 * cos_t - x2 * sin_t).astype(BF16)
        q_ref[hh, QK_NOPE_DIM + half:QK_HEAD_DIM, :] = (x2 * cos_t + x1 * sin_t).astype(BF16)
        q_ref[hh, QK_HEAD_DIM:, :] = jnp.zeros((HEAD_PAD - QK_HEAD_DIM, qh.shape[1]), BF16)


def _kvproj_kernel(x_ref, ga_ref, wkva_ref, gkva_ref, wkvb_ref, gk_ref, cos_ref, sin_ref, k_ref, v_ref,
                   *, kv_rank):
    h = _rms(x_ref[...], ga_ref[...]).astype(BF16)
    ckv = _dot(h, wkva_ref[...])
    c = _rms(ckv[:, :kv_rank], gkva_ref[...]).astype(BF16)
    pe = ckv[:, kv_rank:]
    pe_ss = 0.5 * jnp.sum(pe * pe, axis=-1, keepdims=True)
    g_nope = gk_ref[:, :QK_NOPE_DIM]
    r = _rope_dup(pe * gk_ref[:, QK_NOPE_DIM:], cos_ref[...], sin_ref[...])
    for hh in range(N_HEADS):
        kvh = _dot(c, wkvb_ref[:, hh * 256:(hh + 1) * 256])
        k_nope = kvh[:, :QK_NOPE_DIM]
        ss = jnp.sum(k_nope * k_nope, axis=-1, keepdims=True) + pe_ss
        inv = lax.rsqrt(ss * (1.0 / QK_HEAD_DIM) + EPS)
        k_ref[hh, :, :QK_NOPE_DIM] = (k_nope * inv * g_nope).astype(BF16)
        k_ref[hh, :, QK_NOPE_DIM:] = (r * inv).astype(BF16)
        v_ref[hh, :V_HEAD_DIM, :] = kvh[:, QK_NOPE_DIM:].T.astype(BF16)
        v_ref[hh, V_HEAD_DIM:, :] = (lax.broadcasted_iota(jnp.int32, (BF16_ROWS, kvh.shape[0]), 0) == 0
                                     ).astype(BF16)


def _const_spec(shape):
    return pl.BlockSpec(shape, lambda i: (0,) * len(shape))


def _mla_project(x, ga, wqa, gqa, wqbt, gq, wkva, gkva, wkvb, gk, rope, tm):
    T, D = x.shape
    q_rank = wqa.shape[1]
    kv_rank = gkva.shape[1]
    qscale = QK_HEAD_DIM ** -0.5 * math.log2(math.e)
    row = lambda i: (i, 0)
    head_rows = lambda i: (0, i, 0)
    head_cols = lambda i: (0, 0, i)
    tab = pl.BlockSpec((tm, LANES), row)
    tab_t = pl.BlockSpec((QK_ROPE_DIM // 2, tm), lambda i: (0, i))
    cos_t, sin_t, cos_dup, sin_dup = rope
    qt = pl.pallas_call(
        functools.partial(_qproj_kernel, qscale=qscale),
        grid=(T // tm,),
        in_specs=[pl.BlockSpec((tm, D), row), _const_spec((1, D)), _const_spec(wqa.shape),
                  _const_spec((1, q_rank)), _const_spec(wqbt.shape), _const_spec((HEAD_PAD, 1)), tab_t, tab_t],
        out_specs=pl.BlockSpec((N_HEADS, HEAD_PAD, tm), head_cols),
        out_shape=jax.ShapeDtypeStruct((N_HEADS, HEAD_PAD, T), BF16),
        compiler_params=_params(("parallel",), 2 * (tm * D * 4 + wqa.size * 2 + wqbt.size * 2
                                                    + N_HEADS * tm * HEAD_PAD * 2) + 8 * tm * D * 4),
    )(x, ga, wqa, gqa, wqbt, gq, cos_t, sin_t)
    k, vt = pl.pallas_call(
        functools.partial(_kvproj_kernel, kv_rank=kv_rank),
        grid=(T // tm,),
        in_specs=[pl.BlockSpec((tm, D), row), _const_spec((1, D)), _const_spec(wkva.shape),
                  _const_spec((1, kv_rank)), _const_spec(wkvb.shape), _const_spec((1, HEAD_PAD)), tab, tab],
        out_specs=[pl.BlockSpec((N_HEADS, tm, HEAD_PAD), head_rows),
                   pl.BlockSpec((N_HEADS, V_ROWS, tm), head_cols)],
        out_shape=[jax.ShapeDtypeStruct((N_HEADS, T, HEAD_PAD), BF16),
                   jax.ShapeDtypeStruct((N_HEADS, V_ROWS, T), BF16)],
        compiler_params=_params(("parallel",), 2 * (tm * D * 4 + wkva.size * 2 + wkvb.size * 2
                                                    + N_HEADS * tm * (HEAD_PAD + V_ROWS) * 2)
                                + 8 * tm * D * 4),
    )(x, ga, wkva, gkva, wkvb, gk, cos_dup, sin_dup)
    return qt, k, vt


def _flash_kernel(qt_ref, k_ref, vt_ref, o_ref, s_ref, *, tk, chunks_per_step):
    qt = qt_ref[0]
    tq = qt.shape[1]
    n_chunks = k_ref.shape[1] // tk

    def scores(j, slot):
        s = _dot(k_ref[0, pl.ds(pl.multiple_of(j * tk, tk), tk), :], qt)
        s_ref[slot, :tk] = s
        return jnp.max(s, axis=0, keepdims=True)

    def absorb(j, slot, s_max, m, acc):
        vt = vt_ref[0, :, pl.ds(pl.multiple_of(j * tk, tk), tk)]
        m_new = jnp.maximum(m, s_max)
        p = jnp.exp2(s_ref[slot, :tk] - m_new)
        acc = jnp.exp2(m - m_new) * acc + _dot(vt, p.astype(BF16))
        return m_new, acc

    def body(step, carry):
        s_max, m, acc = carry
        for c in range(chunks_per_step):
            j = step * chunks_per_step + c
            s_max_next = scores(jnp.minimum(j + 1, n_chunks - 1), (c + 1) % 2)
            m, acc = absorb(j, c % 2, s_max, m, acc)
            s_max = s_max_next
        return s_max, m, acc

    m0 = jnp.full((1, tq), -jnp.inf, F32)
    acc0 = jnp.zeros((V_ROWS, tq), F32)
    _, _, acc = lax.fori_loop(0, n_chunks // chunks_per_step, body, (scores(0, 0), m0, acc0))
    o_ref[...] = (acc[:V_HEAD_DIM] / acc[V_HEAD_DIM:V_HEAD_DIM + 1]).T.astype(BF16)


def _attention(qt, k, vt, seq_len, tq, tk):
    n_seq = k.shape[1] // seq_len
    nq = seq_len // tq
    n_chunks = seq_len // tk
    chunks_per_step = math.gcd(n_chunks, 16)
    if chunks_per_step == n_chunks and n_chunks % 4 == 0:
        chunks_per_step //= 2
    assert chunks_per_step % 2 == 0
    return pl.pallas_call(
        functools.partial(_flash_kernel, tk=tk, chunks_per_step=chunks_per_step),
        grid=(n_seq, N_HEADS, nq),
        in_specs=[pl.BlockSpec((1, HEAD_PAD, tq), lambda b, h, i: (h, 0, b * nq + i)),
                  pl.BlockSpec((1, seq_len, HEAD_PAD), lambda b, h, i: (h, b, 0)),
                  pl.BlockSpec((1, V_ROWS, seq_len), lambda b, h, i: (h, 0, b))],
        out_specs=pl.BlockSpec((tq, V_HEAD_DIM), lambda b, h, i: (b * nq + i, h)),
        out_shape=jax.ShapeDtypeStruct((n_seq * seq_len, N_HEADS * V_HEAD_DIM), BF16),
        scratch_shapes=[pltpu.VMEM((2, tk + SCORE_PAD_ROWS, tq), F32)],
        compiler_params=_params(("parallel", "parallel", "arbitrary"),
                                2 * seq_len * (HEAD_PAD + V_ROWS) * 2 + 8 * tq * tk * 4
                                + 4 * tq * HEAD_PAD * 2),
    )(qt, k, vt)


def _pack_bf16_pair(hi, lo):
    bits = lambda v: lax.bitcast_convert_type(v.astype(BF16).astype(F32), jnp.uint32)
    return bits(hi) | (bits(lo) >> 16)


def _unpack_bf16_pair(w):
    hi = lax.bitcast_convert_type(w & jnp.uint32(0xFFFF0000), F32)
    lo = lax.bitcast_convert_type(w << 16, F32)
    return hi.astype(BF16), lo.astype(BF16)


def _fnet_stage1_kernel(x_hbm, g_ref, wc_ref, w1_ref, tw_ref, a_ref, xbuf, sem, u_ref, *, n1, gdim):
    i = pl.program_id(0)
    slot = i % 2

    def gather(step, to_slot):
        rows = pl.ds((step // FFT_N2) * n1, n1)
        return pltpu.make_async_copy(x_hbm.at[rows, step % FFT_N2, :], xbuf.at[to_slot], sem.at[to_slot])

    @pl.when(i == 0)
    def _():
        gather(0, 0).start()

    @pl.when(i + 1 < pl.num_programs(0))
    def _():
        gather(i + 1, 1 - slot).start()

    gather(i, slot).wait()
    h = _rms(xbuf[slot], g_ref[...]).astype(BF16)
    wc = wc_ref[...]
    for gi in range(FNET_GROUPS):
        cols = slice(gi * gdim, (gi + 1) * gdim)
        u = _dot(h[:, cols], wc)
        u_ref[:n1, cols] = u[:, :gdim]
        u_ref[n1:, cols] = u[:, gdim:]
    a = _dot(w1_ref[...], u_ref[...].astype(BF16))
    a_re = a[:n1]
    a_im = a[n1:]
    tw_c = tw_ref[0, :, 0:1]
    tw_s = tw_ref[0, :, 1:2]
    a_ref[...] = _pack_bf16_pair(a_re * tw_c + a_im * tw_s, a_im * tw_c - a_re * tw_s)


def _fnet_stage2_kernel(a_hbm, w2_ref, z_hbm, abuf, zbuf, in_sem, out_sem, *, n1):
    n2 = FFT_N2
    i = pl.program_id(0)
    n_steps = pl.num_programs(0)
    slot = i % 2

    def gather(step, to_slot):
        rows = pl.ds((step // n1) * n2, n2)
        return pltpu.make_async_copy(a_hbm.at[rows, step % n1, :], abuf.at[to_slot], in_sem.at[to_slot])

    def scatter(step, from_slot):
        rows = pl.ds((step // n1) * n2, n2)
        return pltpu.make_async_copy(zbuf.at[from_slot], z_hbm.at[rows, step % n1, :], out_sem.at[from_slot])

    @pl.when(i == 0)
    def _():
        gather(0, 0).start()

    @pl.when(i + 1 < n_steps)
    def _():
        gather(i + 1, 1 - slot).start()

    gather(i, slot).wait()
    a_re, a_im = _unpack_bf16_pair(abuf[slot])
    z = _dot(w2_ref[...], jnp.concatenate([a_re, a_im], axis=0))

    @pl.when(i >= 2)
    def _():
        scatter(i - 2, slot).wait()

    zbuf[slot] = z
    scatter(i, slot).start()

    @pl.when(i == n_steps - 1)
    def _():
        scatter(i - 1, 1 - slot).wait()
        scatter(i, slot).wait()


def _dft_tables(n1, n2, gdim):
    def cs(rows, cols, period):
        ang = 2.0 * np.pi * ((np.arange(rows)[:, None] * np.arange(cols)[None, :]) % period) / period
        return np.cos(ang), np.sin(ang)
    cc, sc = cs(gdim, gdim, gdim)
    w_ch = np.concatenate([cc, -sc], axis=1) / np.sqrt(gdim)
    c1, s1 = cs(n1, n1, n1)
    w1 = np.block([[c1, s1], [-s1, c1]]) / np.sqrt(n1)
    c2, s2 = cs(n2, n2, n2)
    w2 = np.concatenate([c2, s2], axis=1) / np.sqrt(n2)
    twc, tws = cs(n2, n1, n1 * n2)
    tw = np.zeros((n2, n1, LANES), np.float64)
    tw[:, :, 0] = twc
    tw[:, :, 1] = tws
    return [jnp.asarray(t, F32) for t in (w_ch, w1, w2, tw)]


def _fnet_mix(x, g, seq_len):
    T, D = x.shape
    n_seq = T // seq_len
    n2 = FFT_N2
    n1 = seq_len // n2
    gdim = D // FNET_GROUPS
    w_ch, w1, w2, tw = _dft_tables(n1, n2, gdim)
    w_ch, w1, w2 = (t.astype(BF16) for t in (w_ch, w1, w2))
    const2 = lambda shape: pl.BlockSpec(shape, lambda i: (0, 0))
    any_spec = pl.BlockSpec(memory_space=pl.ANY)
    slab = pl.BlockSpec((None, n1, D), lambda i: (i, 0, 0))
    a = pl.pallas_call(
        functools.partial(_fnet_stage1_kernel, n1=n1, gdim=gdim),
        grid=(n_seq * n2,),
        in_specs=[any_spec, const2((1, D)), const2(w_ch.shape), const2(w1.shape),
                  pl.BlockSpec((1, n1, LANES), lambda i: (i % n2, 0, 0))],
        out_specs=slab,
        out_shape=jax.ShapeDtypeStruct((n_seq * n2, n1, D), jnp.uint32),
        scratch_shapes=[pltpu.VMEM((2, n1, D), F32), pltpu.SemaphoreType.DMA((2,)),
                        pltpu.VMEM((2 * n1, D), F32)],
        compiler_params=_params(("arbitrary",), 24 * n1 * D * 4 + 16 * 1024 * 1024),
    )(x.reshape(T // n2, n2, D), g, w_ch, w1, tw)
    z = pl.pallas_call(
        functools.partial(_fnet_stage2_kernel, n1=n1),
        grid=(n_seq * n1,),
        in_specs=[any_spec, const2(w2.shape)],
        out_specs=any_spec,
        out_shape=jax.ShapeDtypeStruct((n_seq * n2, n1, D), F32),
        scratch_shapes=[pltpu.VMEM((2, n2, D), jnp.uint32), pltpu.VMEM((2, n2, D), F32),
                        pltpu.SemaphoreType.DMA((2,)), pltpu.SemaphoreType.DMA((2,))],
        compiler_params=_params(("arbitrary",), 24 * n2 * D * 4 + 16 * 1024 * 1024),
    )(a, w2)
    return z.reshape(T, D)


def _oproj_kernel(x_ref, a_ref, w_ref, g_ref, o_ref, xn_ref):
    x = x_ref[...] + _dot(a_ref[...].astype(BF16), w_ref[...])
    o_ref[...] = x
    xn_ref[...] = _rms(x, g_ref[...]).astype(BF16)


def _oproj_residual(x, a, w, g, tm):
    T, D = x.shape
    row = lambda i: (i, 0)
    return pl.pallas_call(
        _oproj_kernel,
        grid=(T // tm,),
        in_specs=[pl.BlockSpec((tm, D), row), pl.BlockSpec((tm, a.shape[1]), row), _const_spec(w.shape),
                  _const_spec((1, D))],
        out_specs=[pl.BlockSpec((tm, D), row), pl.BlockSpec((tm, D), row)],
        out_shape=[jax.ShapeDtypeStruct((T, D), F32), jax.ShapeDtypeStruct((T, D), BF16)],
        compiler_params=_params(("parallel",), 2 * (2 * tm * D * 4 + tm * a.shape[1] * a.dtype.itemsize
                                                    + w.size * 2 + tm * D * 2) + 3 * tm * D * 4),
    )(x, a, w, g)


def _mlp_kernel(x_ref, xn_ref, wup_ref, wdn_ref, o_ref):
    @pl.when(pl.program_id(1) == 0)
    def _():
        o_ref[...] = x_ref[...]

    hcol = jnp.square(jnp.maximum(_dot(xn_ref[...], wup_ref[...]), 0.0)).astype(BF16)
    o_ref[...] += _dot(hcol, wdn_ref[...])


def _mlp(x, xn, w_up, w_down, layer, tm, tf):
    T, D = x.shape
    dff = w_up.shape[2]
    return pl.pallas_call(
        _mlp_kernel,
        grid=(T // tm, dff // tf),
        in_specs=[pl.BlockSpec((tm, D), lambda i, f: (i, 0)), pl.BlockSpec((tm, D), lambda i, f: (i, 0)),
                  pl.BlockSpec((None, D, tf), lambda i, f: (layer, 0, f)),
                  pl.BlockSpec((None, tf, D), lambda i, f: (layer, f, 0))],
        out_specs=pl.BlockSpec((tm, D), lambda i, f: (i, 0)),
        out_shape=jax.ShapeDtypeStruct((T, D), F32),
        compiler_params=_params(("parallel", "arbitrary"),
                                2 * (2 * tm * D * 4 + tm * D * 2 + 2 * D * tf * 2) + 3 * tm * tf * 4),
    )(x, xn, w_up, w_down)


def _dup_rope_cols(w):
    return jnp.concatenate([w, w], axis=-1)


def _pad_head_vec(g):
    return jnp.concatenate([g[:QK_NOPE_DIM], _dup_rope_cols(g[QK_NOPE_DIM:])])[None, :].astype(F32)


def _rope_tables(n_seq, seq_len):
    inv_freq = ROPE_THETA ** (-jnp.arange(0, QK_ROPE_DIM, 2, dtype=F32) / QK_ROPE_DIM)
    ang = jnp.arange(seq_len, dtype=F32)[:, None] * inv_freq[None, :]
    c, s = jnp.cos(ang), jnp.sin(ang)
    z = jnp.zeros_like(c)
    tile = lambda t: jnp.tile(t, (n_seq, 1))
    return (tile(c).T, tile(s).T,
            tile(jnp.concatenate([c, c, z, z], axis=1)), tile(jnp.concatenate([-s, s, z, z], axis=1)))


def _pick_tile(n, target):
    t = min(n, target)
    while n % t:
        t //= 2
    return t


def _trunk(x, seq_len, p):
    T = x.shape[0]
    rope = _rope_tables(T // seq_len, seq_len)
    tm = _pick_tile(T, 512)
    tf = _pick_tile(p["w_up"].shape[2], 1024)
    ts = _pick_tile(seq_len, 512)
    for i in range(p["w_up"].shape[0]):
        j = i // 2
        if i % 2 == 0:
            qt, k, vt = _mla_project(x, p["attn_norm"][j], p["w_q_a"][j], p["q_a_norm"][j], p["w_q_b"][j],
                                     p["q_norm"][j], p["w_kv_a"][j], p["kv_a_norm"][j], p["w_kv_b"][j],
                                     p["k_norm"][j], rope, tm)
            a = _attention(qt, k, vt, seq_len, ts, ts)
        else:
            a = _fnet_mix(x, p["fnet_norm"][j], seq_len)
        x, xn = _oproj_residual(x, a, p["w_mix_o"][i], p["mlp_norm"][i], tm)
        x = _mlp(x, xn, p["w_up"], p["w_down"], i, tm, tf)
    return x


def kernel(x_prompt, x_sample, attn_norm, w_q_a, q_a_norm, w_q_b, w_kv_a, kv_a_norm, w_kv_b, q_norm, k_norm,
           w_attn_o, fnet_norm, w_fnet_o, mlp_norm, w_up, w_down):
    depth = mlp_norm.shape[0]
    n_attn = attn_norm.shape[0]
    kv_rank = kv_a_norm.shape[1]
    q_rank = q_a_norm.shape[1]
    row2 = lambda v: v[None, :].astype(F32)
    wqbt = jnp.swapaxes(w_q_b.reshape(n_attn, q_rank, N_HEADS, QK_HEAD_DIM), 1, 3)
    wqbt = jnp.pad(jnp.swapaxes(wqbt, 1, 2), ((0, 0), (0, 0), (0, HEAD_PAD - QK_HEAD_DIM), (0, 0)))
    wqbt = wqbt.reshape(n_attn, N_HEADS * HEAD_PAD, q_rank)
    q_gain = lambda g: jnp.pad(g, (0, HEAD_PAD - QK_HEAD_DIM))[:, None].astype(F32)
    wkva = jnp.concatenate([w_kv_a[..., :kv_rank], _dup_rope_cols(w_kv_a[..., kv_rank:])], axis=-1)
    layers = lambda w: [w[j].astype(BF16) for j in range(w.shape[0])]
    p = {
        "attn_norm": [row2(g) for g in attn_norm], "w_q_a": layers(w_q_a),
        "q_a_norm": [row2(g) for g in q_a_norm], "w_q_b": layers(wqbt),
        "q_norm": [q_gain(g) for g in q_norm], "w_kv_a": layers(wkva),
        "kv_a_norm": [row2(g) for g in kv_a_norm], "w_kv_b": layers(w_kv_b),
        "k_norm": [_pad_head_vec(g) for g in k_norm], "fnet_norm": [row2(g) for g in fnet_norm],
        "w_mix_o": [(w_attn_o if i % 2 == 0 else w_fnet_o)[i // 2].astype(BF16) for i in range(depth)],
        "mlp_norm": [row2(g) for g in mlp_norm], "w_up": w_up.astype(BF16), "w_down": w_down.astype(BF16),
    }
    outs = []
    for xg in (x_prompt, x_sample):
        b, s, d = xg.shape
        outs.append(_trunk(xg.reshape(b * s, d), s, p).reshape(b, s, d))
    return tuple(outs)
```

```python
import functools
import math

import numpy as np
import jax
import jax.numpy as jnp
from jax import lax
from jax.experimental import pallas as pl
from jax.experimental.pallas import tpu as pltpu

F32 = jnp.float32
BF16 = jnp.bfloat16

N_HEADS = 16
QK_NOPE_DIM = 128
QK_ROPE_DIM = 64
QK_HEAD_DIM = QK_NOPE_DIM + QK_ROPE_DIM
V_HEAD_DIM = 128
HEAD_PAD = 256
BF16_ROWS = 16
V_ROWS = V_HEAD_DIM + BF16_ROWS
ROPE_THETA = 10000.0
FNET_GROUPS = 8
EPS = 1e-6
FFT_N2 = 128
LANES = 128
SCORE_PAD_ROWS = 8
VMEM_CAP = 60 * 1024 * 1024


def _vmem_limit(nbytes):
    return int(min(max(nbytes, 16 * 1024 * 1024), VMEM_CAP))


def _params(sem, nbytes):
    return pltpu.CompilerParams(dimension_semantics=sem, vmem_limit_bytes=_vmem_limit(nbytes))


def _rms(xf, g):
    return xf * lax.rsqrt(jnp.mean(xf * xf, axis=-1, keepdims=True) + EPS) * g


def _dot(a, b):
    return jnp.dot(a, b, preferred_element_type=F32)


def _rope_dup(pe, cos_t, sin_t):
    return pe * cos_t + pltpu.roll(pe, 32, axis=1) * sin_t


def _qproj_kernel(x_ref, ga_ref, wqa_ref, gqa_ref, wqbt_ref, gq_ref, cos_ref, sin_ref, q_ref, *, qscale):
    h = _rms(x_ref[...], ga_ref[...]).astype(BF16)
    cq_t = _rms(_dot(h, wqa_ref[...]), gqa_ref[...]).T.astype(BF16)
    cos_t = cos_ref[...]
    sin_t = sin_ref[...]
    half = QK_ROPE_DIM // 2
    for hh in range(N_HEADS):
        qh = _dot(wqbt_ref[hh * HEAD_PAD:(hh + 1) * HEAD_PAD, :], cq_t)
        ss = jnp.sum(qh * qh, axis=0, keepdims=True)
        qh = qh * (lax.rsqrt(ss * (1.0 / QK_HEAD_DIM) + EPS) * qscale) * gq_ref[...]
        x1 = qh[QK_NOPE_DIM:QK_NOPE_DIM + half]
        x2 = qh[QK_NOPE_DIM + half:QK_HEAD_DIM]
        q_ref[hh, :QK_NOPE_DIM, :] = qh[:QK_NOPE_DIM].astype(BF16)
        q_ref[hh, QK_NOPE_DIM:QK_NOPE_DIM + half, :] = (x1 * cos_t - x2 * sin_t).astype(BF16)
        q_ref[hh, QK_NOPE_DIM + half:QK_HEAD_DIM, :] = (x2 * cos_t + x1 * sin_t).astype(BF16)
        q_ref[hh, QK_HEAD_DIM:, :] = jnp.zeros((HEAD_PAD - QK_HEAD_DIM, qh.shape[1]), BF16)


def _kvproj_kernel(x_ref, ga_ref, wkva_ref, gkva_ref, wk_ref, wvt_ref, gk_ref, cos_ref, sin_ref, k_ref, v_ref,
                   *, kv_rank):
    h = _rms(x_ref[...], ga_ref[...]).astype(BF16)
    ckv = _dot(h, wkva_ref[...])
    c = _rms(ckv[:, :kv_rank], gkva_ref[...])
    pe = ckv[:, kv_rank:]
    pe_ss = 0.5 * jnp.sum(pe * pe, axis=-1, keepdims=True)
    g_nope = gk_ref[:, :QK_NOPE_DIM]
    r = _rope_dup(pe * gk_ref[:, QK_NOPE_DIM:], cos_ref[...], sin_ref[...])
    vt = _dot(wvt_ref[...], c.T.astype(BF16))
    ones_row = (lax.broadcasted_iota(jnp.int32, (BF16_ROWS, vt.shape[1]), 0) == 0).astype(BF16)
    c = c.astype(BF16)
    for pair in range(N_HEADS // 2):
        k_pair = _dot(c, wk_ref[:, pair * 2 * QK_NOPE_DIM:(pair + 1) * 2 * QK_NOPE_DIM])
        for hh in (2 * pair, 2 * pair + 1):
            k_nope = k_pair[:, (hh % 2) * QK_NOPE_DIM:(hh % 2 + 1) * QK_NOPE_DIM]
            ss = jnp.sum(k_nope * k_nope, axis=-1, keepdims=True) + pe_ss
            inv = lax.rsqrt(ss * (1.0 / QK_HEAD_DIM) + EPS)
            k_ref[hh, :, :QK_NOPE_DIM] = (k_nope * inv * g_nope).astype(BF16)
            k_ref[hh, :, QK_NOPE_DIM:] = (r * inv).astype(BF16)
            v_ref[hh, :V_HEAD_DIM, :] = vt[hh * V_HEAD_DIM:(hh + 1) * V_HEAD_DIM].astype(BF16)
            v_ref[hh, V_HEAD_DIM:, :] = ones_row


def _const_spec(shape):
    return pl.BlockSpec(shape, lambda i: (0,) * len(shape))


def _mla_project(x, ga, wqa, gqa, wqbt, gq, wkva, gkva, wk, wvt, gk, rope, tm):
    T, D = x.shape
    q_rank = wqa.shape[1]
    kv_rank = gkva.shape[1]
    qscale = QK_HEAD_DIM ** -0.5 * math.log2(math.e)
    row = lambda i: (i, 0)
    head_rows = lambda i: (0, i, 0)
    head_cols = lambda i: (0, 0, i)
    tab = pl.BlockSpec((tm, LANES), row)
    tab_t = pl.BlockSpec((QK_ROPE_DIM // 2, tm), lambda i: (0, i))
    cos_t, sin_t, cos_dup, sin_dup = rope
    qt = pl.pallas_call(
        functools.partial(_qproj_kernel, qscale=qscale),
        grid=(T // tm,),
        in_specs=[pl.BlockSpec((tm, D), row), _const_spec((1, D)), _const_spec(wqa.shape),
                  _const_spec((1, q_rank)), _const_spec(wqbt.shape), _const_spec((HEAD_PAD, 1)), tab_t, tab_t],
        out_specs=pl.BlockSpec((N_HEADS, HEAD_PAD, tm), head_cols),
        out_shape=jax.ShapeDtypeStruct((N_HEADS, HEAD_PAD, T), BF16),
        compiler_params=_params(("parallel",), 2 * (tm * D * 4 + wqa.size * 2 + wqbt.size * 2
                                                    + N_HEADS * tm * HEAD_PAD * 2) + 8 * tm * D * 4),
    )(x, ga, wqa, gqa, wqbt, gq, cos_t, sin_t)
    k, vt = pl.pallas_call(
        functools.partial(_kvproj_kernel, kv_rank=kv_rank),
        grid=(T // tm,),
        in_specs=[pl.BlockSpec((tm, D), row), _const_spec((1, D)), _const_spec(wkva.shape),
                  _const_spec((1, kv_rank)), _const_spec(wk.shape), _const_spec(wvt.shape),
                  _const_spec((1, HEAD_PAD)), tab, tab],
        out_specs=[pl.BlockSpec((N_HEADS, tm, HEAD_PAD), head_rows),
                   pl.BlockSpec((N_HEADS, V_ROWS, tm), head_cols)],
        out_shape=[jax.ShapeDtypeStruct((N_HEADS, T, HEAD_PAD), BF16),
                   jax.ShapeDtypeStruct((N_HEADS, V_ROWS, T), BF16)],
        compiler_params=_params(("parallel",), 2 * (tm * D * 4 + wkva.size * 2 + wk.size * 2 + wvt.size * 2
                                                    + N_HEADS * tm * (HEAD_PAD + V_ROWS) * 2)
                                + 8 * tm * D * 4),
    )(x, ga, wkva, gkva, wk, wvt, gk, cos_dup, sin_dup)
    return qt, k, vt


def _flash_kernel(qt_ref, k_ref, vt_ref, o_ref, s_ref, *, tk, chunks_per_step):
    qt = qt_ref[0]
    tq = qt.shape[1]
    n_chunks = k_ref.shape[1] // tk

    def scores(j, slot):
        s = _dot(k_ref[0, pl.ds(pl.multiple_of(j * tk, tk), tk), :], qt)
        s_ref[slot, :tk] = s
        return jnp.max(s, axis=0, keepdims=True)

    def absorb(j, slot, s_max, m, acc):
        vt = vt_ref[0, :, pl.ds(pl.multiple_of(j * tk, tk), tk)]
        m_new = jnp.maximum(m, s_max)
        p = jnp.exp2(s_ref[slot, :tk] - m_new)
        acc = jnp.exp2(m - m_new) * acc + _dot(vt, p.astype(BF16))
        return m_new, acc

    def body(step, carry):
        s_max, m, acc = carry
        for c in range(chunks_per_step):
            j = step * chunks_per_step + c
            s_max_next = scores(jnp.minimum(j + 1, n_chunks - 1), (c + 1) % 2)
            m, acc = absorb(j, c % 2, s_max, m, acc)
            s_max = s_max_next
        return s_max, m, acc

    m0 = jnp.full((1, tq), -jnp.inf, F32)
    acc0 = jnp.zeros((V_ROWS, tq), F32)
    _, _, acc = lax.fori_loop(0, n_chunks // chunks_per_step, body, (scores(0, 0), m0, acc0))
    o_ref[...] = (acc[:V_HEAD_DIM] / acc[V_HEAD_DIM:V_HEAD_DIM + 1]).T.astype(BF16)


def _attention(qt, k, vt, seq_len, tq, tk):
    n_seq = k.shape[1] // seq_len
    nq = seq_len // tq
    n_chunks = seq_len // tk
    chunks_per_step = math.gcd(n_chunks, 16)
    if chunks_per_step == n_chunks and n_chunks % 4 == 0:
        chunks_per_step //= 2
    assert chunks_per_step % 2 == 0
    return pl.pallas_call(
        functools.partial(_flash_kernel, tk=tk, chunks_per_step=chunks_per_step),
        grid=(n_seq, N_HEADS, nq),
        in_specs=[pl.BlockSpec((1, HEAD_PAD, tq), lambda b, h, i: (h, 0, b * nq + i)),
                  pl.BlockSpec((1, seq_len, HEAD_PAD), lambda b, h, i: (h, b, 0)),
                  pl.BlockSpec((1, V_ROWS, seq_len), lambda b, h, i: (h, 0, b))],
        out_specs=pl.BlockSpec((tq, V_HEAD_DIM), lambda b, h, i: (b * nq + i, h)),
        out_shape=jax.ShapeDtypeStruct((n_seq * seq_len, N_HEADS * V_HEAD_DIM), BF16),
        scratch_shapes=[pltpu.VMEM((2, tk + SCORE_PAD_ROWS, tq), F32)],
        compiler_params=_params(("parallel", "parallel", "arbitrary"),
                                2 * seq_len * (HEAD_PAD + V_ROWS) * 2 + 8 * tq * tk * 4
                                + 4 * tq * HEAD_PAD * 2),
    )(qt, k, vt)


def _pack_bf16_pair(hi, lo):
    bits = lambda v: lax.bitcast_convert_type(v.astype(BF16).astype(F32), jnp.uint32)
    return bits(hi) | (bits(lo) >> 16)


def _unpack_bf16_pair(w):
    hi = lax.bitcast_convert_type(w & jnp.uint32(0xFFFF0000), F32)
    lo = lax.bitcast_convert_type(w << 16, F32)
    return hi.astype(BF16), lo.astype(BF16)


def _fnet_stage1_kernel(x_hbm, g_ref, wc_ref, w1_ref, tw_ref, a_ref, xbuf, sem, u_ref, *, n1, gdim):
    i = pl.program_id(0)
    slot = i % 2

    def gather(step, to_slot):
        rows = pl.ds((step // FFT_N2) * n1, n1)
        return pltpu.make_async_copy(x_hbm.at[rows, step % FFT_N2, :], xbuf.at[to_slot], sem.at[to_slot])

    @pl.when(i == 0)
    def _():
        gather(0, 0).start()

    @pl.when(i + 1 < pl.num_programs(0))
    def _():
        gather(i + 1, 1 - slot).start()

    gather(i, slot).wait()
    h = _rms(xbuf[slot], g_ref[...]).astype(BF16)
    wc = wc_ref[...]
    for gi in range(FNET_GROUPS):
        cols = slice(gi * gdim, (gi + 1) * gdim)
        u = _dot(h[:, cols], wc)
        u_ref[:n1, cols] = u[:, :gdim]
        u_ref[n1:, cols] = u[:, gdim:]
    a = _dot(w1_ref[...], u_ref[...].astype(BF16))
    a_re = a[:n1]
    a_im = a[n1:]
    tw_c = tw_ref[0, :, 0:1]
    tw_s = tw_ref[0, :, 1:2]
    a_ref[...] = _pack_bf16_pair(a_re * tw_c + a_im * tw_s, a_im * tw_c - a_re * tw_s)


def _fnet_stage2_kernel(a_hbm, w2_ref, z_hbm, abuf, zbuf, in_sem, out_sem, *, n1):
    n2 = FFT_N2
    i = pl.program_id(0)
    n_steps = pl.num_programs(0)
    slot = i % 2

    def gather(step, to_slot):
        rows = pl.ds((step // n1) * n2, n2)
        return pltpu.make_async_copy(a_hbm.at[rows, step % n1, :], abuf.at[to_slot], in_sem.at[to_slot])

    def scatter(step, from_slot):
        rows = pl.ds((step // n1) * n2, n2)
        return pltpu.make_async_copy(zbuf.at[from_slot], z_hbm.at[rows, step % n1, :], out_sem.at[from_slot])

    @pl.when(i == 0)
    def _():
        gather(0, 0).start()

    @pl.when(i + 1 < n_steps)
    def _():
        gather(i + 1, 1 - slot).start()

    gather(i, slot).wait()
    a_re, a_im = _unpack_bf16_pair(abuf[slot])
    z = _dot(w2_ref[...], jnp.concatenate([a_re, a_im], axis=0))

    @pl.when(i >= 2)
    def _():
        scatter(i - 2, slot).wait()

    zbuf[slot] = z
    scatter(i, slot).start()

    @pl.when(i == n_steps - 1)
    def _():
        scatter(i - 1, 1 - slot).wait()
        scatter(i, slot).wait()


def _dft_tables(n1, n2, gdim):
    def cs(rows, cols, period):
        ang = 2.0 * np.pi * ((np.arange(rows)[:, None] * np.arange(cols)[None, :]) % period) / period
        return np.cos(ang), np.sin(ang)
    cc, sc = cs(gdim, gdim, gdim)
    w_ch = np.concatenate([cc, -sc], axis=1) / np.sqrt(gdim)
    c1, s1 = cs(n1, n1, n1)
    w1 = np.block([[c1, s1], [-s1, c1]]) / np.sqrt(n1)
    c2, s2 = cs(n2, n2, n2)
    w2 = np.concatenate([c2, s2], axis=1) / np.sqrt(n2)
    twc, tws = cs(n2, n1, n1 * n2)
    tw = np.zeros((n2, n1, LANES), np.float64)
    tw[:, :, 0] = twc
    tw[:, :, 1] = tws
    return [jnp.asarray(t, F32) for t in (w_ch, w1, w2, tw)]


def _fnet_mix(x, g, seq_len):
    T, D = x.shape
    n_seq = T // seq_len
    n2 = FFT_N2
    n1 = seq_len // n2
    gdim = D // FNET_GROUPS
    w_ch, w1, w2, tw = _dft_tables(n1, n2, gdim)
    w_ch, w1, w2 = (t.astype(BF16) for t in (w_ch, w1, w2))
    const2 = lambda shape: pl.BlockSpec(shape, lambda i: (0, 0))
    any_spec = pl.BlockSpec(memory_space=pl.ANY)
    slab = pl.BlockSpec((None, n1, D), lambda i: (i, 0, 0))
    a = pl.pallas_call(
        functools.partial(_fnet_stage1_kernel, n1=n1, gdim=gdim),
        grid=(n_seq * n2,),
        in_specs=[any_spec, const2((1, D)), const2(w_ch.shape), const2(w1.shape),
                  pl.BlockSpec((1, n1, LANES), lambda i: (i % n2, 0, 0))],
        out_specs=slab,
        out_shape=jax.ShapeDtypeStruct((n_seq * n2, n1, D), jnp.uint32),
        scratch_shapes=[pltpu.VMEM((2, n1, D), F32), pltpu.SemaphoreType.DMA((2,)),
                        pltpu.VMEM((2 * n1, D), F32)],
        compiler_params=_params(("arbitrary",), 24 * n1 * D * 4 + 16 * 1024 * 1024),
    )(x.reshape(T // n2, n2, D), g, w_ch, w1, tw)
    z = pl.pallas_call(
        functools.partial(_fnet_stage2_kernel, n1=n1),
        grid=(n_seq * n1,),
        in_specs=[any_spec, const2(w2.shape)],
        out_specs=any_spec,
        out_shape=jax.ShapeDtypeStruct((n_seq * n2, n1, D), F32),
        scratch_shapes=[pltpu.VMEM((2, n2, D), jnp.uint32), pltpu.VMEM((2, n2, D), F32),
                        pltpu.SemaphoreType.DMA((2,)), pltpu.SemaphoreType.DMA((2,))],
        compiler_params=_params(("arbitrary",), 24 * n2 * D * 4 + 16 * 1024 * 1024),
    )(a, w2)
    return z.reshape(T, D)


def _oproj_kernel(x_ref, a_ref, w_ref, o_ref):
    o_ref[...] = x_ref[...] + _dot(a_ref[...].astype(BF16), w_ref[...])


def _oproj_residual(x, a, w, tm):
    T, D = x.shape
    row = lambda i: (i, 0)
    return pl.pallas_call(
        _oproj_kernel,
        grid=(T // tm,),
        in_specs=[pl.BlockSpec((tm, D), row), pl.BlockSpec((tm, a.shape[1]), row), _const_spec(w.shape)],
        out_specs=pl.BlockSpec((tm, D), row),
        out_shape=jax.ShapeDtypeStruct((T, D), F32),
        compiler_params=_params(("parallel",), 2 * (2 * tm * D * 4 + tm * a.shape[1] * a.dtype.itemsize
                                                    + w.size * 2) + 2 * tm * D * 4),
    )(x, a, w)


def _mlp_kernel(x_ref, g_ref, wup_ref, wdn_ref, o_ref, xn_ref):
    @pl.when(pl.program_id(1) == 0)
    def _():
        x = x_ref[...]
        xn_ref[...] = _rms(x, g_ref[...]).astype(BF16)
        o_ref[...] = x

    hcol = jnp.square(jnp.maximum(_dot(xn_ref[...], wup_ref[...]), 0.0)).astype(BF16)
    o_ref[...] += _dot(hcol, wdn_ref[...])


def _mlp(x, g, w_up, w_down, layer, tm, tf):
    T, D = x.shape
    dff = w_up.shape[2]
    return pl.pallas_call(
        _mlp_kernel,
        grid=(T // tm, dff // tf),
        in_specs=[pl.BlockSpec((tm, D), lambda i, f: (i, 0)), pl.BlockSpec((1, D), lambda i, f: (0, 0)),
                  pl.BlockSpec((None, D, tf), lambda i, f: (layer, 0, f)),
                  pl.BlockSpec((None, tf, D), lambda i, f: (layer, f, 0))],
        out_specs=pl.BlockSpec((tm, D), lambda i, f: (i, 0)),
        out_shape=jax.ShapeDtypeStruct((T, D), F32),
        scratch_shapes=[pltpu.VMEM((tm, D), BF16)],
        compiler_params=_params(("parallel", "arbitrary"),
                                2 * (2 * tm * D * 4 + 2 * D * tf * 2) + tm * D * 2 + 3 * tm * tf * 4),
    )(x, g, w_up, w_down)


def _dup_rope_cols(w):
    return jnp.concatenate([w, w], axis=-1)


def _pad_head_vec(g):
    return jnp.concatenate([g[:QK_NOPE_DIM], _dup_rope_cols(g[QK_NOPE_DIM:])])[None, :].astype(F32)


def _rope_tables(n_seq, seq_len):
    inv_freq = ROPE_THETA ** (-jnp.arange(0, QK_ROPE_DIM, 2, dtype=F32) / QK_ROPE_DIM)
    ang = jnp.arange(seq_len, dtype=F32)[:, None] * inv_freq[None, :]
    c, s = jnp.cos(ang), jnp.sin(ang)
    z = jnp.zeros_like(c)
    tile = lambda t: jnp.tile(t, (n_seq, 1))
    return (tile(c).T, tile(s).T,
            tile(jnp.concatenate([c, c, z, z], axis=1)), tile(jnp.concatenate([-s, s, z, z], axis=1)))


def _pick_tile(n, target):
    t = min(n, target)
    while n % t:
        t //= 2
    return t


def _trunk(x, seq_len, p):
    T = x.shape[0]
    rope = _rope_tables(T // seq_len, seq_len)
    tm = _pick_tile(T, 512)
    tf = _pick_tile(p["w_up"].shape[2], 1024)
    ts = _pick_tile(seq_len, 512)
    for i in range(p["w_up"].shape[0]):
        j = i // 2
        if i % 2 == 0:
            qt, k, vt = _mla_project(x, p["attn_norm"][j], p["w_q_a"][j], p["q_a_norm"][j], p["w_q_b"][j],
                                     p["q_norm"][j], p["w_kv_a"][j], p["kv_a_norm"][j], p["w_k_b"][j], p["w_v_b_t"][j],
                                     p["k_norm"][j], rope, tm)
            a = _attention(qt, k, vt, seq_len, ts, ts)
        else:
            a = _fnet_mix(x, p["fnet_norm"][j], seq_len)
        x = _oproj_residual(x, a, p["w_mix_o"][i], tm)
        x = _mlp(x, p["mlp_norm"][i], p["w_up"], p["w_down"], i, tm, tf)
    return x


def kernel(x_prompt, x_sample, attn_norm, w_q_a, q_a_norm, w_q_b, w_kv_a, kv_a_norm, w_kv_b, q_norm, k_norm,
           w_attn_o, fnet_norm, w_fnet_o, mlp_norm, w_up, w_down):
    depth = mlp_norm.shape[0]
    n_attn = attn_norm.shape[0]
    kv_rank = kv_a_norm.shape[1]
    q_rank = q_a_norm.shape[1]
    row2 = lambda v: v[None, :].astype(F32)
    wqbt = jnp.swapaxes(w_q_b.reshape(n_attn, q_rank, N_HEADS, QK_HEAD_DIM), 1, 3)
    wqbt = jnp.pad(jnp.swapaxes(wqbt, 1, 2), ((0, 0), (0, 0), (0, HEAD_PAD - QK_HEAD_DIM), (0, 0)))
    wqbt = wqbt.reshape(n_attn, N_HEADS * HEAD_PAD, q_rank)
    wkvb = w_kv_b.reshape(n_attn, kv_rank, N_HEADS, QK_NOPE_DIM + V_HEAD_DIM)
    wk = wkvb[..., :QK_NOPE_DIM].reshape(n_attn, kv_rank, N_HEADS * QK_NOPE_DIM)
    wvt = jnp.swapaxes(wkvb[..., QK_NOPE_DIM:].reshape(n_attn, kv_rank, N_HEADS * V_HEAD_DIM), 1, 2)
    q_gain = lambda g: jnp.pad(g, (0, HEAD_PAD - QK_HEAD_DIM))[:, None].astype(F32)
    wkva = jnp.concatenate([w_kv_a[..., :kv_rank], _dup_rope_cols(w_kv_a[..., kv_rank:])], axis=-1)
    layers = lambda w: [w[j].astype(BF16) for j in range(w.shape[0])]
    p = {
        "attn_norm": [row2(g) for g in attn_norm], "w_q_a": layers(w_q_a),
        "q_a_norm": [row2(g) for g in q_a_norm], "w_q_b": layers(wqbt),
        "q_norm": [q_gain(g) for g in q_norm], "w_kv_a": layers(wkva),
        "kv_a_norm": [row2(g) for g in kv_a_norm], "w_k_b": layers(wk), "w_v_b_t": layers(wvt),
        "k_norm": [_pad_head_vec(g) for g in k_norm], "fnet_norm": [row2(g) for g in fnet_norm],
        "w_mix_o": [(w_attn_o if i % 2 == 0 else w_fnet_o)[i // 2].astype(BF16) for i in range(depth)],
        "mlp_norm": [row2(g) for g in mlp_norm], "w_up": w_up.astype(BF16), "w_down": w_down.astype(BF16),
    }
    outs = []
    for xg in (x_prompt, x_sample):
        b, s, d = xg.shape
        outs.append(_trunk(xg.reshape(b * s, d), s, p).reshape(b, s, d))
    return tuple(outs)
```

```python
import functools
import math

import numpy as np
import jax
import jax.numpy as jnp
from jax import lax
from jax.experimental import pallas as pl
from jax.experimental.pallas import tpu as pltpu

F32 = jnp.float32
BF16 = jnp.bfloat16

N_HEADS = 16
QK_NOPE_DIM = 128
QK_ROPE_DIM = 64
QK_HEAD_DIM = QK_NOPE_DIM + QK_ROPE_DIM
V_HEAD_DIM = 128
HEAD_PAD = 256
BF16_ROWS = 16
V_ROWS = V_HEAD_DIM + BF16_ROWS
ROPE_THETA = 10000.0
FNET_GROUPS = 8
EPS = 1e-6
FFT_N2 = 128
LANES = 128
DMA_SLOTS = 4
SCORE_PAD_ROWS = 8
VMEM_CAP = 60 * 1024 * 1024


def _vmem_limit(nbytes):
    return int(min(max(nbytes, 16 * 1024 * 1024), VMEM_CAP))


def _params(sem, nbytes):
    return pltpu.CompilerParams(dimension_semantics=sem, vmem_limit_bytes=_vmem_limit(nbytes))


def _rms(xf, g):
    return xf * lax.rsqrt(jnp.mean(xf * xf, axis=-1, keepdims=True) + EPS) * g


def _dot(a, b):
    return jnp.dot(a, b, preferred_element_type=F32)


def _rope_dup(pe, cos_t, sin_t):
    return pe * cos_t + pltpu.roll(pe, 32, axis=1) * sin_t


def _qproj_kernel(x_ref, ga_ref, wqa_ref, gqa_ref, wqbt_ref, gq_ref, cos_ref, sin_ref, q_ref, *, qscale):
    h = _rms(x_ref[...], ga_ref[...]).astype(BF16)
    cq_t = _rms(_dot(h, wqa_ref[...]), gqa_ref[...]).T.astype(BF16)
    cos_t = cos_ref[...]
    sin_t = sin_ref[...]
    half = QK_ROPE_DIM // 2
    for hh in range(N_HEADS):
        qh = _dot(wqbt_ref[hh * HEAD_PAD:(hh + 1) * HEAD_PAD, :], cq_t)
        ss = jnp.sum(qh * qh, axis=0, keepdims=True)
        qh = qh * (lax.rsqrt(ss * (1.0 / QK_HEAD_DIM) + EPS) * qscale) * gq_ref[...]
        x1 = qh[QK_NOPE_DIM:QK_NOPE_DIM + half]
        x2 = qh[QK_NOPE_DIM + half:QK_HEAD_DIM]
        q_ref[hh, :QK_NOPE_DIM, :] = qh[:QK_NOPE_DIM].astype(BF16)
        q_ref[hh, QK_NOPE_DIM:QK_NOPE_DIM + half, :] = (x1 * cos_t - x2 * sin_t).astype(BF16)
        q_ref[hh, QK_NOPE_DIM + half:QK_HEAD_DIM, :] = (x2 * cos_t + x1 * sin_t).astype(BF16)
        q_ref[hh, QK_HEAD_DIM:, :] = jnp.zeros((HEAD_PAD - QK_HEAD_DIM, qh.shape[1]), BF16)


def _kvproj_kernel(x_ref, ga_ref, wkva_ref, gkva_ref, wk_ref, wvt_ref, gk_ref, cos_ref, sin_ref, k_ref, v_ref,
                   *, kv_rank):
    h = _rms(x_ref[...], ga_ref[...]).astype(BF16)
    ckv = _dot(h, wkva_ref[...])
    c = _rms(ckv[:, :kv_rank], gkva_ref[...])
    pe = ckv[:, kv_rank:]
    pe_ss = 0.5 * jnp.sum(pe * pe, axis=-1, keepdims=True)
    g_nope = gk_ref[:, :QK_NOPE_DIM]
    r = _rope_dup(pe * gk_ref[:, QK_NOPE_DIM:], cos_ref[...], sin_ref[...])
    vt = _dot(wvt_ref[...], c.T.astype(BF16))
    ones_row = (lax.broadcasted_iota(jnp.int32, (BF16_ROWS, vt.shape[1]), 0) == 0).astype(BF16)
    c = c.astype(BF16)
    for pair in range(N_HEADS // 2):
        k_pair = _dot(c, wk_ref[:, pair * 2 * QK_NOPE_DIM:(pair + 1) * 2 * QK_NOPE_DIM])
        for hh in (2 * pair, 2 * pair + 1):
            k_nope = k_pair[:, (hh % 2) * QK_NOPE_DIM:(hh % 2 + 1) * QK_NOPE_DIM]
            ss = jnp.sum(k_nope * k_nope, axis=-1, keepdims=True) + pe_ss
            inv = lax.rsqrt(ss * (1.0 / QK_HEAD_DIM) + EPS)
            k_ref[hh, :, :QK_NOPE_DIM] = (k_nope * inv * g_nope).astype(BF16)
            k_ref[hh, :, QK_NOPE_DIM:] = (r * inv).astype(BF16)
            v_ref[hh, :V_HEAD_DIM, :] = vt[hh * V_HEAD_DIM:(hh + 1) * V_HEAD_DIM].astype(BF16)
            v_ref[hh, V_HEAD_DIM:, :] = ones_row


def _const_spec(shape):
    return pl.BlockSpec(shape, lambda i: (0,) * len(shape))


def _mla_project(x, ga, wqa, gqa, wqbt, gq, wkva, gkva, wk, wvt, gk, rope, tm):
    T, D = x.shape
    q_rank = wqa.shape[1]
    kv_rank = gkva.shape[1]
    qscale = QK_HEAD_DIM ** -0.5 * math.log2(math.e)
    row = lambda i: (i, 0)
    head_rows = lambda i: (0, i, 0)
    head_cols = lambda i: (0, 0, i)
    tab = pl.BlockSpec((tm, LANES), row)
    tab_t = pl.BlockSpec((QK_ROPE_DIM // 2, tm), lambda i: (0, i))
    cos_t, sin_t, cos_dup, sin_dup = rope
    qt = pl.pallas_call(
        functools.partial(_qproj_kernel, qscale=qscale),
        grid=(T // tm,),
        in_specs=[pl.BlockSpec((tm, D), row), _const_spec((1, D)), _const_spec(wqa.shape),
                  _const_spec((1, q_rank)), _const_spec(wqbt.shape), _const_spec((HEAD_PAD, 1)), tab_t, tab_t],
        out_specs=pl.BlockSpec((N_HEADS, HEAD_PAD, tm), head_cols),
        out_shape=jax.ShapeDtypeStruct((N_HEADS, HEAD_PAD, T), BF16),
        compiler_params=_params(("parallel",), 2 * (tm * D * 4 + wqa.size * 2 + wqbt.size * 2
                                                    + N_HEADS * tm * HEAD_PAD * 2) + 8 * tm * D * 4),
    )(x, ga, wqa, gqa, wqbt, gq, cos_t, sin_t)
    k, vt = pl.pallas_call(
        functools.partial(_kvproj_kernel, kv_rank=kv_rank),
        grid=(T // tm,),
        in_specs=[pl.BlockSpec((tm, D), row), _const_spec((1, D)), _const_spec(wkva.shape),
                  _const_spec((1, kv_rank)), _const_spec(wk.shape), _const_spec(wvt.shape),
                  _const_spec((1, HEAD_PAD)), tab, tab],
        out_specs=[pl.BlockSpec((N_HEADS, tm, HEAD_PAD), head_rows),
                   pl.BlockSpec((N_HEADS, V_ROWS, tm), head_cols)],
        out_shape=[jax.ShapeDtypeStruct((N_HEADS, T, HEAD_PAD), BF16),
                   jax.ShapeDtypeStruct((N_HEADS, V_ROWS, T), BF16)],
        compiler_params=_params(("parallel",), 2 * (tm * D * 4 + wkva.size * 2 + wk.size * 2 + wvt.size * 2
                                                    + N_HEADS * tm * (HEAD_PAD + V_ROWS) * 2)
                                + 8 * tm * D * 4),
    )(x, ga, wkva, gkva, wk, wvt, gk, cos_dup, sin_dup)
    return qt, k, vt


def _flash_kernel(qt_ref, k_ref, vt_ref, o_ref, s_ref, *, tk, chunks_per_step):
    qt = qt_ref[0]
    tq = qt.shape[1]
    n_chunks = k_ref.shape[1] // tk

    def scores(j, slot):
        s = _dot(k_ref[0, pl.ds(pl.multiple_of(j * tk, tk), tk), :], qt)
        s_ref[slot, :tk] = s
        return jnp.max(s, axis=0, keepdims=True)

    def absorb(j, slot, s_max, m, acc):
        vt = vt_ref[0, :, pl.ds(pl.multiple_of(j * tk, tk), tk)]
        m_new = jnp.maximum(m, s_max)
        p = jnp.exp2(s_ref[slot, :tk] - m_new)
        acc = jnp.exp2(m - m_new) * acc + _dot(vt, p.astype(BF16))
        return m_new, acc

    def body(step, carry):
        s_max, m, acc = carry
        for c in range(chunks_per_step):
            j = step * chunks_per_step + c
            s_max_next = scores(jnp.minimum(j + 1, n_chunks - 1), (c + 1) % 2)
            m, acc = absorb(j, c % 2, s_max, m, acc)
            s_max = s_max_next
        return s_max, m, acc

    m0 = jnp.full((1, tq), -jnp.inf, F32)
    acc0 = jnp.zeros((V_ROWS, tq), F32)
    _, _, acc = lax.fori_loop(0, n_chunks // chunks_per_step, body, (scores(0, 0), m0, acc0))
    o_ref[...] = (acc[:V_HEAD_DIM] / acc[V_HEAD_DIM:V_HEAD_DIM + 1]).T.astype(BF16)


def _attention(qt, k, vt, seq_len, tq, tk):
    n_seq = k.shape[1] // seq_len
    nq = seq_len // tq
    n_chunks = seq_len // tk
    chunks_per_step = math.gcd(n_chunks, 16)
    if chunks_per_step == n_chunks and n_chunks % 4 == 0:
        chunks_per_step //= 2
    assert chunks_per_step % 2 == 0
    return pl.pallas_call(
        functools.partial(_flash_kernel, tk=tk, chunks_per_step=chunks_per_step),
        grid=(n_seq, N_HEADS, nq),
        in_specs=[pl.BlockSpec((1, HEAD_PAD, tq), lambda b, h, i: (h, 0, b * nq + i)),
                  pl.BlockSpec((1, seq_len, HEAD_PAD), lambda b, h, i: (h, b, 0)),
                  pl.BlockSpec((1, V_ROWS, seq_len), lambda b, h, i: (h, 0, b))],
        out_specs=pl.BlockSpec((tq, V_HEAD_DIM), lambda b, h, i: (b * nq + i, h)),
        out_shape=jax.ShapeDtypeStruct((n_seq * seq_len, N_HEADS * V_HEAD_DIM), BF16),
        scratch_shapes=[pltpu.VMEM((2, tk + SCORE_PAD_ROWS, tq), F32)],
        compiler_params=_params(("parallel", "parallel", "arbitrary"),
                                2 * seq_len * (HEAD_PAD + V_ROWS) * 2 + 8 * tq * tk * 4
                                + 4 * tq * HEAD_PAD * 2),
    )(qt, k, vt)


def _pack_bf16_pair(hi, lo):
    bits = lambda v: lax.bitcast_convert_type(v.astype(BF16).astype(F32), jnp.uint32)
    return bits(hi) | (bits(lo) >> 16)


def _unpack_bf16_pair(w):
    hi = lax.bitcast_convert_type(w & jnp.uint32(0xFFFF0000), F32)
    lo = lax.bitcast_convert_type(w << 16, F32)
    return hi.astype(BF16), lo.astype(BF16)


def _fnet_stage1_kernel(x_hbm, g_ref, wc_ref, w1_ref, tw_ref, a_ref, xbuf, sem, u_ref, *, n1, gdim):
    i = pl.program_id(0)
    slot = i % DMA_SLOTS

    def gather(step):
        rows = pl.ds((step // FFT_N2) * n1, n1)
        to_slot = step % DMA_SLOTS
        return pltpu.make_async_copy(x_hbm.at[rows, step % FFT_N2, :], xbuf.at[to_slot], sem.at[to_slot])

    @pl.when(i == 0)
    def _():
        for ahead in range(DMA_SLOTS - 1):
            gather(ahead).start()

    @pl.when(i + DMA_SLOTS - 1 < pl.num_programs(0))
    def _():
        gather(i + DMA_SLOTS - 1).start()

    gather(i).wait()
    h = _rms(xbuf[slot], g_ref[...]).astype(BF16)
    wc = wc_ref[...]
    for gi in range(FNET_GROUPS):
        cols = slice(gi * gdim, (gi + 1) * gdim)
        u = _dot(h[:, cols], wc)
        u_ref[:n1, cols] = u[:, :gdim]
        u_ref[n1:, cols] = u[:, gdim:]
    a = _dot(w1_ref[...], u_ref[...].astype(BF16))
    a_re = a[:n1]
    a_im = a[n1:]
    tw_c = tw_ref[0, :, 0:1]
    tw_s = tw_ref[0, :, 1:2]
    a_ref[...] = _pack_bf16_pair(a_re * tw_c + a_im * tw_s, a_im * tw_c - a_re * tw_s)


def _fnet_stage2_kernel(a_hbm, w2_ref, z_hbm, abuf, zbuf, in_sem, out_sem, *, n1):
    n2 = FFT_N2
    i = pl.program_id(0)
    n_steps = pl.num_programs(0)
    slot = i % DMA_SLOTS

    def gather(step):
        rows = pl.ds((step // n1) * n2, n2)
        to_slot = step % DMA_SLOTS
        return pltpu.make_async_copy(a_hbm.at[rows, step % n1, :], abuf.at[to_slot], in_sem.at[to_slot])

    def scatter(step):
        rows = pl.ds((step // n1) * n2, n2)
        from_slot = step % DMA_SLOTS
        return pltpu.make_async_copy(zbuf.at[from_slot], z_hbm.at[rows, step % n1, :], out_sem.at[from_slot])

    @pl.when(i == 0)
    def _():
        for ahead in range(DMA_SLOTS - 1):
            gather(ahead).start()

    @pl.when(i + DMA_SLOTS - 1 < n_steps)
    def _():
        gather(i + DMA_SLOTS - 1).start()

    gather(i).wait()
    a_re, a_im = _unpack_bf16_pair(abuf[slot])
    z = _dot(w2_ref[...], jnp.concatenate([a_re, a_im], axis=0))

    @pl.when(i >= DMA_SLOTS)
    def _():
        scatter(i - DMA_SLOTS).wait()

    zbuf[slot] = z
    scatter(i).start()

    @pl.when(i == n_steps - 1)
    def _():
        for back in range(DMA_SLOTS):
            scatter(i - back).wait()


def _dft_tables(n1, n2, gdim):
    def cs(rows, cols, period):
        ang = 2.0 * np.pi * ((np.arange(rows)[:, None] * np.arange(cols)[None, :]) % period) / period
        return np.cos(ang), np.sin(ang)
    cc, sc = cs(gdim, gdim, gdim)
    w_ch = np.concatenate([cc, -sc], axis=1) / np.sqrt(gdim)
    c1, s1 = cs(n1, n1, n1)
    w1 = np.block([[c1, s1], [-s1, c1]]) / np.sqrt(n1)
    c2, s2 = cs(n2, n2, n2)
    w2 = np.concatenate([c2, s2], axis=1) / np.sqrt(n2)
    twc, tws = cs(n2, n1, n1 * n2)
    tw = np.zeros((n2, n1, LANES), np.float64)
    tw[:, :, 0] = twc
    tw[:, :, 1] = tws
    return [jnp.asarray(t, F32) for t in (w_ch, w1, w2, tw)]


def _fnet_mix(x, g, seq_len):
    T, D = x.shape
    n_seq = T // seq_len
    n2 = FFT_N2
    n1 = seq_len // n2
    assert n_seq * min(n1, n2) >= DMA_SLOTS
    gdim = D // FNET_GROUPS
    w_ch, w1, w2, tw = _dft_tables(n1, n2, gdim)
    w_ch, w1, w2 = (t.astype(BF16) for t in (w_ch, w1, w2))
    const2 = lambda shape: pl.BlockSpec(shape, lambda i: (0, 0))
    any_spec = pl.BlockSpec(memory_space=pl.ANY)
    slab = pl.BlockSpec((None, n1, D), lambda i: (i, 0, 0))
    a = pl.pallas_call(
        functools.partial(_fnet_stage1_kernel, n1=n1, gdim=gdim),
        grid=(n_seq * n2,),
        in_specs=[any_spec, const2((1, D)), const2(w_ch.shape), const2(w1.shape),
                  pl.BlockSpec((1, n1, LANES), lambda i: (i % n2, 0, 0))],
        out_specs=slab,
        out_shape=jax.ShapeDtypeStruct((n_seq * n2, n1, D), jnp.uint32),
        scratch_shapes=[pltpu.VMEM((DMA_SLOTS, n1, D), F32), pltpu.SemaphoreType.DMA((DMA_SLOTS,)),
                        pltpu.VMEM((2 * n1, D), F32)],
        compiler_params=_params(("arbitrary",), 24 * n1 * D * 4 + 16 * 1024 * 1024),
    )(x.reshape(T // n2, n2, D), g, w_ch, w1, tw)
    z = pl.pallas_call(
        functools.partial(_fnet_stage2_kernel, n1=n1),
        grid=(n_seq * n1,),
        in_specs=[any_spec, const2(w2.shape)],
        out_specs=any_spec,
        out_shape=jax.ShapeDtypeStruct((n_seq * n2, n1, D), F32),
        scratch_shapes=[pltpu.VMEM((DMA_SLOTS, n2, D), jnp.uint32), pltpu.VMEM((DMA_SLOTS, n2, D), F32),
                        pltpu.SemaphoreType.DMA((DMA_SLOTS,)), pltpu.SemaphoreType.DMA((DMA_SLOTS,))],
        compiler_params=_params(("arbitrary",), 24 * n2 * D * 4 + 16 * 1024 * 1024),
    )(a, w2)
    return z.reshape(T, D)


def _oproj_kernel(x_ref, a_ref, w_ref, o_ref):
    o_ref[...] = x_ref[...] + _dot(a_ref[...].astype(BF16), w_ref[...])


def _oproj_residual(x, a, w, tm):
    T, D = x.shape
    row = lambda i: (i, 0)
    return pl.pallas_call(
        _oproj_kernel,
        grid=(T // tm,),
        in_specs=[pl.BlockSpec((tm, D), row), pl.BlockSpec((tm, a.shape[1]), row), _const_spec(w.shape)],
        out_specs=pl.BlockSpec((tm, D), row),
        out_shape=jax.ShapeDtypeStruct((T, D), F32),
        compiler_params=_params(("parallel",), 2 * (2 * tm * D * 4 + tm * a.shape[1] * a.dtype.itemsize
                                                    + w.size * 2) + 2 * tm * D * 4),
    )(x, a, w)


def _mlp_kernel(x_ref, g_ref, wup_ref, wdn_ref, o_ref, xn_ref):
    @pl.when(pl.program_id(1) == 0)
    def _():
        x = x_ref[...]
        xn_ref[...] = _rms(x, g_ref[...]).astype(BF16)
        o_ref[...] = x

    hcol = jnp.square(jnp.maximum(_dot(xn_ref[...], wup_ref[...]), 0.0)).astype(BF16)
    o_ref[...] += _dot(hcol, wdn_ref[...])


def _mlp(x, g, w_up, w_down, layer, tm, tf):
    T, D = x.shape
    dff = w_up.shape[2]
    return pl.pallas_call(
        _mlp_kernel,
        grid=(T // tm, dff // tf),
        in_specs=[pl.BlockSpec((tm, D), lambda i, f: (i, 0)), pl.BlockSpec((1, D), lambda i, f: (0, 0)),
                  pl.BlockSpec((None, D, tf), lambda i, f: (layer, 0, f)),
                  pl.BlockSpec((None, tf, D), lambda i, f: (layer, f, 0))],
        out_specs=pl.BlockSpec((tm, D), lambda i, f: (i, 0)),
        out_shape=jax.ShapeDtypeStruct((T, D), F32),
        scratch_shapes=[pltpu.VMEM((tm, D), BF16)],
        compiler_params=_params(("parallel", "arbitrary"),
                                2 * (2 * tm * D * 4 + 2 * D * tf * 2) + tm * D * 2 + 3 * tm * tf * 4),
    )(x, g, w_up, w_down)


def _dup_rope_cols(w):
    return jnp.concatenate([w, w], axis=-1)


def _pad_head_vec(g):
    return jnp.concatenate([g[:QK_NOPE_DIM], _dup_rope_cols(g[QK_NOPE_DIM:])])[None, :].astype(F32)


def _rope_tables(n_seq, seq_len):
    inv_freq = ROPE_THETA ** (-jnp.arange(0, QK_ROPE_DIM, 2, dtype=F32) / QK_ROPE_DIM)
    ang = jnp.arange(seq_len, dtype=F32)[:, None] * inv_freq[None, :]
    c, s = jnp.cos(ang), jnp.sin(ang)
    z = jnp.zeros_like(c)
    tile = lambda t: jnp.tile(t, (n_seq, 1))
    return (tile(c).T, tile(s).T,
            tile(jnp.concatenate([c, c, z, z], axis=1)), tile(jnp.concatenate([-s, s, z, z], axis=1)))


def _pick_tile(n, target):
    t = min(n, target)
    while n % t:
        t //= 2
    return t


def _trunk(x, seq_len, p):
    T = x.shape[0]
    rope = _rope_tables(T // seq_len, seq_len)
    tm = _pick_tile(T, 512)
    tf = _pick_tile(p["w_up"].shape[2], 1024)
    ts = _pick_tile(seq_len, 512)
    for i in range(p["w_up"].shape[0]):
        j = i // 2
        if i % 2 == 0:
            qt, k, vt = _mla_project(x, p["attn_norm"][j], p["w_q_a"][j], p["q_a_norm"][j], p["w_q_b"][j],
                                     p["q_norm"][j], p["w_kv_a"][j], p["kv_a_norm"][j], p["w_k_b"][j], p["w_v_b_t"][j],
                                     p["k_norm"][j], rope, tm)
            a = _attention(qt, k, vt, seq_len, ts, ts)
        else:
            a = _fnet_mix(x, p["fnet_norm"][j], seq_len)
        x = _oproj_residual(x, a, p["w_mix_o"][i], tm)
        x = _mlp(x, p["mlp_norm"][i], p["w_up"], p["w_down"], i, tm, tf)
    return x


def kernel(x_prompt, x_sample, attn_norm, w_q_a, q_a_norm, w_q_b, w_kv_a, kv_a_norm, w_kv_b, q_norm, k_norm,
           w_attn_o, fnet_norm, w_fnet_o, mlp_norm, w_up, w_down):
    depth = mlp_norm.shape[0]
    n_attn = attn_norm.shape[0]
    kv_rank = kv_a_norm.shape[1]
    q_rank = q_a_norm.shape[1]
    row2 = lambda v: v[None, :].astype(F32)
    wqbt = jnp.swapaxes(w_q_b.reshape(n_attn, q_rank, N_HEADS, QK_HEAD_DIM), 1, 3)
    wqbt = jnp.pad(jnp.swapaxes(wqbt, 1, 2), ((0, 0), (0, 0), (0, HEAD_PAD - QK_HEAD_DIM), (0, 0)))
    wqbt = wqbt.reshape(n_attn, N_HEADS * HEAD_PAD, q_rank)
    wkvb = w_kv_b.reshape(n_attn, kv_rank, N_HEADS, QK_NOPE_DIM + V_HEAD_DIM)
    wk = wkvb[..., :QK_NOPE_DIM].reshape(n_attn, kv_rank, N_HEADS * QK_NOPE_DIM)
    wvt = jnp.swapaxes(wkvb[..., QK_NOPE_DIM:].reshape(n_attn, kv_rank, N_HEADS * V_HEAD_DIM), 1, 2)
    q_gain = lambda g: jnp.pad(g, (0, HEAD_PAD - QK_HEAD_DIM))[:, None].astype(F32)
    wkva = jnp.concatenate([w_kv_a[..., :kv_rank], _dup_rope_cols(w_kv_a[..., kv_rank:])], axis=-1)
    layers = lambda w: [w[j].astype(BF16) for j in range(w.shape[0])]
    p = {
        "attn_norm": [row2(g) for g in attn_norm], "w_q_a": layers(w_q_a),
        "q_a_norm": [row2(g) for g in q_a_norm], "w_q_b": layers(wqbt),
        "q_norm": [q_gain(g) for g in q_norm], "w_kv_a": layers(wkva),
        "kv_a_norm": [row2(g) for g in kv_a_norm], "w_k_b": layers(wk), "w_v_b_t": layers(wvt),
        "k_norm": [_pad_head_vec(g) for g in k_norm], "fnet_norm": [row2(g) for g in fnet_norm],
        "w_mix_o": [(w_attn_o if i % 2 == 0 else w_fnet_o)[i // 2].astype(BF16) for i in range(depth)],
        "mlp_norm": [row2(g) for g in mlp_norm], "w_up": w_up.astype(BF16), "w_down": w_down.astype(BF16),
    }
    outs = []
    for xg in (x_prompt, x_sample):
        b, s, d = xg.shape
        outs.append(_trunk(xg.reshape(b * s, d), s, p).reshape(b, s, d))
    return tuple(outs)
```

```python
import functools
import math

import numpy as np
import jax
import jax.numpy as jnp
from jax import lax
from jax.experimental import pallas as pl
from jax.experimental.pallas import tpu as pltpu

F32 = jnp.float32
BF16 = jnp.bfloat16

N_HEADS = 16
QK_NOPE_DIM = 128
QK_ROPE_DIM = 64
QK_HEAD_DIM = QK_NOPE_DIM + QK_ROPE_DIM
V_HEAD_DIM = 128
HEAD_PAD = 256
BF16_ROWS = 16
V_ROWS = V_HEAD_DIM + BF16_ROWS
ROPE_THETA = 10000.0
FNET_GROUPS = 8
EPS = 1e-6
FFT_N1 = 128
LANES = 128
DMA_SLOTS = 4
VMEM_CAP = 60 * 1024 * 1024


def _vmem_limit(nbytes):
    return int(min(max(nbytes, 16 * 1024 * 1024), VMEM_CAP))


def _params(sem, nbytes):
    return pltpu.CompilerParams(dimension_semantics=sem, vmem_limit_bytes=_vmem_limit(nbytes))


def _rms(xf, g):
    return xf * lax.rsqrt(jnp.mean(xf * xf, axis=-1, keepdims=True) + EPS) * g


def _dot(a, b):
    return jnp.dot(a, b, preferred_element_type=F32)


def _rope_dup(pe, cos_t, sin_t):
    return pe * cos_t + pltpu.roll(pe, 32, axis=1) * sin_t


def _qproj_kernel(x_ref, ga_ref, wqa_ref, gqa_ref, wqbt_ref, gq_ref, cos_ref, sin_ref, q_ref, *, qscale):
    h = _rms(x_ref[...], ga_ref[...]).astype(BF16)
    cq_t = _rms(_dot(h, wqa_ref[...]), gqa_ref[...]).T.astype(BF16)
    cos_t = cos_ref[...]
    sin_t = sin_ref[...]
    half = QK_ROPE_DIM // 2
    for hh in range(N_HEADS):
        qh = _dot(wqbt_ref[hh * HEAD_PAD:(hh + 1) * HEAD_PAD, :], cq_t)
        ss = jnp.sum(qh * qh, axis=0, keepdims=True)
        qh = qh * (lax.rsqrt(ss * (1.0 / QK_HEAD_DIM) + EPS) * qscale) * gq_ref[...]
        x1 = qh[QK_NOPE_DIM:QK_NOPE_DIM + half]
        x2 = qh[QK_NOPE_DIM + half:QK_HEAD_DIM]
        q_ref[hh, :QK_NOPE_DIM, :] = qh[:QK_NOPE_DIM].astype(BF16)
        q_ref[hh, QK_NOPE_DIM:QK_NOPE_DIM + half, :] = (x1 * cos_t - x2 * sin_t).astype(BF16)
        q_ref[hh, QK_NOPE_DIM + half:QK_HEAD_DIM, :] = (x2 * cos_t + x1 * sin_t).astype(BF16)
        q_ref[hh, QK_HEAD_DIM:, :] = jnp.zeros((HEAD_PAD - QK_HEAD_DIM, qh.shape[1]), BF16)


def _kvproj_kernel(x_ref, ga_ref, wkva_ref, gkva_ref, wk_ref, wvt_ref, gk_ref, cos_ref, sin_ref, k_ref, v_ref,
                   *, kv_rank):
    h = _rms(x_ref[...], ga_ref[...]).astype(BF16)
    ckv = _dot(h, wkva_ref[...])
    c = _rms(ckv[:, :kv_rank], gkva_ref[...])
    pe = ckv[:, kv_rank:]
    pe_ss = 0.5 * jnp.sum(pe * pe, axis=-1, keepdims=True)
    g_nope = gk_ref[:, :QK_NOPE_DIM]
    r = _rope_dup(pe * gk_ref[:, QK_NOPE_DIM:], cos_ref[...], sin_ref[...])
    vt = _dot(wvt_ref[...], c.T.astype(BF16))
    ones_row = (lax.broadcasted_iota(jnp.int32, (BF16_ROWS, vt.shape[1]), 0) == 0).astype(BF16)
    c = c.astype(BF16)
    for pair in range(N_HEADS // 2):
        k_pair = _dot(c, wk_ref[:, pair * 2 * QK_NOPE_DIM:(pair + 1) * 2 * QK_NOPE_DIM])
        for hh in (2 * pair, 2 * pair + 1):
            k_nope = k_pair[:, (hh % 2) * QK_NOPE_DIM:(hh % 2 + 1) * QK_NOPE_DIM]
            ss = jnp.sum(k_nope * k_nope, axis=-1, keepdims=True) + pe_ss
            inv = lax.rsqrt(ss * (1.0 / QK_HEAD_DIM) + EPS)
            k_ref[hh, :, :QK_NOPE_DIM] = (k_nope * inv * g_nope).astype(BF16)
            k_ref[hh, :, QK_NOPE_DIM:] = (r * inv).astype(BF16)
            v_ref[hh, :V_HEAD_DIM, :] = vt[hh * V_HEAD_DIM:(hh + 1) * V_HEAD_DIM].astype(BF16)
            v_ref[hh, V_HEAD_DIM:, :] = ones_row


def _const_spec(shape):
    return pl.BlockSpec(shape, lambda i: (0,) * len(shape))


def _mla_project(x, ga, wqa, gqa, wqbt, gq, wkva, gkva, wk, wvt, gk, rope, tm):
    T, D = x.shape
    q_rank = wqa.shape[1]
    kv_rank = gkva.shape[1]
    qscale = QK_HEAD_DIM ** -0.5 * math.log2(math.e)
    row = lambda i: (i, 0)
    head_rows = lambda i: (0, i, 0)
    head_cols = lambda i: (0, 0, i)
    tab = pl.BlockSpec((tm, LANES), row)
    tab_t = pl.BlockSpec((QK_ROPE_DIM // 2, tm), lambda i: (0, i))
    cos_t, sin_t, cos_dup, sin_dup = rope
    qt = pl.pallas_call(
        functools.partial(_qproj_kernel, qscale=qscale),
        grid=(T // tm,),
        in_specs=[pl.BlockSpec((tm, D), row), _const_spec((1, D)), _const_spec(wqa.shape),
                  _const_spec((1, q_rank)), _const_spec(wqbt.shape), _const_spec((HEAD_PAD, 1)), tab_t, tab_t],
        out_specs=pl.BlockSpec((N_HEADS, HEAD_PAD, tm), head_cols),
        out_shape=jax.ShapeDtypeStruct((N_HEADS, HEAD_PAD, T), BF16),
        compiler_params=_params(("parallel",), 2 * (tm * D * 4 + wqa.size * 2 + wqbt.size * 2
                                                    + N_HEADS * tm * HEAD_PAD * 2) + 8 * tm * D * 4),
    )(x, ga, wqa, gqa, wqbt, gq, cos_t, sin_t)
    k, vt = pl.pallas_call(
        functools.partial(_kvproj_kernel, kv_rank=kv_rank),
        grid=(T // tm,),
        in_specs=[pl.BlockSpec((tm, D), row), _const_spec((1, D)), _const_spec(wkva.shape),
                  _const_spec((1, kv_rank)), _const_spec(wk.shape), _const_spec(wvt.shape),
                  _const_spec((1, HEAD_PAD)), tab, tab],
        out_specs=[pl.BlockSpec((N_HEADS, tm, HEAD_PAD), head_rows),
                   pl.BlockSpec((N_HEADS, V_ROWS, tm), head_cols)],
        out_shape=[jax.ShapeDtypeStruct((N_HEADS, T, HEAD_PAD), BF16),
                   jax.ShapeDtypeStruct((N_HEADS, V_ROWS, T), BF16)],
        compiler_params=_params(("parallel",), 2 * (tm * D * 4 + wkva.size * 2 + wk.size * 2 + wvt.size * 2
                                                    + N_HEADS * tm * (HEAD_PAD + V_ROWS) * 2)
                                + 8 * tm * D * 4),
    )(x, ga, wkva, gkva, wk, wvt, gk, cos_dup, sin_dup)
    return qt, k, vt


def _flash_kernel(qt_ref, k_ref, vt_ref, o_ref, s_ref, *, tk, chunks_per_step):
    qt = qt_ref[0]
    tq = qt.shape[1]
    n_chunks = k_ref.shape[1] // tk

    def scores(j, slot):
        s = _dot(k_ref[0, pl.ds(pl.multiple_of(j * tk, tk), tk), :], qt)
        s_ref[slot] = s
        return jnp.max(s, axis=0, keepdims=True)

    def absorb(j, slot, s_max, m, acc):
        vt = vt_ref[0, :, pl.ds(pl.multiple_of(j * tk, tk), tk)]
        m_new = jnp.maximum(m, s_max)
        p = jnp.exp2(s_ref[slot] - m_new)
        acc = jnp.exp2(m - m_new) * acc + _dot(vt, p.astype(BF16))
        return m_new, acc

    def body(step, carry):
        s_max, m, acc = carry
        for c in range(chunks_per_step):
            j = step * chunks_per_step + c
            s_max_next = scores(jnp.minimum(j + 1, n_chunks - 1), (c + 1) % 2)
            m, acc = absorb(j, c % 2, s_max, m, acc)
            s_max = s_max_next
        return s_max, m, acc

    m0 = jnp.full((1, tq), -jnp.inf, F32)
    acc0 = jnp.zeros((V_ROWS, tq), F32)
    _, _, acc = lax.fori_loop(0, n_chunks // chunks_per_step, body, (scores(0, 0), m0, acc0))
    o_ref[...] = (acc[:V_HEAD_DIM] / acc[V_HEAD_DIM:V_HEAD_DIM + 1]).T.astype(BF16)


def _attention(qt, k, vt, seq_len, tq, tk):
    n_seq = k.shape[1] // seq_len
    nq = seq_len // tq
    n_chunks = seq_len // tk
    chunks_per_step = math.gcd(n_chunks, 16)
    if chunks_per_step == n_chunks and n_chunks % 4 == 0:
        chunks_per_step //= 2
    assert chunks_per_step % 2 == 0
    return pl.pallas_call(
        functools.partial(_flash_kernel, tk=tk, chunks_per_step=chunks_per_step),
        grid=(n_seq, N_HEADS, nq),
        in_specs=[pl.BlockSpec((1, HEAD_PAD, tq), lambda b, h, i: (h, 0, b * nq + i)),
                  pl.BlockSpec((1, seq_len, HEAD_PAD), lambda b, h, i: (h, b, 0)),
                  pl.BlockSpec((1, V_ROWS, seq_len), lambda b, h, i: (h, 0, b))],
        out_specs=pl.BlockSpec((tq, V_HEAD_DIM), lambda b, h, i: (b * nq + i, h)),
        out_shape=jax.ShapeDtypeStruct((n_seq * seq_len, N_HEADS * V_HEAD_DIM), BF16),
        scratch_shapes=[pltpu.VMEM((2, tk, tq), F32)],
        compiler_params=_params(("parallel", "parallel", "arbitrary"),
                                2 * seq_len * (HEAD_PAD + V_ROWS) * 2 + 8 * tq * tk * 4
                                + 4 * tq * HEAD_PAD * 2),
    )(qt, k, vt)


def _pack_bf16_pair(hi, lo):
    bits = lambda v: lax.bitcast_convert_type(v.astype(BF16).astype(F32), jnp.uint32)
    return bits(hi) | (bits(lo) >> 16)


def _unpack_bf16_pair(w):
    hi = lax.bitcast_convert_type(w & jnp.uint32(0xFFFF0000), F32)
    lo = lax.bitcast_convert_type(w << 16, F32)
    return hi.astype(BF16), lo.astype(BF16)


def _fnet_stage1_kernel(x_hbm, g_ref, wc_ref, w1_ref, tw_ref, a_ref, xbuf, sem, u_ref, *, n1, n2, gdim):
    i = pl.program_id(0)
    slot = i % DMA_SLOTS

    def gather(step):
        rows = pl.ds((step // n2) * n1, n1)
        to_slot = step % DMA_SLOTS
        return pltpu.make_async_copy(x_hbm.at[rows, step % n2, :], xbuf.at[to_slot], sem.at[to_slot])

    @pl.when(i == 0)
    def _():
        for ahead in range(DMA_SLOTS - 1):
            gather(ahead).start()

    @pl.when(i + DMA_SLOTS - 1 < pl.num_programs(0))
    def _():
        gather(i + DMA_SLOTS - 1).start()

    gather(i).wait()
    h = _rms(xbuf[slot], g_ref[...]).astype(BF16)
    wc = wc_ref[...]
    for gi in range(FNET_GROUPS):
        cols = slice(gi * gdim, (gi + 1) * gdim)
        u = _dot(h[:, cols], wc)
        u_ref[:n1, cols] = u[:, :gdim]
        u_ref[n1:, cols] = u[:, gdim:]
    a = _dot(w1_ref[...], u_ref[...].astype(BF16))
    a_re = a[:n1]
    a_im = a[n1:]
    tw_c = tw_ref[0, :, 0:1]
    tw_s = tw_ref[0, :, 1:2]
    a_ref[...] = _pack_bf16_pair(a_re * tw_c + a_im * tw_s, a_im * tw_c - a_re * tw_s)


def _fnet_stage2_kernel(a_hbm, w2_ref, z_hbm, abuf, zbuf, in_sem, out_sem, *, n1, n2):
    i = pl.program_id(0)
    n_steps = pl.num_programs(0)
    slot = i % DMA_SLOTS

    def gather(step):
        rows = pl.ds((step // n1) * n2, n2)
        to_slot = step % DMA_SLOTS
        return pltpu.make_async_copy(a_hbm.at[rows, step % n1, :], abuf.at[to_slot], in_sem.at[to_slot])

    def scatter(step):
        rows = pl.ds((step // n1) * n2, n2)
        from_slot = step % DMA_SLOTS
        return pltpu.make_async_copy(zbuf.at[from_slot], z_hbm.at[rows, step % n1, :], out_sem.at[from_slot])

    @pl.when(i == 0)
    def _():
        for ahead in range(DMA_SLOTS - 1):
            gather(ahead).start()

    @pl.when(i + DMA_SLOTS - 1 < n_steps)
    def _():
        gather(i + DMA_SLOTS - 1).start()

    gather(i).wait()
    a_re, a_im = _unpack_bf16_pair(abuf[slot])
    z = _dot(w2_ref[...], jnp.concatenate([a_re, a_im], axis=0))

    @pl.when(i >= DMA_SLOTS)
    def _():
        scatter(i - DMA_SLOTS).wait()

    zbuf[slot] = z
    scatter(i).start()

    @pl.when(i == n_steps - 1)
    def _():
        for back in range(DMA_SLOTS):
            scatter(i - back).wait()


def _dft_tables(n1, n2, gdim):
    def cs(rows, cols, period):
        ang = 2.0 * np.pi * ((np.arange(rows)[:, None] * np.arange(cols)[None, :]) % period) / period
        return np.cos(ang), np.sin(ang)
    cc, sc = cs(gdim, gdim, gdim)
    w_ch = np.concatenate([cc, -sc], axis=1) / np.sqrt(gdim)
    c1, s1 = cs(n1, n1, n1)
    w1 = np.block([[c1, s1], [-s1, c1]]) / np.sqrt(n1)
    c2, s2 = cs(n2, n2, n2)
    w2 = np.concatenate([c2, s2], axis=1) / np.sqrt(n2)
    twc, tws = cs(n2, n1, n1 * n2)
    tw = np.zeros((n2, n1, LANES), np.float64)
    tw[:, :, 0] = twc
    tw[:, :, 1] = tws
    return [jnp.asarray(t, F32) for t in (w_ch, w1, w2, tw)]


def _fnet_mix(x, g, seq_len):
    T, D = x.shape
    n_seq = T // seq_len
    n1 = FFT_N1
    n2 = seq_len // n1
    assert n_seq * min(n1, n2) >= DMA_SLOTS
    gdim = D // FNET_GROUPS
    w_ch, w1, w2, tw = _dft_tables(n1, n2, gdim)
    w_ch, w1, w2 = (t.astype(BF16) for t in (w_ch, w1, w2))
    const2 = lambda shape: pl.BlockSpec(shape, lambda i: (0, 0))
    any_spec = pl.BlockSpec(memory_space=pl.ANY)
    slab = pl.BlockSpec((None, n1, D), lambda i: (i, 0, 0))
    a = pl.pallas_call(
        functools.partial(_fnet_stage1_kernel, n1=n1, n2=n2, gdim=gdim),
        grid=(n_seq * n2,),
        in_specs=[any_spec, const2((1, D)), const2(w_ch.shape), const2(w1.shape),
                  pl.BlockSpec((1, n1, LANES), lambda i: (i % n2, 0, 0))],
        out_specs=slab,
        out_shape=jax.ShapeDtypeStruct((n_seq * n2, n1, D), jnp.uint32),
        scratch_shapes=[pltpu.VMEM((DMA_SLOTS, n1, D), F32), pltpu.SemaphoreType.DMA((DMA_SLOTS,)),
                        pltpu.VMEM((2 * n1, D), F32)],
        compiler_params=_params(("arbitrary",), 24 * n1 * D * 4 + 16 * 1024 * 1024),
    )(x.reshape(T // n2, n2, D), g, w_ch, w1, tw)
    z = pl.pallas_call(
        functools.partial(_fnet_stage2_kernel, n1=n1, n2=n2),
        grid=(n_seq * n1,),
        in_specs=[any_spec, const2(w2.shape)],
        out_specs=any_spec,
        out_shape=jax.ShapeDtypeStruct((n_seq * n2, n1, D), F32),
        scratch_shapes=[pltpu.VMEM((DMA_SLOTS, n2, D), jnp.uint32), pltpu.VMEM((DMA_SLOTS, n2, D), F32),
                        pltpu.SemaphoreType.DMA((DMA_SLOTS,)), pltpu.SemaphoreType.DMA((DMA_SLOTS,))],
        compiler_params=_params(("arbitrary",), 24 * n2 * D * 4 + 16 * 1024 * 1024),
    )(a, w2)
    return z.reshape(T, D)


def _oproj_kernel(x_ref, a_ref, w_ref, o_ref):
    o_ref[...] = x_ref[...] + _dot(a_ref[...].astype(BF16), w_ref[...])


def _oproj_residual(x, a, w, tm):
    T, D = x.shape
    row = lambda i: (i, 0)
    return pl.pallas_call(
        _oproj_kernel,
        grid=(T // tm,),
        in_specs=[pl.BlockSpec((tm, D), row), pl.BlockSpec((tm, a.shape[1]), row), _const_spec(w.shape)],
        out_specs=pl.BlockSpec((tm, D), row),
        out_shape=jax.ShapeDtypeStruct((T, D), F32),
        compiler_params=_params(("parallel",), 2 * (2 * tm * D * 4 + tm * a.shape[1] * a.dtype.itemsize
                                                    + w.size * 2) + 2 * tm * D * 4),
    )(x, a, w)


def _mlp_kernel(x_ref, g_ref, wup_ref, wdn_ref, o_ref, xn_ref):
    @pl.when(pl.program_id(1) == 0)
    def _():
        x = x_ref[...]
        xn_ref[...] = _rms(x, g_ref[...]).astype(BF16)
        o_ref[...] = x

    hcol = jnp.square(jnp.maximum(_dot(xn_ref[...], wup_ref[...]), 0.0)).astype(BF16)
    o_ref[...] += _dot(hcol, wdn_ref[...])


def _mlp(x, g, w_up, w_down, layer, tm, tf):
    T, D = x.shape
    dff = w_up.shape[2]
    return pl.pallas_call(
        _mlp_kernel,
        grid=(T // tm, dff // tf),
        in_specs=[pl.BlockSpec((tm, D), lambda i, f: (i, 0)), pl.BlockSpec((1, D), lambda i, f: (0, 0)),
                  pl.BlockSpec((None, D, tf), lambda i, f: (layer, 0, f)),
                  pl.BlockSpec((None, tf, D), lambda i, f: (layer, f, 0))],
        out_specs=pl.BlockSpec((tm, D), lambda i, f: (i, 0)),
        out_shape=jax.ShapeDtypeStruct((T, D), F32),
        scratch_shapes=[pltpu.VMEM((tm, D), BF16)],
        compiler_params=_params(("parallel", "arbitrary"),
                                2 * (2 * tm * D * 4 + 2 * D * tf * 2) + tm * D * 2 + 3 * tm * tf * 4),
    )(x, g, w_up, w_down)


def _dup_rope_cols(w):
    return jnp.concatenate([w, w], axis=-1)


def _pad_head_vec(g):
    return jnp.concatenate([g[:QK_NOPE_DIM], _dup_rope_cols(g[QK_NOPE_DIM:])])[None, :].astype(F32)


def _rope_tables(n_seq, seq_len):
    inv_freq = ROPE_THETA ** (-jnp.arange(0, QK_ROPE_DIM, 2, dtype=F32) / QK_ROPE_DIM)
    ang = jnp.arange(seq_len, dtype=F32)[:, None] * inv_freq[None, :]
    c, s = jnp.cos(ang), jnp.sin(ang)
    z = jnp.zeros_like(c)
    tile = lambda t: jnp.tile(t, (n_seq, 1))
    return (tile(c).T, tile(s).T,
            tile(jnp.concatenate([c, c, z, z], axis=1)), tile(jnp.concatenate([-s, s, z, z], axis=1)))


def _pick_tile(n, target):
    t = min(n, target)
    while n % t:
        t //= 2
    return t


def _trunk(x, seq_len, p):
    T = x.shape[0]
    rope = _rope_tables(T // seq_len, seq_len)
    tm = _pick_tile(T, 512)
    tf = _pick_tile(p["w_up"].shape[2], 1024)
    ts = _pick_tile(seq_len, 512)
    for i in range(p["w_up"].shape[0]):
        j = i // 2
        if i % 2 == 0:
            qt, k, vt = _mla_project(x, p["attn_norm"][j], p["w_q_a"][j], p["q_a_norm"][j], p["w_q_b"][j],
                                     p["q_norm"][j], p["w_kv_a"][j], p["kv_a_norm"][j], p["w_k_b"][j], p["w_v_b_t"][j],
                                     p["k_norm"][j], rope, tm)
            a = _attention(qt, k, vt, seq_len, ts, ts)
        else:
            a = _fnet_mix(x, p["fnet_norm"][j], seq_len)
        x = _oproj_residual(x, a, p["w_mix_o"][i], tm)
        x = _mlp(x, p["mlp_norm"][i], p["w_up"], p["w_down"], i, tm, tf)
    return x


def kernel(x_prompt, x_sample, attn_norm, w_q_a, q_a_norm, w_q_b, w_kv_a, kv_a_norm, w_kv_b, q_norm, k_norm,
           w_attn_o, fnet_norm, w_fnet_o, mlp_norm, w_up, w_down):
    depth = mlp_norm.shape[0]
    n_attn = attn_norm.shape[0]
    kv_rank = kv_a_norm.shape[1]
    q_rank = q_a_norm.shape[1]
    row2 = lambda v: v[None, :].astype(F32)
    wqbt = jnp.swapaxes(w_q_b.reshape(n_attn, q_rank, N_HEADS, QK_HEAD_DIM), 1, 3)
    wqbt = jnp.pad(jnp.swapaxes(wqbt, 1, 2), ((0, 0), (0, 0), (0, HEAD_PAD - QK_HEAD_DIM), (0, 0)))
    wqbt = wqbt.reshape(n_attn, N_HEADS * HEAD_PAD, q_rank)
    wkvb = w_kv_b.reshape(n_attn, kv_rank, N_HEADS, QK_NOPE_DIM + V_HEAD_DIM)
    wk = wkvb[..., :QK_NOPE_DIM].reshape(n_attn, kv_rank, N_HEADS * QK_NOPE_DIM)
    wvt = jnp.swapaxes(wkvb[..., QK_NOPE_DIM:].reshape(n_attn, kv_rank, N_HEADS * V_HEAD_DIM), 1, 2)
    q_gain = lambda g: jnp.pad(g, (0, HEAD_PAD - QK_HEAD_DIM))[:, None].astype(F32)
    wkva = jnp.concatenate([w_kv_a[..., :kv_rank], _dup_rope_cols(w_kv_a[..., kv_rank:])], axis=-1)
    layers = lambda w: [w[j].astype(BF16) for j in range(w.shape[0])]
    p = {
        "attn_norm": [row2(g) for g in attn_norm], "w_q_a": layers(w_q_a),
        "q_a_norm": [row2(g) for g in q_a_norm], "w_q_b": layers(wqbt),
        "q_norm": [q_gain(g) for g in q_norm], "w_kv_a": layers(wkva),
        "kv_a_norm": [row2(g) for g in kv_a_norm], "w_k_b": layers(wk), "w_v_b_t": layers(wvt),
        "k_norm": [_pad_head_vec(g) for g in k_norm], "fnet_norm": [row2(g) for g in fnet_norm],
        "w_mix_o": [(w_attn_o if i % 2 == 0 else w_fnet_o)[i // 2].astype(BF16) for i in range(depth)],
        "mlp_norm": [row2(g) for g in mlp_norm], "w_up": w_up.astype(BF16), "w_down": w_down.astype(BF16),
    }
    outs = []
    for xg in (x_prompt, x_sample):
        b, s, d = xg.shape
        outs.append(_trunk(xg.reshape(b * s, d), s, p).reshape(b, s, d))
    return tuple(outs)
```

```python
import functools
import math

import numpy as np
import jax
import jax.numpy as jnp
from jax import lax
from jax.experimental import pallas as pl
from jax.experimental.pallas import tpu as pltpu

F32 = jnp.float32
BF16 = jnp.bfloat16

N_HEADS = 16
QK_NOPE_DIM = 128
QK_ROPE_DIM = 64
QK_HEAD_DIM = QK_NOPE_DIM + QK_ROPE_DIM
V_HEAD_DIM = 128
HEAD_PAD = 256
BF16_ROWS = 16
V_ROWS = V_HEAD_DIM + BF16_ROWS
ROPE_THETA = 10000.0
FNET_GROUPS = 8
EPS = 1e-6
FFT_N1 = 128
LANES = 128
DMA_SLOTS = 8
SCORE_PAD_ROWS = 8
VMEM_CAP = 60 * 1024 * 1024


def _vmem_limit(nbytes):
    return int(min(max(nbytes, 16 * 1024 * 1024), VMEM_CAP))


def _params(sem, nbytes):
    return pltpu.CompilerParams(dimension_semantics=sem, vmem_limit_bytes=_vmem_limit(nbytes))


def _rms(xf, g):
    return xf * lax.rsqrt(jnp.mean(xf * xf, axis=-1, keepdims=True) + EPS) * g


def _dot(a, b):
    return jnp.dot(a, b, preferred_element_type=F32)


def _rope_dup(pe, cos_t, sin_t):
    return pe * cos_t + pltpu.roll(pe, 32, axis=1) * sin_t


def _qproj_kernel(x_ref, ga_ref, wqa_ref, gqa_ref, wqbt_ref, gq_ref, cos_ref, sin_ref, q_ref, *, qscale):
    h = _rms(x_ref[...], ga_ref[...]).astype(BF16)
    cq_t = _rms(_dot(h, wqa_ref[...]), gqa_ref[...]).T.astype(BF16)
    cos_t = cos_ref[...]
    sin_t = sin_ref[...]
    half = QK_ROPE_DIM // 2
    for hh in range(N_HEADS):
        qh = _dot(wqbt_ref[hh * HEAD_PAD:(hh + 1) * HEAD_PAD, :], cq_t)
        ss = jnp.sum(qh * qh, axis=0, keepdims=True)
        qh = qh * (lax.rsqrt(ss * (1.0 / QK_HEAD_DIM) + EPS) * qscale) * gq_ref[...]
        x1 = qh[QK_NOPE_DIM:QK_NOPE_DIM + half]
        x2 = qh[QK_NOPE_DIM + half:QK_HEAD_DIM]
        q_ref[hh, :QK_NOPE_DIM, :] = qh[:QK_NOPE_DIM].astype(BF16)
        q_ref[hh, QK_NOPE_DIM:QK_NOPE_DIM + half, :] = (x1 * cos_t - x2 * sin_t).astype(BF16)
        q_ref[hh, QK_NOPE_DIM + half:QK_HEAD_DIM, :] = (x2 * cos_t + x1 * sin_t).astype(BF16)
        q_ref[hh, QK_HEAD_DIM:, :] = jnp.zeros((HEAD_PAD - QK_HEAD_DIM, qh.shape[1]), BF16)


def _kvproj_kernel(x_ref, ga_ref, wkva_ref, gkva_ref, wk_ref, wvt_ref, gk_ref, cos_ref, sin_ref, k_ref, v_ref,
                   *, kv_rank):
    h = _rms(x_ref[...], ga_ref[...]).astype(BF16)
    ckv = _dot(h, wkva_ref[...])
    c = _rms(ckv[:, :kv_rank], gkva_ref[...])
    pe = ckv[:, kv_rank:]
    pe_ss = 0.5 * jnp.sum(pe * pe, axis=-1, keepdims=True)
    g_nope = gk_ref[:, :QK_NOPE_DIM]
    r = _rope_dup(pe * gk_ref[:, QK_NOPE_DIM:], cos_ref[...], sin_ref[...])
    vt = _dot(wvt_ref[...], c.T.astype(BF16))
    ones_row = (lax.broadcasted_iota(jnp.int32, (BF16_ROWS, vt.shape[1]), 0) == 0).astype(BF16)
    c = c.astype(BF16)
    for pair in range(N_HEADS // 2):
        k_pair = _dot(c, wk_ref[:, pair * 2 * QK_NOPE_DIM:(pair + 1) * 2 * QK_NOPE_DIM])
        for hh in (2 * pair, 2 * pair + 1):
            k_nope = k_pair[:, (hh % 2) * QK_NOPE_DIM:(hh % 2 + 1) * QK_NOPE_DIM]
            ss = jnp.sum(k_nope * k_nope, axis=-1, keepdims=True) + pe_ss
            inv = lax.rsqrt(ss * (1.0 / QK_HEAD_DIM) + EPS)
            k_ref[hh, :, :QK_NOPE_DIM] = (k_nope * inv * g_nope).astype(BF16)
            k_ref[hh, :, QK_NOPE_DIM:] = (r * inv).astype(BF16)
            v_ref[hh, :V_HEAD_DIM, :] = vt[hh * V_HEAD_DIM:(hh + 1) * V_HEAD_DIM].astype(BF16)
            v_ref[hh, V_HEAD_DIM:, :] = ones_row


def _const_spec(shape):
    return pl.BlockSpec(shape, lambda i: (0,) * len(shape))


def _mla_project(x, ga, wqa, gqa, wqbt, gq, wkva, gkva, wk, wvt, gk, rope, tm):
    T, D = x.shape
    q_rank = wqa.shape[1]
    kv_rank = gkva.shape[1]
    qscale = QK_HEAD_DIM ** -0.5 * math.log2(math.e)
    row = lambda i: (i, 0)
    head_rows = lambda i: (0, i, 0)
    head_cols = lambda i: (0, 0, i)
    tab = pl.BlockSpec((tm, LANES), row)
    tab_t = pl.BlockSpec((QK_ROPE_DIM // 2, tm), lambda i: (0, i))
    cos_t, sin_t, cos_dup, sin_dup = rope
    qt = pl.pallas_call(
        functools.partial(_qproj_kernel, qscale=qscale),
        grid=(T // tm,),
        in_specs=[pl.BlockSpec((tm, D), row), _const_spec((1, D)), _const_spec(wqa.shape),
                  _const_spec((1, q_rank)), _const_spec(wqbt.shape), _const_spec((HEAD_PAD, 1)), tab_t, tab_t],
        out_specs=pl.BlockSpec((N_HEADS, HEAD_PAD, tm), head_cols),
        out_shape=jax.ShapeDtypeStruct((N_HEADS, HEAD_PAD, T), BF16),
        compiler_params=_params(("parallel",), 2 * (tm * D * 4 + wqa.size * 2 + wqbt.size * 2
                                                    + N_HEADS * tm * HEAD_PAD * 2) + 8 * tm * D * 4),
    )(x, ga, wqa, gqa, wqbt, gq, cos_t, sin_t)
    k, vt = pl.pallas_call(
        functools.partial(_kvproj_kernel, kv_rank=kv_rank),
        grid=(T // tm,),
        in_specs=[pl.BlockSpec((tm, D), row), _const_spec((1, D)), _const_spec(wkva.shape),
                  _const_spec((1, kv_rank)), _const_spec(wk.shape), _const_spec(wvt.shape),
                  _const_spec((1, HEAD_PAD)), tab, tab],
        out_specs=[pl.BlockSpec((N_HEADS, tm, HEAD_PAD), head_rows),
                   pl.BlockSpec((N_HEADS, V_ROWS, tm), head_cols)],
        out_shape=[jax.ShapeDtypeStruct((N_HEADS, T, HEAD_PAD), BF16),
                   jax.ShapeDtypeStruct((N_HEADS, V_ROWS, T), BF16)],
        compiler_params=_params(("parallel",), 2 * (tm * D * 4 + wkva.size * 2 + wk.size * 2 + wvt.size * 2
                                                    + N_HEADS * tm * (HEAD_PAD + V_ROWS) * 2)
                                + 8 * tm * D * 4),
    )(x, ga, wkva, gkva, wk, wvt, gk, cos_dup, sin_dup)
    return qt, k, vt


def _flash_kernel(qt_ref, k_ref, vt_ref, o_ref, s_ref, *, tk, chunks_per_step):
    qt = qt_ref[0]
    tq = qt.shape[1]
    n_chunks = k_ref.shape[1] // tk

    def scores(j, slot):
        s = _dot(k_ref[0, pl.ds(pl.multiple_of(j * tk, tk), tk), :], qt)
        s_ref[slot, :tk] = s
        return jnp.max(s, axis=0, keepdims=True)

    def absorb(j, slot, s_max, m, acc):
        vt = vt_ref[0, :, pl.ds(pl.multiple_of(j * tk, tk), tk)]
        m_new = jnp.maximum(m, s_max)
        p = jnp.exp2(s_ref[slot, :tk] - m_new)
        acc = jnp.exp2(m - m_new) * acc + _dot(vt, p.astype(BF16))
        return m_new, acc

    def body(step, carry):
        s_max, m, acc = carry
        for c in range(chunks_per_step):
            j = step * chunks_per_step + c
            s_max_next = scores(jnp.minimum(j + 1, n_chunks - 1), (c + 1) % 2)
            m, acc = absorb(j, c % 2, s_max, m, acc)
            s_max = s_max_next
        return s_max, m, acc

    m0 = jnp.full((1, tq), -jnp.inf, F32)
    acc0 = jnp.zeros((V_ROWS, tq), F32)
    _, _, acc = lax.fori_loop(0, n_chunks // chunks_per_step, body, (scores(0, 0), m0, acc0))
    o_ref[...] = (acc[:V_HEAD_DIM] / acc[V_HEAD_DIM:V_HEAD_DIM + 1]).T.astype(BF16)


def _attention(qt, k, vt, seq_len, tq, tk):
    n_seq = k.shape[1] // seq_len
    nq = seq_len // tq
    n_chunks = seq_len // tk
    chunks_per_step = math.gcd(n_chunks, 16)
    if chunks_per_step == n_chunks and n_chunks % 4 == 0:
        chunks_per_step //= 2
    assert chunks_per_step % 2 == 0
    return pl.pallas_call(
        functools.partial(_flash_kernel, tk=tk, chunks_per_step=chunks_per_step),
        grid=(n_seq, N_HEADS, nq),
        in_specs=[pl.BlockSpec((1, HEAD_PAD, tq), lambda b, h, i: (h, 0, b * nq + i)),
                  pl.BlockSpec((1, seq_len, HEAD_PAD), lambda b, h, i: (h, b, 0)),
                  pl.BlockSpec((1, V_ROWS, seq_len), lambda b, h, i: (h, 0, b))],
        out_specs=pl.BlockSpec((tq, V_HEAD_DIM), lambda b, h, i: (b * nq + i, h)),
        out_shape=jax.ShapeDtypeStruct((n_seq * seq_len, N_HEADS * V_HEAD_DIM), BF16),
        scratch_shapes=[pltpu.VMEM((2, tk + SCORE_PAD_ROWS, tq), F32)],
        compiler_params=_params(("parallel", "parallel", "arbitrary"),
                                2 * seq_len * (HEAD_PAD + V_ROWS) * 2 + 8 * tq * tk * 4
                                + 4 * tq * HEAD_PAD * 2),
    )(qt, k, vt)


def _pack_bf16_pair(hi, lo):
    bits = lambda v: lax.bitcast_convert_type(v.astype(BF16).astype(F32), jnp.uint32)
    return bits(hi) | (bits(lo) >> 16)


def _unpack_bf16_pair(w):
    hi = lax.bitcast_convert_type(w & jnp.uint32(0xFFFF0000), F32)
    lo = lax.bitcast_convert_type(w << 16, F32)
    return hi.astype(BF16), lo.astype(BF16)


def _fnet_stage1_kernel(x_hbm, g_ref, wc_ref, w1_ref, tw_ref, a_ref, xbuf, sem, u_ref, *, n1, n2, gdim):
    i = pl.program_id(0)
    slot = i % DMA_SLOTS

    def gather(step):
        rows = pl.ds((step // n2) * n1, n1)
        to_slot = step % DMA_SLOTS
        return pltpu.make_async_copy(x_hbm.at[rows, step % n2, :], xbuf.at[to_slot], sem.at[to_slot])

    @pl.when(i == 0)
    def _():
        for ahead in range(DMA_SLOTS - 1):
            gather(ahead).start()

    @pl.when(i + DMA_SLOTS - 1 < pl.num_programs(0))
    def _():
        gather(i + DMA_SLOTS - 1).start()

    gather(i).wait()
    h = _rms(xbuf[slot], g_ref[...]).astype(BF16)
    wc = wc_ref[...]
    for gi in range(FNET_GROUPS):
        cols = slice(gi * gdim, (gi + 1) * gdim)
        u = _dot(h[:, cols], wc)
        u_ref[:n1, cols] = u[:, :gdim]
        u_ref[n1:, cols] = u[:, gdim:]
    a = _dot(w1_ref[...], u_ref[...].astype(BF16))
    a_re = a[:n1]
    a_im = a[n1:]
    tw_c = tw_ref[0, :, 0:1]
    tw_s = tw_ref[0, :, 1:2]
    a_ref[...] = _pack_bf16_pair(a_re * tw_c + a_im * tw_s, a_im * tw_c - a_re * tw_s)


def _fnet_stage2_kernel(a_hbm, w2_ref, z_hbm, abuf, zbuf, in_sem, out_sem, *, n1, n2):
    i = pl.program_id(0)
    n_steps = pl.num_programs(0)
    slot = i % DMA_SLOTS

    def gather(step):
        rows = pl.ds((step // n1) * n2, n2)
        to_slot = step % DMA_SLOTS
        return pltpu.make_async_copy(a_hbm.at[rows, step % n1, :], abuf.at[to_slot], in_sem.at[to_slot])

    def scatter(step):
        rows = pl.ds((step // n1) * n2, n2)
        from_slot = step % DMA_SLOTS
        return pltpu.make_async_copy(zbuf.at[from_slot], z_hbm.at[rows, step % n1, :], out_sem.at[from_slot])

    @pl.when(i == 0)
    def _():
        for ahead in range(DMA_SLOTS - 1):
            gather(ahead).start()

    @pl.when(i + DMA_SLOTS - 1 < n_steps)
    def _():
        gather(i + DMA_SLOTS - 1).start()

    gather(i).wait()
    a_re, a_im = _unpack_bf16_pair(abuf[slot])
    z = _dot(w2_ref[...], jnp.concatenate([a_re, a_im], axis=0))

    @pl.when(i >= DMA_SLOTS)
    def _():
        scatter(i - DMA_SLOTS).wait()

    zbuf[slot] = z
    scatter(i).start()

    @pl.when(i == n_steps - 1)
    def _():
        for back in range(DMA_SLOTS):
            scatter(i - back).wait()


def _dft_tables(n1, n2, gdim):
    def cs(rows, cols, period):
        ang = 2.0 * np.pi * ((np.arange(rows)[:, None] * np.arange(cols)[None, :]) % period) / period
        return np.cos(ang), np.sin(ang)
    cc, sc = cs(gdim, gdim, gdim)
    w_ch = np.concatenate([cc, -sc], axis=1) / np.sqrt(gdim)
    c1, s1 = cs(n1, n1, n1)
    w1 = np.block([[c1, s1], [-s1, c1]]) / np.sqrt(n1)
    c2, s2 = cs(n2, n2, n2)
    w2 = np.concatenate([c2, s2], axis=1) / np.sqrt(n2)
    twc, tws = cs(n2, n1, n1 * n2)
    tw = np.zeros((n2, n1, LANES), np.float64)
    tw[:, :, 0] = twc
    tw[:, :, 1] = tws
    return [jnp.asarray(t, F32) for t in (w_ch, w1, w2, tw)]


def _fnet_mix(x, g, seq_len):
    T, D = x.shape
    n_seq = T // seq_len
    n1 = FFT_N1
    n2 = seq_len // n1
    assert n_seq * min(n1, n2) >= DMA_SLOTS
    gdim = D // FNET_GROUPS
    w_ch, w1, w2, tw = _dft_tables(n1, n2, gdim)
    w_ch, w1, w2 = (t.astype(BF16) for t in (w_ch, w1, w2))
    const2 = lambda shape: pl.BlockSpec(shape, lambda i: (0, 0))
    any_spec = pl.BlockSpec(memory_space=pl.ANY)
    slab = pl.BlockSpec((None, n1, D), lambda i: (i, 0, 0))
    a = pl.pallas_call(
        functools.partial(_fnet_stage1_kernel, n1=n1, n2=n2, gdim=gdim),
        grid=(n_seq * n2,),
        in_specs=[any_spec, const2((1, D)), const2(w_ch.shape), const2(w1.shape),
                  pl.BlockSpec((1, n1, LANES), lambda i: (i % n2, 0, 0))],
        out_specs=slab,
        out_shape=jax.ShapeDtypeStruct((n_seq * n2, n1, D), jnp.uint32),
        scratch_shapes=[pltpu.VMEM((DMA_SLOTS, n1, D), F32), pltpu.SemaphoreType.DMA((DMA_SLOTS,)),
                        pltpu.VMEM((2 * n1, D), F32)],
        compiler_params=_params(("arbitrary",), 24 * n1 * D * 4 + 16 * 1024 * 1024),
    )(x.reshape(T // n2, n2, D), g, w_ch, w1, tw)
    z = pl.pallas_call(
        functools.partial(_fnet_stage2_kernel, n1=n1, n2=n2),
        grid=(n_seq * n1,),
        in_specs=[any_spec, const2(w2.shape)],
        out_specs=any_spec,
        out_shape=jax.ShapeDtypeStruct((n_seq * n2, n1, D), F32),
        scratch_shapes=[pltpu.VMEM((DMA_SLOTS, n2, D), jnp.uint32), pltpu.VMEM((DMA_SLOTS, n2, D), F32),
                        pltpu.SemaphoreType.DMA((DMA_SLOTS,)), pltpu.SemaphoreType.DMA((DMA_SLOTS,))],
        compiler_params=_params(("arbitrary",), 24 * n2 * D * 4 + 16 * 1024 * 1024),
    )(a, w2)
    return z.reshape(T, D)


def _oproj_kernel(x_ref, a_ref, w_ref, o_ref):
    o_ref[...] = x_ref[...] + _dot(a_ref[...].astype(BF16), w_ref[...])


def _oproj_residual(x, a, w, tm):
    T, D = x.shape
    row = lambda i: (i, 0)
    return pl.pallas_call(
        _oproj_kernel,
        grid=(T // tm,),
        in_specs=[pl.BlockSpec((tm, D), row), pl.BlockSpec((tm, a.shape[1]), row), _const_spec(w.shape)],
        out_specs=pl.BlockSpec((tm, D), row),
        out_shape=jax.ShapeDtypeStruct((T, D), F32),
        compiler_params=_params(("parallel",), 2 * (2 * tm * D * 4 + tm * a.shape[1] * a.dtype.itemsize
                                                    + w.size * 2) + 2 * tm * D * 4),
    )(x, a, w)


def _mlp_kernel(x_ref, g_ref, wup_ref, wdn_ref, o_ref, xn_ref):
    @pl.when(pl.program_id(1) == 0)
    def _():
        x = x_ref[...]
        xn_ref[...] = _rms(x, g_ref[...]).astype(BF16)
        o_ref[...] = x

    hcol = jnp.square(jnp.maximum(_dot(xn_ref[...], wup_ref[...]), 0.0)).astype(BF16)
    o_ref[...] += _dot(hcol, wdn_ref[...])


def _mlp(x, g, w_up, w_down, layer, tm, tf):
    T, D = x.shape
    dff = w_up.shape[2]
    return pl.pallas_call(
        _mlp_kernel,
        grid=(T // tm, dff // tf),
        in_specs=[pl.BlockSpec((tm, D), lambda i, f: (i, 0)), pl.BlockSpec((1, D), lambda i, f: (0, 0)),
                  pl.BlockSpec((None, D, tf), lambda i, f: (layer, 0, f)),
                  pl.BlockSpec((None, tf, D), lambda i, f: (layer, f, 0))],
        out_specs=pl.BlockSpec((tm, D), lambda i, f: (i, 0)),
        out_shape=jax.ShapeDtypeStruct((T, D), F32),
        scratch_shapes=[pltpu.VMEM((tm, D), BF16)],
        compiler_params=_params(("parallel", "arbitrary"),
                                2 * (2 * tm * D * 4 + 2 * D * tf * 2) + tm * D * 2 + 3 * tm * tf * 4),
    )(x, g, w_up, w_down)


def _dup_rope_cols(w):
    return jnp.concatenate([w, w], axis=-1)


def _pad_head_vec(g):
    return jnp.concatenate([g[:QK_NOPE_DIM], _dup_rope_cols(g[QK_NOPE_DIM:])])[None, :].astype(F32)


def _rope_tables(n_seq, seq_len):
    inv_freq = ROPE_THETA ** (-jnp.arange(0, QK_ROPE_DIM, 2, dtype=F32) / QK_ROPE_DIM)
    ang = jnp.arange(seq_len, dtype=F32)[:, None] * inv_freq[None, :]
    c, s = jnp.cos(ang), jnp.sin(ang)
    z = jnp.zeros_like(c)
    tile = lambda t: jnp.tile(t, (n_seq, 1))
    return (tile(c).T, tile(s).T,
            tile(jnp.concatenate([c, c, z, z], axis=1)), tile(jnp.concatenate([-s, s, z, z], axis=1)))


def _pick_tile(n, target):
    t = min(n, target)
    while n % t:
        t //= 2
    return t


def _trunk(x, seq_len, p):
    T = x.shape[0]
    rope = _rope_tables(T // seq_len, seq_len)
    tm = _pick_tile(T, 512)
    tf = _pick_tile(p["w_up"].shape[2], 1024)
    ts = _pick_tile(seq_len, 512)
    for i in range(p["w_up"].shape[0]):
        j = i // 2
        if i % 2 == 0:
            qt, k, vt = _mla_project(x, p["attn_norm"][j], p["w_q_a"][j], p["q_a_norm"][j], p["w_q_b"][j],
                                     p["q_norm"][j], p["w_kv_a"][j], p["kv_a_norm"][j], p["w_k_b"][j], p["w_v_b_t"][j],
                                     p["k_norm"][j], rope, tm)
            a = _attention(qt, k, vt, seq_len, ts, ts)
        else:
            a = _fnet_mix(x, p["fnet_norm"][j], seq_len)
        x = _oproj_residual(x, a, p["w_mix_o"][i], tm)
        x = _mlp(x, p["mlp_norm"][i], p["w_up"], p["w_down"], i, tm, tf)
    return x


def kernel(x_prompt, x_sample, attn_norm, w_q_a, q_a_norm, w_q_b, w_kv_a, kv_a_norm, w_kv_b, q_norm, k_norm,
           w_attn_o, fnet_norm, w_fnet_o, mlp_norm, w_up, w_down):
    depth = mlp_norm.shape[0]
    n_attn = attn_norm.shape[0]
    kv_rank = kv_a_norm.shape[1]
    q_rank = q_a_norm.shape[1]
    row2 = lambda v: v[None, :].astype(F32)
    wqbt = jnp.swapaxes(w_q_b.reshape(n_attn, q_rank, N_HEADS, QK_HEAD_DIM), 1, 3)
    wqbt = jnp.pad(jnp.swapaxes(wqbt, 1, 2), ((0, 0), (0, 0), (0, HEAD_PAD - QK_HEAD_DIM), (0, 0)))
    wqbt = wqbt.reshape(n_attn, N_HEADS * HEAD_PAD, q_rank)
    wkvb = w_kv_b.reshape(n_attn, kv_rank, N_HEADS, QK_NOPE_DIM + V_HEAD_DIM)
    wk = wkvb[..., :QK_NOPE_DIM].reshape(n_attn, kv_rank, N_HEADS * QK_NOPE_DIM)
    wvt = jnp.swapaxes(wkvb[..., QK_NOPE_DIM:].reshape(n_attn, kv_rank, N_HEADS * V_HEAD_DIM), 1, 2)
    q_gain = lambda g: jnp.pad(g, (0, HEAD_PAD - QK_HEAD_DIM))[:, None].astype(F32)
    wkva = jnp.concatenate([w_kv_a[..., :kv_rank], _dup_rope_cols(w_kv_a[..., kv_rank:])], axis=-1)
    layers = lambda w: [w[j].astype(BF16) for j in range(w.shape[0])]
    p = {
        "attn_norm": [row2(g) for g in attn_norm], "w_q_a": layers(w_q_a),
        "q_a_norm": [row2(g) for g in q_a_norm], "w_q_b": layers(wqbt),
        "q_norm": [q_gain(g) for g in q_norm], "w_kv_a": layers(wkva),
        "kv_a_norm": [row2(g) for g in kv_a_norm], "w_k_b": layers(wk), "w_v_b_t": layers(wvt),
        "k_norm": [_pad_head_vec(g) for g in k_norm], "fnet_norm": [row2(g) for g in fnet_norm],
        "w_mix_o": [(w_attn_o if i % 2 == 0 else w_fnet_o)[i // 2].astype(BF16) for i in range(depth)],
        "mlp_norm": [row2(g) for g in mlp_norm], "w_up": w_up.astype(BF16), "w_down": w_down.astype(BF16),
    }
    outs = []
    for xg in (x_prompt, x_sample):
        b, s, d = xg.shape
        outs.append(_trunk(xg.reshape(b * s, d), s, p).reshape(b, s, d))
    return tuple(outs)
```

```python
import functools
import math

import numpy as np
import jax
import jax.numpy as jnp
from jax import lax
from jax.experimental import pallas as pl
from jax.experimental.pallas import tpu as pltpu

F32 = jnp.float32
BF16 = jnp.bfloat16

N_HEADS = 16
QK_NOPE_DIM = 128
QK_ROPE_DIM = 64
QK_HEAD_DIM = QK_NOPE_DIM + QK_ROPE_DIM
V_HEAD_DIM = 128
HEAD_PAD = 256
BF16_ROWS = 16
V_ROWS = V_HEAD_DIM + BF16_ROWS
ROPE_THETA = 10000.0
FNET_GROUPS = 8
EPS = 1e-6
FFT_N1 = 128
LANES = 128
DMA_SLOTS = 8
SCORE_PAD_ROWS = 8
VMEM_CAP = 60 * 1024 * 1024


def _vmem_limit(nbytes):
    return int(min(max(nbytes, 16 * 1024 * 1024), VMEM_CAP))


def _params(sem, nbytes):
    return pltpu.CompilerParams(dimension_semantics=sem, vmem_limit_bytes=_vmem_limit(nbytes))


def _rms(xf, g):
    return xf * lax.rsqrt(jnp.mean(xf * xf, axis=-1, keepdims=True) + EPS) * g


def _dot(a, b):
    return jnp.dot(a, b, preferred_element_type=F32)


def _rope_dup(pe, cos_t, sin_t):
    return pe * cos_t + pltpu.roll(pe, 32, axis=1) * sin_t


def _qproj_kernel(x_ref, ga_ref, wqa_ref, gqa_ref, wqbt_ref, gq_ref, cos_ref, sin_ref, q_ref, *, qscale):
    h = _rms(x_ref[...], ga_ref[...]).astype(BF16)
    cq_t = _rms(_dot(h, wqa_ref[...]), gqa_ref[...]).T.astype(BF16)
    cos_t = cos_ref[...]
    sin_t = sin_ref[...]
    half = QK_ROPE_DIM // 2
    for hh in range(N_HEADS):
        qh = _dot(wqbt_ref[hh * HEAD_PAD:(hh + 1) * HEAD_PAD, :], cq_t)
        ss = jnp.sum(qh * qh, axis=0, keepdims=True)
        qh = qh * (lax.rsqrt(ss * (1.0 / QK_HEAD_DIM) + EPS) * qscale) * gq_ref[...]
        x1 = qh[QK_NOPE_DIM:QK_NOPE_DIM + half]
        x2 = qh[QK_NOPE_DIM + half:QK_HEAD_DIM]
        q_ref[hh, :QK_NOPE_DIM, :] = qh[:QK_NOPE_DIM].astype(BF16)
        q_ref[hh, QK_NOPE_DIM:QK_NOPE_DIM + half, :] = (x1 * cos_t - x2 * sin_t).astype(BF16)
        q_ref[hh, QK_NOPE_DIM + half:QK_HEAD_DIM, :] = (x2 * cos_t + x1 * sin_t).astype(BF16)
        q_ref[hh, QK_HEAD_DIM:, :] = jnp.zeros((HEAD_PAD - QK_HEAD_DIM, qh.shape[1]), BF16)


def _kvproj_kernel(x_ref, ga_ref, wkva_ref, gkva_ref, wk_ref, wvt_ref, gk_ref, cos_ref, sin_ref, k_ref, v_ref,
                   *, kv_rank):
    h = _rms(x_ref[...], ga_ref[...]).astype(BF16)
    ckv = _dot(h, wkva_ref[...])
    c = _rms(ckv[:, :kv_rank], gkva_ref[...])
    pe = ckv[:, kv_rank:]
    pe_ss = 0.5 * jnp.sum(pe * pe, axis=-1, keepdims=True)
    g_nope = gk_ref[:, :QK_NOPE_DIM]
    r = _rope_dup(pe * gk_ref[:, QK_NOPE_DIM:], cos_ref[...], sin_ref[...])
    vt = _dot(wvt_ref[...], c.T.astype(BF16))
    ones_row = (lax.broadcasted_iota(jnp.int32, (BF16_ROWS, vt.shape[1]), 0) == 0).astype(BF16)
    c = c.astype(BF16)
    for pair in range(N_HEADS // 2):
        k_pair = _dot(c, wk_ref[:, pair * 2 * QK_NOPE_DIM:(pair + 1) * 2 * QK_NOPE_DIM])
        for hh in (2 * pair, 2 * pair + 1):
            k_nope = k_pair[:, (hh % 2) * QK_NOPE_DIM:(hh % 2 + 1) * QK_NOPE_DIM]
            ss = jnp.sum(k_nope * k_nope, axis=-1, keepdims=True) + pe_ss
            inv = lax.rsqrt(ss * (1.0 / QK_HEAD_DIM) + EPS)
            k_ref[hh, :, :QK_NOPE_DIM] = (k_nope * inv * g_nope).astype(BF16)
            k_ref[hh, :, QK_NOPE_DIM:] = (r * inv).astype(BF16)
            v_ref[hh, :V_HEAD_DIM, :] = vt[hh * V_HEAD_DIM:(hh + 1) * V_HEAD_DIM].astype(BF16)
            v_ref[hh, V_HEAD_DIM:, :] = ones_row


def _const_spec(shape):
    return pl.BlockSpec(shape, lambda i: (0,) * len(shape))


def _mla_project(x, ga, wqa, gqa, wqbt, gq, wkva, gkva, wk, wvt, gk, rope, tm):
    T, D = x.shape
    q_rank = wqa.shape[1]
    kv_rank = gkva.shape[1]
    qscale = QK_HEAD_DIM ** -0.5 * math.log2(math.e)
    row = lambda i: (i, 0)
    head_rows = lambda i: (0, i, 0)
    head_cols = lambda i: (0, 0, i)
    tab = pl.BlockSpec((tm, LANES), row)
    tab_t = pl.BlockSpec((QK_ROPE_DIM // 2, tm), lambda i: (0, i))
    cos_t, sin_t, cos_dup, sin_dup = rope
    qt = pl.pallas_call(
        functools.partial(_qproj_kernel, qscale=qscale),
        grid=(T // tm,),
        in_specs=[pl.BlockSpec((tm, D), row), _const_spec((1, D)), _const_spec(wqa.shape),
                  _const_spec((1, q_rank)), _const_spec(wqbt.shape), _const_spec((HEAD_PAD, 1)), tab_t, tab_t],
        out_specs=pl.BlockSpec((N_HEADS, HEAD_PAD, tm), head_cols),
        out_shape=jax.ShapeDtypeStruct((N_HEADS, HEAD_PAD, T), BF16),
        compiler_params=_params(("parallel",), 2 * (tm * D * 4 + wqa.size * 2 + wqbt.size * 2
                                                    + N_HEADS * tm * HEAD_PAD * 2) + 8 * tm * D * 4),
    )(x, ga, wqa, gqa, wqbt, gq, cos_t, sin_t)
    k, vt = pl.pallas_call(
        functools.partial(_kvproj_kernel, kv_rank=kv_rank),
        grid=(T // tm,),
        in_specs=[pl.BlockSpec((tm, D), row), _const_spec((1, D)), _const_spec(wkva.shape),
                  _const_spec((1, kv_rank)), _const_spec(wk.shape), _const_spec(wvt.shape),
                  _const_spec((1, HEAD_PAD)), tab, tab],
        out_specs=[pl.BlockSpec((N_HEADS, tm, HEAD_PAD), head_rows),
                   pl.BlockSpec((N_HEADS, V_ROWS, tm), head_cols)],
        out_shape=[jax.ShapeDtypeStruct((N_HEADS, T, HEAD_PAD), BF16),
                   jax.ShapeDtypeStruct((N_HEADS, V_ROWS, T), BF16)],
        compiler_params=_params(("parallel",), 2 * (tm * D * 4 + wkva.size * 2 + wk.size * 2 + wvt.size * 2
                                                    + N_HEADS * tm * (HEAD_PAD + V_ROWS) * 2)
                                + 8 * tm * D * 4),
    )(x, ga, wkva, gkva, wk, wvt, gk, cos_dup, sin_dup)
    return qt, k, vt


def _flash_kernel(qt_ref, k_ref, vt_ref, o_ref, s_ref, *, tk, chunks_per_step):
    qt = qt_ref[0]
    tq = qt.shape[1]
    n_chunks = k_ref.shape[1] // tk

    def scores(j, slot):
        s = _dot(k_ref[0, pl.ds(pl.multiple_of(j * tk, tk), tk), :], qt)
        s_ref[slot, :tk, :tq] = s
        return jnp.max(s, axis=0, keepdims=True)

    def absorb(j, slot, s_max, m, acc):
        vt = vt_ref[0, :, pl.ds(pl.multiple_of(j * tk, tk), tk)]
        m_new = jnp.maximum(m, s_max)
        p = jnp.exp2(s_ref[slot, :tk, :tq] - m_new)
        acc = jnp.exp2(m - m_new) * acc + _dot(vt, p.astype(BF16))
        return m_new, acc

    def body(step, carry):
        s_max, m, acc = carry
        for c in range(chunks_per_step):
            j = step * chunks_per_step + c
            s_max_next = scores(jnp.minimum(j + 1, n_chunks - 1), (c + 1) % 2)
            m, acc = absorb(j, c % 2, s_max, m, acc)
            s_max = s_max_next
        return s_max, m, acc

    m0 = jnp.full((1, tq), -jnp.inf, F32)
    acc0 = jnp.zeros((V_ROWS, tq), F32)
    _, _, acc = lax.fori_loop(0, n_chunks // chunks_per_step, body, (scores(0, 0), m0, acc0))
    o_ref[...] = (acc[:V_HEAD_DIM] / acc[V_HEAD_DIM:V_HEAD_DIM + 1]).T.astype(BF16)


def _attention(qt, k, vt, seq_len, tq, tk):
    n_seq = k.shape[1] // seq_len
    nq = seq_len // tq
    n_chunks = seq_len // tk
    chunks_per_step = math.gcd(n_chunks, 16)
    if chunks_per_step == n_chunks and n_chunks % 4 == 0:
        chunks_per_step //= 2
    assert chunks_per_step % 2 == 0
    return pl.pallas_call(
        functools.partial(_flash_kernel, tk=tk, chunks_per_step=chunks_per_step),
        grid=(n_seq, N_HEADS, nq),
        in_specs=[pl.BlockSpec((1, HEAD_PAD, tq), lambda b, h, i: (h, 0, b * nq + i)),
                  pl.BlockSpec((1, seq_len, HEAD_PAD), lambda b, h, i: (h, b, 0)),
                  pl.BlockSpec((1, V_ROWS, seq_len), lambda b, h, i: (h, 0, b))],
        out_specs=pl.BlockSpec((tq, V_HEAD_DIM), lambda b, h, i: (b * nq + i, h)),
        out_shape=jax.ShapeDtypeStruct((n_seq * seq_len, N_HEADS * V_HEAD_DIM), BF16),
        scratch_shapes=[pltpu.VMEM((2, tk + SCORE_PAD_ROWS, tq + LANES), F32)],
        compiler_params=_params(("parallel", "parallel", "arbitrary"),
                                2 * seq_len * (HEAD_PAD + V_ROWS) * 2 + 8 * tq * tk * 4
                                + 4 * tq * HEAD_PAD * 2),
    )(qt, k, vt)


def _pack_bf16_pair(hi, lo):
    bits = lambda v: lax.bitcast_convert_type(v.astype(BF16).astype(F32), jnp.uint32)
    return bits(hi) | (bits(lo) >> 16)


def _unpack_bf16_pair(w):
    hi = lax.bitcast_convert_type(w & jnp.uint32(0xFFFF0000), F32)
    lo = lax.bitcast_convert_type(w << 16, F32)
    return hi.astype(BF16), lo.astype(BF16)


def _fnet_stage1_kernel(x_hbm, g_ref, wc_ref, w1_ref, tw_ref, a_ref, xbuf, sem, u_ref, *, n1, n2, gdim):
    i = pl.program_id(0)
    slot = i % DMA_SLOTS

    def gather(step):
        rows = pl.ds((step // n2) * n1, n1)
        to_slot = step % DMA_SLOTS
        return pltpu.make_async_copy(x_hbm.at[rows, step % n2, :], xbuf.at[to_slot], sem.at[to_slot])

    @pl.when(i == 0)
    def _():
        for ahead in range(DMA_SLOTS - 1):
            gather(ahead).start()

    @pl.when(i + DMA_SLOTS - 1 < pl.num_programs(0))
    def _():
        gather(i + DMA_SLOTS - 1).start()

    gather(i).wait()
    h = _rms(xbuf[slot], g_ref[...]).astype(BF16)
    wc = wc_ref[...]
    for gi in range(FNET_GROUPS):
        cols = slice(gi * gdim, (gi + 1) * gdim)
        u = _dot(h[:, cols], wc)
        u_ref[:n1, cols] = u[:, :gdim]
        u_ref[n1:, cols] = u[:, gdim:]
    a = _dot(w1_ref[...], u_ref[...].astype(BF16))
    a_re = a[:n1]
    a_im = a[n1:]
    tw_c = tw_ref[0, :, 0:1]
    tw_s = tw_ref[0, :, 1:2]
    a_ref[...] = _pack_bf16_pair(a_re * tw_c + a_im * tw_s, a_im * tw_c - a_re * tw_s)


def _fnet_stage2_kernel(a_hbm, w2_ref, z_hbm, abuf, zbuf, in_sem, out_sem, *, n1, n2):
    i = pl.program_id(0)
    n_steps = pl.num_programs(0)
    slot = i % DMA_SLOTS

    def gather(step):
        rows = pl.ds((step // n1) * n2, n2)
        to_slot = step % DMA_SLOTS
        return pltpu.make_async_copy(a_hbm.at[rows, step % n1, :], abuf.at[to_slot], in_sem.at[to_slot])

    def scatter(step):
        rows = pl.ds((step // n1) * n2, n2)
        from_slot = step % DMA_SLOTS
        return pltpu.make_async_copy(zbuf.at[from_slot], z_hbm.at[rows, step % n1, :], out_sem.at[from_slot])

    @pl.when(i == 0)
    def _():
        for ahead in range(DMA_SLOTS - 1):
            gather(ahead).start()

    @pl.when(i + DMA_SLOTS - 1 < n_steps)
    def _():
        gather(i + DMA_SLOTS - 1).start()

    gather(i).wait()
    a_re, a_im = _unpack_bf16_pair(abuf[slot])
    z = _dot(w2_ref[...], jnp.concatenate([a_re, a_im], axis=0))

    @pl.when(i >= DMA_SLOTS)
    def _():
        scatter(i - DMA_SLOTS).wait()

    zbuf[slot] = z
    scatter(i).start()

    @pl.when(i == n_steps - 1)
    def _():
        for back in range(DMA_SLOTS):
            scatter(i - back).wait()


def _dft_tables(n1, n2, gdim):
    def cs(rows, cols, period):
        ang = 2.0 * np.pi * ((np.arange(rows)[:, None] * np.arange(cols)[None, :]) % period) / period
        return np.cos(ang), np.sin(ang)
    cc, sc = cs(gdim, gdim, gdim)
    w_ch = np.concatenate([cc, -sc], axis=1) / np.sqrt(gdim)
    c1, s1 = cs(n1, n1, n1)
    w1 = np.block([[c1, s1], [-s1, c1]]) / np.sqrt(n1)
    c2, s2 = cs(n2, n2, n2)
    w2 = np.concatenate([c2, s2], axis=1) / np.sqrt(n2)
    twc, tws = cs(n2, n1, n1 * n2)
    tw = np.zeros((n2, n1, LANES), np.float64)
    tw[:, :, 0] = twc
    tw[:, :, 1] = tws
    return [jnp.asarray(t, F32) for t in (w_ch, w1, w2, tw)]


def _fnet_mix(x, g, seq_len):
    T, D = x.shape
    n_seq = T // seq_len
    n1 = FFT_N1
    n2 = seq_len // n1
    assert n_seq * min(n1, n2) >= DMA_SLOTS
    gdim = D // FNET_GROUPS
    w_ch, w1, w2, tw = _dft_tables(n1, n2, gdim)
    w_ch, w1, w2 = (t.astype(BF16) for t in (w_ch, w1, w2))
    const2 = lambda shape: pl.BlockSpec(shape, lambda i: (0, 0))
    any_spec = pl.BlockSpec(memory_space=pl.ANY)
    slab = pl.BlockSpec((None, n1, D), lambda i: (i, 0, 0))
    a = pl.pallas_call(
        functools.partial(_fnet_stage1_kernel, n1=n1, n2=n2, gdim=gdim),
        grid=(n_seq * n2,),
        in_specs=[any_spec, const2((1, D)), const2(w_ch.shape), const2(w1.shape),
                  pl.BlockSpec((1, n1, LANES), lambda i: (i % n2, 0, 0))],
        out_specs=slab,
        out_shape=jax.ShapeDtypeStruct((n_seq * n2, n1, D), jnp.uint32),
        scratch_shapes=[pltpu.VMEM((DMA_SLOTS, n1, D), F32), pltpu.SemaphoreType.DMA((DMA_SLOTS,)),
                        pltpu.VMEM((2 * n1, D), F32)],
        compiler_params=_params(("arbitrary",), 24 * n1 * D * 4 + 16 * 1024 * 1024),
    )(x.reshape(T // n2, n2, D), g, w_ch, w1, tw)
    z = pl.pallas_call(
        functools.partial(_fnet_stage2_kernel, n1=n1, n2=n2),
        grid=(n_seq * n1,),
        in_specs=[any_spec, const2(w2.shape)],
        out_specs=any_spec,
        out_shape=jax.ShapeDtypeStruct((n_seq * n2, n1, D), F32),
        scratch_shapes=[pltpu.VMEM((DMA_SLOTS, n2, D), jnp.uint32), pltpu.VMEM((DMA_SLOTS, n2, D), F32),
                        pltpu.SemaphoreType.DMA((DMA_SLOTS,)), pltpu.SemaphoreType.DMA((DMA_SLOTS,))],
        compiler_params=_params(("arbitrary",), 24 * n2 * D * 4 + 16 * 1024 * 1024),
    )(a, w2)
    return z.reshape(T, D)


def _oproj_kernel(x_ref, a_ref, w_ref, o_ref):
    o_ref[...] = x_ref[...] + _dot(a_ref[...].astype(BF16), w_ref[...])


def _oproj_residual(x, a, w, tm):
    T, D = x.shape
    row = lambda i: (i, 0)
    return pl.pallas_call(
        _oproj_kernel,
        grid=(T // tm,),
        in_specs=[pl.BlockSpec((tm, D), row), pl.BlockSpec((tm, a.shape[1]), row), _const_spec(w.shape)],
        out_specs=pl.BlockSpec((tm, D), row),
        out_shape=jax.ShapeDtypeStruct((T, D), F32),
        compiler_params=_params(("parallel",), 2 * (2 * tm * D * 4 + tm * a.shape[1] * a.dtype.itemsize
                                                    + w.size * 2) + 2 * tm * D * 4),
    )(x, a, w)


def _mlp_kernel(x_ref, g_ref, wup_ref, wdn_ref, o_ref, xn_ref):
    @pl.when(pl.program_id(1) == 0)
    def _():
        x = x_ref[...]
        xn_ref[...] = _rms(x, g_ref[...]).astype(BF16)
        o_ref[...] = x

    hcol = jnp.square(jnp.maximum(_dot(xn_ref[...], wup_ref[...]), 0.0)).astype(BF16)
    o_ref[...] += _dot(hcol, wdn_ref[...])


def _mlp(x, g, w_up, w_down, layer, tm, tf):
    T, D = x.shape
    dff = w_up.shape[2]
    return pl.pallas_call(
        _mlp_kernel,
        grid=(T // tm, dff // tf),
        in_specs=[pl.BlockSpec((tm, D), lambda i, f: (i, 0)), pl.BlockSpec((1, D), lambda i, f: (0, 0)),
                  pl.BlockSpec((None, D, tf), lambda i, f: (layer, 0, f)),
                  pl.BlockSpec((None, tf, D), lambda i, f: (layer, f, 0))],
        out_specs=pl.BlockSpec((tm, D), lambda i, f: (i, 0)),
        out_shape=jax.ShapeDtypeStruct((T, D), F32),
        scratch_shapes=[pltpu.VMEM((tm, D), BF16)],
        compiler_params=_params(("parallel", "arbitrary"),
                                2 * (2 * tm * D * 4 + 2 * D * tf * 2) + tm * D * 2 + 3 * tm * tf * 4),
    )(x, g, w_up, w_down)


def _dup_rope_cols(w):
    return jnp.concatenate([w, w], axis=-1)


def _pad_head_vec(g):
    return jnp.concatenate([g[:QK_NOPE_DIM], _dup_rope_cols(g[QK_NOPE_DIM:])])[None, :].astype(F32)


def _rope_tables(n_seq, seq_len):
    inv_freq = ROPE_THETA ** (-jnp.arange(0, QK_ROPE_DIM, 2, dtype=F32) / QK_ROPE_DIM)
    ang = jnp.arange(seq_len, dtype=F32)[:, None] * inv_freq[None, :]
    c, s = jnp.cos(ang), jnp.sin(ang)
    z = jnp.zeros_like(c)
    tile = lambda t: jnp.tile(t, (n_seq, 1))
    return (tile(c).T, tile(s).T,
            tile(jnp.concatenate([c, c, z, z], axis=1)), tile(jnp.concatenate([-s, s, z, z], axis=1)))


def _pick_tile(n, target):
    t = min(n, target)
    while n % t:
        t //= 2
    return t


def _trunk(x, seq_len, p):
    T = x.shape[0]
    rope = _rope_tables(T // seq_len, seq_len)
    tm = _pick_tile(T, 512)
    tf = _pick_tile(p["w_up"].shape[2], 1024)
    ts = _pick_tile(seq_len, 512)
    for i in range(p["w_up"].shape[0]):
        j = i // 2
        if i % 2 == 0:
            qt, k, vt = _mla_project(x, p["attn_norm"][j], p["w_q_a"][j], p["q_a_norm"][j], p["w_q_b"][j],
                                     p["q_norm"][j], p["w_kv_a"][j], p["kv_a_norm"][j], p["w_k_b"][j], p["w_v_b_t"][j],
                                     p["k_norm"][j], rope, tm)
            a = _attention(qt, k, vt, seq_len, ts, ts)
        else:
            a = _fnet_mix(x, p["fnet_norm"][j], seq_len)
        x = _oproj_residual(x, a, p["w_mix_o"][i], tm)
        x = _mlp(x, p["mlp_norm"][i], p["w_up"], p["w_down"], i, tm, tf)
    return x


def kernel(x_prompt, x_sample, attn_norm, w_q_a, q_a_norm, w_q_b, w_kv_a, kv_a_norm, w_kv_b, q_norm, k_norm,
           w_attn_o, fnet_norm, w_fnet_o, mlp_norm, w_up, w_down):
    depth = mlp_norm.shape[0]
    n_attn = attn_norm.shape[0]
    kv_rank = kv_a_norm.shape[1]
    q_rank = q_a_norm.shape[1]
    row2 = lambda v: v[None, :].astype(F32)
    wqbt = jnp.swapaxes(w_q_b.reshape(n_attn, q_rank, N_HEADS, QK_HEAD_DIM), 1, 3)
    wqbt = jnp.pad(jnp.swapaxes(wqbt, 1, 2), ((0, 0), (0, 0), (0, HEAD_PAD - QK_HEAD_DIM), (0, 0)))
    wqbt = wqbt.reshape(n_attn, N_HEADS * HEAD_PAD, q_rank)
    wkvb = w_kv_b.reshape(n_attn, kv_rank, N_HEADS, QK_NOPE_DIM + V_HEAD_DIM)
    wk = wkvb[..., :QK_NOPE_DIM].reshape(n_attn, kv_rank, N_HEADS * QK_NOPE_DIM)
    wvt = jnp.swapaxes(wkvb[..., QK_NOPE_DIM:].reshape(n_attn, kv_rank, N_HEADS * V_HEAD_DIM), 1, 2)
    q_gain = lambda g: jnp.pad(g, (0, HEAD_PAD - QK_HEAD_DIM))[:, None].astype(F32)
    wkva = jnp.concatenate([w_kv_a[..., :kv_rank], _dup_rope_cols(w_kv_a[..., kv_rank:])], axis=-1)
    layers = lambda w: [w[j].astype(BF16) for j in range(w.shape[0])]
    p = {
        "attn_norm": [row2(g) for g in attn_norm], "w_q_a": layers(w_q_a),
        "q_a_norm": [row2(g) for g in q_a_norm], "w_q_b": layers(wqbt),
        "q_norm": [q_gain(g) for g in q_norm], "w_kv_a": layers(wkva),
        "kv_a_norm": [row2(g) for g in kv_a_norm], "w_k_b": layers(wk), "w_v_b_t": layers(wvt),
        "k_norm": [_pad_head_vec(g) for g in k_norm], "fnet_norm": [row2(g) for g in fnet_norm],
        "w_mix_o": [(w_attn_o if i % 2 == 0 else w_fnet_o)[i // 2].astype(BF16) for i in range(depth)],
        "mlp_norm": [row2(g) for g in mlp_norm], "w_up": w_up.astype(BF16), "w_down": w_down.astype(BF16),
    }
    outs = []
    for xg in (x_prompt, x_sample):
        b, s, d = xg.shape
        outs.append(_trunk(xg.reshape(b * s, d), s, p).reshape(b, s, d))
    return tuple(outs)
```

```python
import functools
import math

import numpy as np
import jax
import jax.numpy as jnp
from jax import lax
from jax.experimental import pallas as pl
from jax.experimental.pallas import tpu as pltpu

F32 = jnp.float32
BF16 = jnp.bfloat16

N_HEADS = 16
QK_NOPE_DIM = 128
QK_ROPE_DIM = 64
QK_HEAD_DIM = QK_NOPE_DIM + QK_ROPE_DIM
V_HEAD_DIM = 128
HEAD_PAD = 256
BF16_ROWS = 16
V_ROWS = V_HEAD_DIM + BF16_ROWS
ROPE_THETA = 10000.0
FNET_GROUPS = 8
EPS = 1e-6
FFT_N1 = 128
LANES = 128
DMA_SLOTS = 8
SCORE_PAD_ROWS = 256
VMEM_CAP = 60 * 1024 * 1024


def _vmem_limit(nbytes):
    return int(min(max(nbytes, 16 * 1024 * 1024), VMEM_CAP))


def _params(sem, nbytes):
    return pltpu.CompilerParams(dimension_semantics=sem, vmem_limit_bytes=_vmem_limit(nbytes))


def _rms(xf, g):
    return xf * lax.rsqrt(jnp.mean(xf * xf, axis=-1, keepdims=True) + EPS) * g


def _dot(a, b):
    return jnp.dot(a, b, preferred_element_type=F32)


def _rope_dup(pe, cos_t, sin_t):
    return pe * cos_t + pltpu.roll(pe, 32, axis=1) * sin_t


def _qproj_kernel(x_ref, ga_ref, wqa_ref, gqa_ref, wqbt_ref, gq_ref, cos_ref, sin_ref, q_ref, *, qscale):
    h = _rms(x_ref[...], ga_ref[...]).astype(BF16)
    cq_t = _rms(_dot(h, wqa_ref[...]), gqa_ref[...]).T.astype(BF16)
    cos_t = cos_ref[...]
    sin_t = sin_ref[...]
    half = QK_ROPE_DIM // 2
    for hh in range(N_HEADS):
        qh = _dot(wqbt_ref[hh * HEAD_PAD:(hh + 1) * HEAD_PAD, :], cq_t)
        ss = jnp.sum(qh * qh, axis=0, keepdims=True)
        qh = qh * (lax.rsqrt(ss * (1.0 / QK_HEAD_DIM) + EPS) * qscale) * gq_ref[...]
        x1 = qh[QK_NOPE_DIM:QK_NOPE_DIM + half]
        x2 = qh[QK_NOPE_DIM + half:QK_HEAD_DIM]
        q_ref[hh, :QK_NOPE_DIM, :] = qh[:QK_NOPE_DIM].astype(BF16)
        q_ref[hh, QK_NOPE_DIM:QK_NOPE_DIM + half, :] = (x1 * cos_t - x2 * sin_t).astype(BF16)
        q_ref[hh, QK_NOPE_DIM + half:QK_HEAD_DIM, :] = (x2 * cos_t + x1 * sin_t).astype(BF16)
        q_ref[hh, QK_HEAD_DIM:, :] = jnp.zeros((HEAD_PAD - QK_HEAD_DIM, qh.shape[1]), BF16)


def _kvproj_kernel(x_ref, ga_ref, wkva_ref, gkva_ref, wk_ref, wvt_ref, gk_ref, cos_ref, sin_ref, k_ref, v_ref,
                   *, kv_rank):
    h = _rms(x_ref[...], ga_ref[...]).astype(BF16)
    ckv = _dot(h, wkva_ref[...])
    c = _rms(ckv[:, :kv_rank], gkva_ref[...])
    pe = ckv[:, kv_rank:]
    pe_ss = 0.5 * jnp.sum(pe * pe, axis=-1, keepdims=True)
    g_nope = gk_ref[:, :QK_NOPE_DIM]
    r = _rope_dup(pe * gk_ref[:, QK_NOPE_DIM:], cos_ref[...], sin_ref[...])
    vt = _dot(wvt_ref[...], c.T.astype(BF16))
    ones_row = (lax.broadcasted_iota(jnp.int32, (BF16_ROWS, vt.shape[1]), 0) == 0).astype(BF16)
    c = c.astype(BF16)
    for pair in range(N_HEADS // 2):
        k_pair = _dot(c, wk_ref[:, pair * 2 * QK_NOPE_DIM:(pair + 1) * 2 * QK_NOPE_DIM])
        for hh in (2 * pair, 2 * pair + 1):
            k_nope = k_pair[:, (hh % 2) * QK_NOPE_DIM:(hh % 2 + 1) * QK_NOPE_DIM]
            ss = jnp.sum(k_nope * k_nope, axis=-1, keepdims=True) + pe_ss
            inv = lax.rsqrt(ss * (1.0 / QK_HEAD_DIM) + EPS)
            k_ref[hh, :, :QK_NOPE_DIM] = (k_nope * inv * g_nope).astype(BF16)
            k_ref[hh, :, QK_NOPE_DIM:] = (r * inv).astype(BF16)
            v_ref[hh, :V_HEAD_DIM, :] = vt[hh * V_HEAD_DIM:(hh + 1) * V_HEAD_DIM].astype(BF16)
            v_ref[hh, V_HEAD_DIM:, :] = ones_row


def _const_spec(shape):
    return pl.BlockSpec(shape, lambda i: (0,) * len(shape))


def _mla_project(x, ga, wqa, gqa, wqbt, gq, wkva, gkva, wk, wvt, gk, rope, tm):
    T, D = x.shape
    q_rank = wqa.shape[1]
    kv_rank = gkva.shape[1]
    qscale = QK_HEAD_DIM ** -0.5 * math.log2(math.e)
    row = lambda i: (i, 0)
    head_rows = lambda i: (0, i, 0)
    head_cols = lambda i: (0, 0, i)
    tab = pl.BlockSpec((tm, LANES), row)
    tab_t = pl.BlockSpec((QK_ROPE_DIM // 2, tm), lambda i: (0, i))
    cos_t, sin_t, cos_dup, sin_dup = rope
    qt = pl.pallas_call(
        functools.partial(_qproj_kernel, qscale=qscale),
        grid=(T // tm,),
        in_specs=[pl.BlockSpec((tm, D), row), _const_spec((1, D)), _const_spec(wqa.shape),
                  _const_spec((1, q_rank)), _const_spec(wqbt.shape), _const_spec((HEAD_PAD, 1)), tab_t, tab_t],
        out_specs=pl.BlockSpec((N_HEADS, HEAD_PAD, tm), head_cols),
        out_shape=jax.ShapeDtypeStruct((N_HEADS, HEAD_PAD, T), BF16),
        compiler_params=_params(("parallel",), 2 * (tm * D * 4 + wqa.size * 2 + wqbt.size * 2
                                                    + N_HEADS * tm * HEAD_PAD * 2) + 8 * tm * D * 4),
    )(x, ga, wqa, gqa, wqbt, gq, cos_t, sin_t)
    k, vt = pl.pallas_call(
        functools.partial(_kvproj_kernel, kv_rank=kv_rank),
        grid=(T // tm,),
        in_specs=[pl.BlockSpec((tm, D), row), _const_spec((1, D)), _const_spec(wkva.shape),
                  _const_spec((1, kv_rank)), _const_spec(wk.shape), _const_spec(wvt.shape),
                  _const_spec((1, HEAD_PAD)), tab, tab],
        out_specs=[pl.BlockSpec((N_HEADS, tm, HEAD_PAD), head_rows),
                   pl.BlockSpec((N_HEADS, V_ROWS, tm), head_cols)],
        out_shape=[jax.ShapeDtypeStruct((N_HEADS, T, HEAD_PAD), BF16),
                   jax.ShapeDtypeStruct((N_HEADS, V_ROWS, T), BF16)],
        compiler_params=_params(("parallel",), 2 * (tm * D * 4 + wkva.size * 2 + wk.size * 2 + wvt.size * 2
                                                    + N_HEADS * tm * (HEAD_PAD + V_ROWS) * 2)
                                + 8 * tm * D * 4),
    )(x, ga, wkva, gkva, wk, wvt, gk, cos_dup, sin_dup)
    return qt, k, vt


def _flash_kernel(qt_ref, k_ref, vt_ref, o_ref, s_ref, *, tk, chunks_per_step):
    qt = qt_ref[0]
    tq = qt.shape[1]
    n_chunks = k_ref.shape[1] // tk

    def scores(j, slot):
        s = _dot(k_ref[0, pl.ds(pl.multiple_of(j * tk, tk), tk), :], qt)
        s_ref[slot, :tk] = s
        return jnp.max(s, axis=0, keepdims=True)

    def absorb(j, slot, s_max, m, acc):
        vt = vt_ref[0, :, pl.ds(pl.multiple_of(j * tk, tk), tk)]
        m_new = jnp.maximum(m, s_max)
        p = jnp.exp2(s_ref[slot, :tk] - m_new)
        acc = jnp.exp2(m - m_new) * acc + _dot(vt, p.astype(BF16))
        return m_new, acc

    def body(step, carry):
        s_max, m, acc = carry
        for c in range(chunks_per_step):
            j = step * chunks_per_step + c
            s_max_next = scores(jnp.minimum(j + 1, n_chunks - 1), (c + 1) % 2)
            m, acc = absorb(j, c % 2, s_max, m, acc)
            s_max = s_max_next
        return s_max, m, acc

    m0 = jnp.full((1, tq), -jnp.inf, F32)
    acc0 = jnp.zeros((V_ROWS, tq), F32)
    _, _, acc = lax.fori_loop(0, n_chunks // chunks_per_step, body, (scores(0, 0), m0, acc0))
    o_ref[...] = (acc[:V_HEAD_DIM] / acc[V_HEAD_DIM:V_HEAD_DIM + 1]).T.astype(BF16)


def _attention(qt, k, vt, seq_len, tq, tk):
    n_seq = k.shape[1] // seq_len
    nq = seq_len // tq
    n_chunks = seq_len // tk
    chunks_per_step = math.gcd(n_chunks, 16)
    if chunks_per_step == n_chunks and n_chunks % 4 == 0:
        chunks_per_step //= 2
    assert chunks_per_step % 2 == 0
    return pl.pallas_call(
        functools.partial(_flash_kernel, tk=tk, chunks_per_step=chunks_per_step),
        grid=(n_seq, N_HEADS, nq),
        in_specs=[pl.BlockSpec((1, HEAD_PAD, tq), lambda b, h, i: (h, 0, b * nq + i)),
                  pl.BlockSpec((1, seq_len, HEAD_PAD), lambda b, h, i: (h, b, 0)),
                  pl.BlockSpec((1, V_ROWS, seq_len), lambda b, h, i: (h, 0, b))],
        out_specs=pl.BlockSpec((tq, V_HEAD_DIM), lambda b, h, i: (b * nq + i, h)),
        out_shape=jax.ShapeDtypeStruct((n_seq * seq_len, N_HEADS * V_HEAD_DIM), BF16),
        scratch_shapes=[pltpu.VMEM((2, tk + SCORE_PAD_ROWS, tq), F32)],
        compiler_params=_params(("parallel", "parallel", "arbitrary"),
                                2 * seq_len * (HEAD_PAD + V_ROWS) * 2 + 8 * tq * tk * 4
                                + 4 * tq * HEAD_PAD * 2),
    )(qt, k, vt)


def _pack_bf16_pair(hi, lo):
    bits = lambda v: lax.bitcast_convert_type(v.astype(BF16).astype(F32), jnp.uint32)
    return bits(hi) | (bits(lo) >> 16)


def _unpack_bf16_pair(w):
    hi = lax.bitcast_convert_type(w & jnp.uint32(0xFFFF0000), F32)
    lo = lax.bitcast_convert_type(w << 16, F32)
    return hi.astype(BF16), lo.astype(BF16)


def _fnet_stage1_kernel(x_hbm, g_ref, wc_ref, w1_ref, tw_ref, a_ref, xbuf, sem, u_ref, *, n1, n2, gdim):
    i = pl.program_id(0)
    slot = i % DMA_SLOTS

    def gather(step):
        rows = pl.ds((step // n2) * n1, n1)
        to_slot = step % DMA_SLOTS
        return pltpu.make_async_copy(x_hbm.at[rows, step % n2, :], xbuf.at[to_slot], sem.at[to_slot])

    @pl.when(i == 0)
    def _():
        for ahead in range(DMA_SLOTS - 1):
            gather(ahead).start()

    @pl.when(i + DMA_SLOTS - 1 < pl.num_programs(0))
    def _():
        gather(i + DMA_SLOTS - 1).start()

    gather(i).wait()
    h = _rms(xbuf[slot], g_ref[...]).astype(BF16)
    wc = wc_ref[...]
    for gi in range(FNET_GROUPS):
        cols = slice(gi * gdim, (gi + 1) * gdim)
        u = _dot(h[:, cols], wc)
        u_ref[:n1, cols] = u[:, :gdim]
        u_ref[n1:, cols] = u[:, gdim:]
    a = _dot(w1_ref[...], u_ref[...].astype(BF16))
    a_re = a[:n1]
    a_im = a[n1:]
    tw_c = tw_ref[0, :, 0:1]
    tw_s = tw_ref[0, :, 1:2]
    a_ref[...] = _pack_bf16_pair(a_re * tw_c + a_im * tw_s, a_im * tw_c - a_re * tw_s)


def _fnet_stage2_kernel(a_hbm, w2_ref, z_hbm, abuf, zbuf, in_sem, out_sem, *, n1, n2):
    i = pl.program_id(0)
    n_steps = pl.num_programs(0)
    slot = i % DMA_SLOTS

    def gather(step):
        rows = pl.ds((step // n1) * n2, n2)
        to_slot = step % DMA_SLOTS
        return pltpu.make_async_copy(a_hbm.at[rows, step % n1, :], abuf.at[to_slot], in_sem.at[to_slot])

    def scatter(step):
        rows = pl.ds((step // n1) * n2, n2)
        from_slot = step % DMA_SLOTS
        return pltpu.make_async_copy(zbuf.at[from_slot], z_hbm.at[rows, step % n1, :], out_sem.at[from_slot])

    @pl.when(i == 0)
    def _():
        for ahead in range(DMA_SLOTS - 1):
            gather(ahead).start()

    @pl.when(i + DMA_SLOTS - 1 < n_steps)
    def _():
        gather(i + DMA_SLOTS - 1).start()

    gather(i).wait()
    a_re, a_im = _unpack_bf16_pair(abuf[slot])
    z = _dot(w2_ref[...], jnp.concatenate([a_re, a_im], axis=0))

    @pl.when(i >= DMA_SLOTS)
    def _():
        scatter(i - DMA_SLOTS).wait()

    zbuf[slot] = z
    scatter(i).start()

    @pl.when(i == n_steps - 1)
    def _():
        for back in range(DMA_SLOTS):
            scatter(i - back).wait()


def _dft_tables(n1, n2, gdim):
    def cs(rows, cols, period):
        ang = 2.0 * np.pi * ((np.arange(rows)[:, None] * np.arange(cols)[None, :]) % period) / period
        return np.cos(ang), np.sin(ang)
    cc, sc = cs(gdim, gdim, gdim)
    w_ch = np.concatenate([cc, -sc], axis=1) / np.sqrt(gdim)
    c1, s1 = cs(n1, n1, n1)
    w1 = np.block([[c1, s1], [-s1, c1]]) / np.sqrt(n1)
    c2, s2 = cs(n2, n2, n2)
    w2 = np.concatenate([c2, s2], axis=1) / np.sqrt(n2)
    twc, tws = cs(n2, n1, n1 * n2)
    tw = np.zeros((n2, n1, LANES), np.float64)
    tw[:, :, 0] = twc
    tw[:, :, 1] = tws
    return [jnp.asarray(t, F32) for t in (w_ch, w1, w2, tw)]


def _fnet_mix(x, g, seq_len):
    T, D = x.shape
    n_seq = T // seq_len
    n1 = FFT_N1
    n2 = seq_len // n1
    assert n_seq * min(n1, n2) >= DMA_SLOTS
    gdim = D // FNET_GROUPS
    w_ch, w1, w2, tw = _dft_tables(n1, n2, gdim)
    w_ch, w1, w2 = (t.astype(BF16) for t in (w_ch, w1, w2))
    const2 = lambda shape: pl.BlockSpec(shape, lambda i: (0, 0))
    any_spec = pl.BlockSpec(memory_space=pl.ANY)
    slab = pl.BlockSpec((None, n1, D), lambda i: (i, 0, 0))
    a = pl.pallas_call(
        functools.partial(_fnet_stage1_kernel, n1=n1, n2=n2, gdim=gdim),
        grid=(n_seq * n2,),
        in_specs=[any_spec, const2((1, D)), const2(w_ch.shape), const2(w1.shape),
                  pl.BlockSpec((1, n1, LANES), lambda i: (i % n2, 0, 0))],
        out_specs=slab,
        out_shape=jax.ShapeDtypeStruct((n_seq * n2, n1, D), jnp.uint32),
        scratch_shapes=[pltpu.VMEM((DMA_SLOTS, n1, D), F32), pltpu.SemaphoreType.DMA((DMA_SLOTS,)),
                        pltpu.VMEM((2 * n1, D), F32)],
        compiler_params=_params(("arbitrary",), 24 * n1 * D * 4 + 16 * 1024 * 1024),
    )(x.reshape(T // n2, n2, D), g, w_ch, w1, tw)
    z = pl.pallas_call(
        functools.partial(_fnet_stage2_kernel, n1=n1, n2=n2),
        grid=(n_seq * n1,),
        in_specs=[any_spec, const2(w2.shape)],
        out_specs=any_spec,
        out_shape=jax.ShapeDtypeStruct((n_seq * n2, n1, D), F32),
        scratch_shapes=[pltpu.VMEM((DMA_SLOTS, n2, D), jnp.uint32), pltpu.VMEM((DMA_SLOTS, n2, D), F32),
                        pltpu.SemaphoreType.DMA((DMA_SLOTS,)), pltpu.SemaphoreType.DMA((DMA_SLOTS,))],
        compiler_params=_params(("arbitrary",), 24 * n2 * D * 4 + 16 * 1024 * 1024),
    )(a, w2)
    return z.reshape(T, D)


def _oproj_kernel(x_ref, a_ref, w_ref, o_ref):
    o_ref[...] = x_ref[...] + _dot(a_ref[...].astype(BF16), w_ref[...])


def _oproj_residual(x, a, w, tm):
    T, D = x.shape
    row = lambda i: (i, 0)
    return pl.pallas_call(
        _oproj_kernel,
        grid=(T // tm,),
        in_specs=[pl.BlockSpec((tm, D), row), pl.BlockSpec((tm, a.shape[1]), row), _const_spec(w.shape)],
        out_specs=pl.BlockSpec((tm, D), row),
        out_shape=jax.ShapeDtypeStruct((T, D), F32),
        compiler_params=_params(("parallel",), 2 * (2 * tm * D * 4 + tm * a.shape[1] * a.dtype.itemsize
                                                    + w.size * 2) + 2 * tm * D * 4),
    )(x, a, w)


def _mlp_kernel(x_ref, g_ref, wup_ref, wdn_ref, o_ref, xn_ref):
    @pl.when(pl.program_id(1) == 0)
    def _():
        x = x_ref[...]
        xn_ref[...] = _rms(x, g_ref[...]).astype(BF16)
        o_ref[...] = x

    hcol = jnp.square(jnp.maximum(_dot(xn_ref[...], wup_ref[...]), 0.0)).astype(BF16)
    o_ref[...] += _dot(hcol, wdn_ref[...])


def _mlp(x, g, w_up, w_down, layer, tm, tf):
    T, D = x.shape
    dff = w_up.shape[2]
    return pl.pallas_call(
        _mlp_kernel,
        grid=(T // tm, dff // tf),
        in_specs=[pl.BlockSpec((tm, D), lambda i, f: (i, 0)), pl.BlockSpec((1, D), lambda i, f: (0, 0)),
                  pl.BlockSpec((None, D, tf), lambda i, f: (layer, 0, f)),
                  pl.BlockSpec((None, tf, D), lambda i, f: (layer, f, 0))],
        out_specs=pl.BlockSpec((tm, D), lambda i, f: (i, 0)),
        out_shape=jax.ShapeDtypeStruct((T, D), F32),
        scratch_shapes=[pltpu.VMEM((tm, D), BF16)],
        compiler_params=_params(("parallel", "arbitrary"),
                                2 * (2 * tm * D * 4 + 2 * D * tf * 2) + tm * D * 2 + 3 * tm * tf * 4),
    )(x, g, w_up, w_down)


def _dup_rope_cols(w):
    return jnp.concatenate([w, w], axis=-1)


def _pad_head_vec(g):
    return jnp.concatenate([g[:QK_NOPE_DIM], _dup_rope_cols(g[QK_NOPE_DIM:])])[None, :].astype(F32)


def _rope_tables(n_seq, seq_len):
    inv_freq = ROPE_THETA ** (-jnp.arange(0, QK_ROPE_DIM, 2, dtype=F32) / QK_ROPE_DIM)
    ang = jnp.arange(seq_len, dtype=F32)[:, None] * inv_freq[None, :]
    c, s = jnp.cos(ang), jnp.sin(ang)
    z = jnp.zeros_like(c)
    tile = lambda t: jnp.tile(t, (n_seq, 1))
    return (tile(c).T, tile(s).T,
            tile(jnp.concatenate([c, c, z, z], axis=1)), tile(jnp.concatenate([-s, s, z, z], axis=1)))


def _pick_tile(n, target):
    t = min(n, target)
    while n % t:
        t //= 2
    return t


def _trunk(x, seq_len, p):
    T = x.shape[0]
    rope = _rope_tables(T // seq_len, seq_len)
    tm = _pick_tile(T, 512)
    tf = _pick_tile(p["w_up"].shape[2], 1024)
    ts = _pick_tile(seq_len, 512)
    for i in range(p["w_up"].shape[0]):
        j = i // 2
        if i % 2 == 0:
            qt, k, vt = _mla_project(x, p["attn_norm"][j], p["w_q_a"][j], p["q_a_norm"][j], p["w_q_b"][j],
                                     p["q_norm"][j], p["w_kv_a"][j], p["kv_a_norm"][j], p["w_k_b"][j], p["w_v_b_t"][j],
                                     p["k_norm"][j], rope, tm)
            a = _attention(qt, k, vt, seq_len, ts, ts)
        else:
            a = _fnet_mix(x, p["fnet_norm"][j], seq_len)
        x = _oproj_residual(x, a, p["w_mix_o"][i], tm)
        x = _mlp(x, p["mlp_norm"][i], p["w_up"], p["w_down"], i, tm, tf)
    return x


def kernel(x_prompt, x_sample, attn_norm, w_q_a, q_a_norm, w_q_b, w_kv_a, kv_a_norm, w_kv_b, q_norm, k_norm,
           w_attn_o, fnet_norm, w_fnet_o, mlp_norm, w_up, w_down):
    depth = mlp_norm.shape[0]
    n_attn = attn_norm.shape[0]
    kv_rank = kv_a_norm.shape[1]
    q_rank = q_a_norm.shape[1]
    row2 = lambda v: v[None, :].astype(F32)
    wqbt = jnp.swapaxes(w_q_b.reshape(n_attn, q_rank, N_HEADS, QK_HEAD_DIM), 1, 3)
    wqbt = jnp.pad(jnp.swapaxes(wqbt, 1, 2), ((0, 0), (0, 0), (0, HEAD_PAD - QK_HEAD_DIM), (0, 0)))
    wqbt = wqbt.reshape(n_attn, N_HEADS * HEAD_PAD, q_rank)
    wkvb = w_kv_b.reshape(n_attn, kv_rank, N_HEADS, QK_NOPE_DIM + V_HEAD_DIM)
    wk = wkvb[..., :QK_NOPE_DIM].reshape(n_attn, kv_rank, N_HEADS * QK_NOPE_DIM)
    wvt = jnp.swapaxes(wkvb[..., QK_NOPE_DIM:].reshape(n_attn, kv_rank, N_HEADS * V_HEAD_DIM), 1, 2)
    q_gain = lambda g: jnp.pad(g, (0, HEAD_PAD - QK_HEAD_DIM))[:, None].astype(F32)
    wkva = jnp.concatenate([w_kv_a[..., :kv_rank], _dup_rope_cols(w_kv_a[..., kv_rank:])], axis=-1)
    layers = lambda w: [w[j].astype(BF16) for j in range(w.shape[0])]
    p = {
        "attn_norm": [row2(g) for g in attn_norm], "w_q_a": layers(w_q_a),
        "q_a_norm": [row2(g) for g in q_a_norm], "w_q_b": layers(wqbt),
        "q_norm": [q_gain(g) for g in q_norm], "w_kv_a": layers(wkva),
        "kv_a_norm": [row2(g) for g in kv_a_norm], "w_k_b": layers(wk), "w_v_b_t": layers(wvt),
        "k_norm": [_pad_head_vec(g) for g in k_norm], "fnet_norm": [row2(g) for g in fnet_norm],
        "w_mix_o": [(w_attn_o if i % 2 == 0 else w_fnet_o)[i // 2].astype(BF16) for i in range(depth)],
        "mlp_norm": [row2(g) for g in mlp_norm], "w_up": w_up.astype(BF16), "w_down": w_down.astype(BF16),
    }
    outs = []
    for xg in (x_prompt, x_sample):
        b, s, d = xg.shape
        outs.append(_trunk(xg.reshape(b * s, d), s, p).reshape(b, s, d))
    return tuple(outs)
```

```python
import functools
import math

import numpy as np
import jax
import jax.numpy as jnp
from jax import lax
from jax.experimental import pallas as pl
from jax.experimental.pallas import tpu as pltpu

F32 = jnp.float32
BF16 = jnp.bfloat16

N_HEADS = 16
QK_NOPE_DIM = 128
QK_ROPE_DIM = 64
QK_HEAD_DIM = QK_NOPE_DIM + QK_ROPE_DIM
V_HEAD_DIM = 128
HEAD_PAD = 256
BF16_ROWS = 16
V_ROWS = V_HEAD_DIM + BF16_ROWS
ROPE_THETA = 10000.0
FNET_GROUPS = 8
EPS = 1e-6
FFT_N1 = 128
LANES = 128
DMA_SLOTS = 8
SCORE_PAD_ROWS = 8
VMEM_CAP = 60 * 1024 * 1024


def _vmem_limit(nbytes):
    return int(min(max(nbytes, 16 * 1024 * 1024), VMEM_CAP))


def _params(sem, nbytes):
    return pltpu.CompilerParams(dimension_semantics=sem, vmem_limit_bytes=_vmem_limit(nbytes))


def _rms(xf, g):
    return xf * lax.rsqrt(jnp.mean(xf * xf, axis=-1, keepdims=True) + EPS) * g


def _dot(a, b):
    return jnp.dot(a, b, preferred_element_type=F32)


def _rope_dup(pe, cos_t, sin_t):
    return pe * cos_t + pltpu.roll(pe, 32, axis=1) * sin_t


def _qproj_kernel(x_ref, ga_ref, wqa_ref, gqa_ref, wqbt_ref, gq_ref, cos_ref, sin_ref, q_ref, *, qscale):
    h = _rms(x_ref[...], ga_ref[...]).astype(BF16)
    cq_t = _rms(_dot(h, wqa_ref[...]), gqa_ref[...]).T.astype(BF16)
    cos_t = cos_ref[...]
    sin_t = sin_ref[...]
    half = QK_ROPE_DIM // 2
    for hh in range(N_HEADS):
        qh = _dot(wqbt_ref[hh * HEAD_PAD:(hh + 1) * HEAD_PAD, :], cq_t)
        ss = jnp.sum(qh * qh, axis=0, keepdims=True)
        qh = qh * (lax.rsqrt(ss * (1.0 / QK_HEAD_DIM) + EPS) * qscale) * gq_ref[...]
        x1 = qh[QK_NOPE_DIM:QK_NOPE_DIM + half]
        x2 = qh[QK_NOPE_DIM + half:QK_HEAD_DIM]
        q_ref[hh, :QK_NOPE_DIM, :] = qh[:QK_NOPE_DIM].astype(BF16)
        q_ref[hh, QK_NOPE_DIM:QK_NOPE_DIM + half, :] = (x1 * cos_t - x2 * sin_t).astype(BF16)
        q_ref[hh, QK_NOPE_DIM + half:QK_HEAD_DIM, :] = (x2 * cos_t + x1 * sin_t).astype(BF16)
        q_ref[hh, QK_HEAD_DIM:, :] = jnp.zeros((HEAD_PAD - QK_HEAD_DIM, qh.shape[1]), BF16)


def _kvproj_kernel(x_ref, ga_ref, wkva_ref, gkva_ref, wk_ref, wvt_ref, gk_ref, cos_ref, sin_ref, k_ref, v_ref,
                   *, kv_rank):
    h = _rms(x_ref[...], ga_ref[...]).astype(BF16)
    ckv = _dot(h, wkva_ref[...])
    c = _rms(ckv[:, :kv_rank], gkva_ref[...])
    pe = ckv[:, kv_rank:]
    pe_ss = 0.5 * jnp.sum(pe * pe, axis=-1, keepdims=True)
    g_nope = gk_ref[:, :QK_NOPE_DIM]
    r = _rope_dup(pe * gk_ref[:, QK_NOPE_DIM:], cos_ref[...], sin_ref[...])
    vt = _dot(wvt_ref[...], c.T.astype(BF16))
    ones_row = (lax.broadcasted_iota(jnp.int32, (BF16_ROWS, vt.shape[1]), 0) == 0).astype(BF16)
    c = c.astype(BF16)
    for pair in range(N_HEADS // 2):
        k_pair = _dot(c, wk_ref[:, pair * 2 * QK_NOPE_DIM:(pair + 1) * 2 * QK_NOPE_DIM])
        for hh in (2 * pair, 2 * pair + 1):
            k_nope = k_pair[:, (hh % 2) * QK_NOPE_DIM:(hh % 2 + 1) * QK_NOPE_DIM]
            ss = jnp.sum(k_nope * k_nope, axis=-1, keepdims=True) + pe_ss
            inv = lax.rsqrt(ss * (1.0 / QK_HEAD_DIM) + EPS)
            k_ref[hh, :, :QK_NOPE_DIM] = (k_nope * inv * g_nope).astype(BF16)
            k_ref[hh, :, QK_NOPE_DIM:] = (r * inv).astype(BF16)
            v_ref[hh, :V_HEAD_DIM, :] = vt[hh * V_HEAD_DIM:(hh + 1) * V_HEAD_DIM].astype(BF16)
            v_ref[hh, V_HEAD_DIM:, :] = ones_row


def _const_spec(shape):
    return pl.BlockSpec(shape, lambda i: (0,) * len(shape))


def _mla_project(x, ga, wqa, gqa, wqbt, gq, wkva, gkva, wk, wvt, gk, rope, tm):
    T, D = x.shape
    q_rank = wqa.shape[1]
    kv_rank = gkva.shape[1]
    qscale = QK_HEAD_DIM ** -0.5 * math.log2(math.e)
    row = lambda i: (i, 0)
    head_rows = lambda i: (0, i, 0)
    head_cols = lambda i: (0, 0, i)
    tab = pl.BlockSpec((tm, LANES), row)
    tab_t = pl.BlockSpec((QK_ROPE_DIM // 2, tm), lambda i: (0, i))
    cos_t, sin_t, cos_dup, sin_dup = rope
    qt = pl.pallas_call(
        functools.partial(_qproj_kernel, qscale=qscale),
        grid=(T // tm,),
        in_specs=[pl.BlockSpec((tm, D), row), _const_spec((1, D)), _const_spec(wqa.shape),
                  _const_spec((1, q_rank)), _const_spec(wqbt.shape), _const_spec((HEAD_PAD, 1)), tab_t, tab_t],
        out_specs=pl.BlockSpec((N_HEADS, HEAD_PAD, tm), head_cols),
        out_shape=jax.ShapeDtypeStruct((N_HEADS, HEAD_PAD, T), BF16),
        compiler_params=_params(("parallel",), 2 * (tm * D * 4 + wqa.size * 2 + wqbt.size * 2
                                                    + N_HEADS * tm * HEAD_PAD * 2) + 8 * tm * D * 4),
    )(x, ga, wqa, gqa, wqbt, gq, cos_t, sin_t)
    k, vt = pl.pallas_call(
        functools.partial(_kvproj_kernel, kv_rank=kv_rank),
        grid=(T // tm,),
        in_specs=[pl.BlockSpec((tm, D), row), _const_spec((1, D)), _const_spec(wkva.shape),
                  _const_spec((1, kv_rank)), _const_spec(wk.shape), _const_spec(wvt.shape),
                  _const_spec((1, HEAD_PAD)), tab, tab],
        out_specs=[pl.BlockSpec((N_HEADS, tm, HEAD_PAD), head_rows),
                   pl.BlockSpec((N_HEADS, V_ROWS, tm), head_cols)],
        out_shape=[jax.ShapeDtypeStruct((N_HEADS, T, HEAD_PAD), BF16),
                   jax.ShapeDtypeStruct((N_HEADS, V_ROWS, T), BF16)],
        compiler_params=_params(("parallel",), 2 * (tm * D * 4 + wkva.size * 2 + wk.size * 2 + wvt.size * 2
                                                    + N_HEADS * tm * (HEAD_PAD + V_ROWS) * 2)
                                + 8 * tm * D * 4),
    )(x, ga, wkva, gkva, wk, wvt, gk, cos_dup, sin_dup)
    return qt, k, vt


def _flash_kernel(qt_ref, k_ref, vt_ref, o_ref, s_ref, *, tk, chunks_per_step):
    qt = qt_ref[0]
    tq = qt.shape[1]
    n_chunks = k_ref.shape[1] // tk

    def scores(j, slot):
        s = _dot(k_ref[0, pl.ds(pl.multiple_of(j * tk, tk), tk), :], qt)
        s_ref[slot, :tk] = s
        return jnp.max(s, axis=0, keepdims=True)

    def absorb(j, slot, s_max, m, acc):
        vt = vt_ref[0, :, pl.ds(pl.multiple_of(j * tk, tk), tk)]
        m_new = jnp.maximum(m, s_max)
        p = jnp.exp2(s_ref[slot, :tk] - m_new)
        acc = jnp.exp2(m - m_new) * acc + _dot(vt, p.astype(BF16))
        return m_new, acc

    def body(step, carry):
        s_max, m, acc = carry
        for c in range(chunks_per_step):
            j = step * chunks_per_step + c
            s_max_next = scores(jnp.minimum(j + 1, n_chunks - 1), (c + 1) % 2)
            m, acc = absorb(j, c % 2, s_max, m, acc)
            s_max = s_max_next
        return s_max, m, acc

    m0 = jnp.full((1, tq), -jnp.inf, F32)
    acc0 = jnp.zeros((V_ROWS, tq), F32)
    _, _, acc = lax.fori_loop(0, n_chunks // chunks_per_step, body, (scores(0, 0), m0, acc0))
    o_ref[...] = (acc[:V_HEAD_DIM] / acc[V_HEAD_DIM:V_HEAD_DIM + 1]).T.astype(BF16)


def _attention(qt, k, vt, seq_len, tq, tk):
    n_seq = k.shape[1] // seq_len
    nq = seq_len // tq
    n_chunks = seq_len // tk
    chunks_per_step = math.gcd(n_chunks, 16)
    if chunks_per_step == n_chunks and n_chunks % 4 == 0:
        chunks_per_step //= 2
    assert chunks_per_step % 2 == 0
    return pl.pallas_call(
        functools.partial(_flash_kernel, tk=tk, chunks_per_step=chunks_per_step),
        grid=(n_seq, N_HEADS, nq),
        in_specs=[pl.BlockSpec((1, HEAD_PAD, tq), lambda b, h, i: (h, 0, b * nq + i)),
                  pl.BlockSpec((1, seq_len, HEAD_PAD), lambda b, h, i: (h, b, 0)),
                  pl.BlockSpec((1, V_ROWS, seq_len), lambda b, h, i: (h, 0, b))],
        out_specs=pl.BlockSpec((tq, V_HEAD_DIM), lambda b, h, i: (b * nq + i, h)),
        out_shape=jax.ShapeDtypeStruct((n_seq * seq_len, N_HEADS * V_HEAD_DIM), BF16),
        scratch_shapes=[pltpu.VMEM((2, tk + SCORE_PAD_ROWS, tq), F32)],
        compiler_params=_params(("parallel", "parallel", "arbitrary"),
                                2 * seq_len * (HEAD_PAD + V_ROWS) * 2 + 8 * tq * tk * 4
                                + 4 * tq * HEAD_PAD * 2),
    )(qt, k, vt)


def _pack_bf16_pair(hi, lo):
    bits = lambda v: lax.bitcast_convert_type(v.astype(BF16).astype(F32), jnp.uint32)
    return bits(hi) | (bits(lo) >> 16)


def _unpack_bf16_pair(w):
    hi = lax.bitcast_convert_type(w & jnp.uint32(0xFFFF0000), F32)
    lo = lax.bitcast_convert_type(w << 16, F32)
    return hi.astype(BF16), lo.astype(BF16)


def _fnet_stage1_kernel(x_hbm, g_ref, wc_ref, w1_ref, tw_ref, a_ref, xbuf, sem, u_ref, *, n1, n2, gdim):
    i = pl.program_id(0)
    slot = i % DMA_SLOTS

    def gather(step):
        rows = pl.ds((step // n2) * n1, n1)
        to_slot = step % DMA_SLOTS
        return pltpu.make_async_copy(x_hbm.at[rows, step % n2, :], xbuf.at[to_slot], sem.at[to_slot])

    @pl.when(i == 0)
    def _():
        for ahead in range(DMA_SLOTS - 1):
            gather(ahead).start()

    @pl.when(i + DMA_SLOTS - 1 < pl.num_programs(0))
    def _():
        gather(i + DMA_SLOTS - 1).start()

    gather(i).wait()
    h = _rms(xbuf[slot], g_ref[...]).astype(BF16)
    wc = wc_ref[...]
    for gi in range(FNET_GROUPS):
        cols = slice(gi * gdim, (gi + 1) * gdim)
        u = _dot(h[:, cols], wc)
        u_ref[:n1, cols] = u[:, :gdim]
        u_ref[n1:, cols] = u[:, gdim:]
    a = _dot(w1_ref[...], u_ref[...].astype(BF16))
    a_re = a[:n1]
    a_im = a[n1:]
    tw_c = tw_ref[0, :, 0:1]
    tw_s = tw_ref[0, :, 1:2]
    a_ref[...] = _pack_bf16_pair(a_re * tw_c + a_im * tw_s, a_im * tw_c - a_re * tw_s)


def _fnet_stage2_kernel(a_hbm, w2_ref, z_hbm, abuf, zbuf, in_sem, out_sem, *, n1, n2):
    i = pl.program_id(0)
    n_steps = pl.num_programs(0)
    slot = i % DMA_SLOTS

    def gather(step):
        rows = pl.ds((step // n1) * n2, n2)
        to_slot = step % DMA_SLOTS
        return pltpu.make_async_copy(a_hbm.at[rows, step % n1, :], abuf.at[to_slot], in_sem.at[to_slot])

    def scatter(step):
        rows = pl.ds((step // n1) * n2, n2)
        from_slot = step % DMA_SLOTS
        return pltpu.make_async_copy(zbuf.at[from_slot], z_hbm.at[rows, step % n1, :], out_sem.at[from_slot])

    @pl.when(i == 0)
    def _():
        for ahead in range(DMA_SLOTS - 1):
            gather(ahead).start()

    @pl.when(i + DMA_SLOTS - 1 < n_steps)
    def _():
        gather(i + DMA_SLOTS - 1).start()

    gather(i).wait()
    a_re, a_im = _unpack_bf16_pair(abuf[slot])
    z = _dot(w2_ref[...], jnp.concatenate([a_re, a_im], axis=0))

    @pl.when(i >= DMA_SLOTS)
    def _():
        scatter(i - DMA_SLOTS).wait()

    zbuf[slot] = z
    scatter(i).start()

    @pl.when(i == n_steps - 1)
    def _():
        for back in range(DMA_SLOTS):
            scatter(i - back).wait()


def _dft_tables(n1, n2, gdim):
    def cs(rows, cols, period):
        ang = 2.0 * np.pi * ((np.arange(rows)[:, None] * np.arange(cols)[None, :]) % period) / period
        return np.cos(ang), np.sin(ang)
    cc, sc = cs(gdim, gdim, gdim)
    w_ch = np.concatenate([cc, -sc], axis=1) / np.sqrt(gdim)
    c1, s1 = cs(n1, n1, n1)
    w1 = np.block([[c1, s1], [-s1, c1]]) / np.sqrt(n1)
    c2, s2 = cs(n2, n2, n2)
    w2 = np.concatenate([c2, s2], axis=1) / np.sqrt(n2)
    twc, tws = cs(n2, n1, n1 * n2)
    tw = np.zeros((n2, n1, LANES), np.float64)
    tw[:, :, 0] = twc
    tw[:, :, 1] = tws
    return [jnp.asarray(t, F32) for t in (w_ch, w1, w2, tw)]


def _fnet_mix(x, g, seq_len):
    T, D = x.shape
    n_seq = T // seq_len
    n1 = FFT_N1
    n2 = seq_len // n1
    assert n_seq * min(n1, n2) >= DMA_SLOTS
    gdim = D // FNET_GROUPS
    w_ch, w1, w2, tw = _dft_tables(n1, n2, gdim)
    w_ch, w1, w2 = (t.astype(BF16) for t in (w_ch, w1, w2))
    const2 = lambda shape: pl.BlockSpec(shape, lambda i: (0, 0))
    any_spec = pl.BlockSpec(memory_space=pl.ANY)
    slab = pl.BlockSpec((None, n1, D), lambda i: (i, 0, 0))
    a = pl.pallas_call(
        functools.partial(_fnet_stage1_kernel, n1=n1, n2=n2, gdim=gdim),
        grid=(n_seq * n2,),
        in_specs=[any_spec, const2((1, D)), const2(w_ch.shape), const2(w1.shape),
                  pl.BlockSpec((1, n1, LANES), lambda i: (i % n2, 0, 0))],
        out_specs=slab,
        out_shape=jax.ShapeDtypeStruct((n_seq * n2, n1, D), jnp.uint32),
        scratch_shapes=[pltpu.VMEM((DMA_SLOTS, n1, D), F32), pltpu.SemaphoreType.DMA((DMA_SLOTS,)),
                        pltpu.VMEM((2 * n1, D), F32)],
        compiler_params=_params(("arbitrary",), 24 * n1 * D * 4 + 16 * 1024 * 1024),
    )(x.reshape(T // n2, n2, D), g, w_ch, w1, tw)
    z = pl.pallas_call(
        functools.partial(_fnet_stage2_kernel, n1=n1, n2=n2),
        grid=(n_seq * n1,),
        in_specs=[any_spec, const2(w2.shape)],
        out_specs=any_spec,
        out_shape=jax.ShapeDtypeStruct((n_seq * n2, n1, D), F32),
        scratch_shapes=[pltpu.VMEM((DMA_SLOTS, n2, D), jnp.uint32), pltpu.VMEM((DMA_SLOTS, n2, D), F32),
                        pltpu.SemaphoreType.DMA((DMA_SLOTS,)), pltpu.SemaphoreType.DMA((DMA_SLOTS,))],
        compiler_params=_params(("arbitrary",), 24 * n2 * D * 4 + 16 * 1024 * 1024),
    )(a, w2)
    return z.reshape(T, D)


def _oproj_kernel(x_ref, a_ref, w_ref, o_ref):
    o_ref[...] = x_ref[...] + _dot(a_ref[...].astype(BF16), w_ref[...])


def _oproj_residual(x, a, w, tm):
    T, D = x.shape
    row = lambda i: (i, 0)
    return pl.pallas_call(
        _oproj_kernel,
        grid=(T // tm,),
        in_specs=[pl.BlockSpec((tm, D), row), pl.BlockSpec((tm, a.shape[1]), row), _const_spec(w.shape)],
        out_specs=pl.BlockSpec((tm, D), row),
        out_shape=jax.ShapeDtypeStruct((T, D), F32),
        compiler_params=_params(("parallel",), 2 * (2 * tm * D * 4 + tm * a.shape[1] * a.dtype.itemsize
                                                    + w.size * 2) + 2 * tm * D * 4),
    )(x, a, w)


def _mlp_kernel(x_ref, g_ref, wup_ref, wdn_ref, o_ref, xn_ref):
    @pl.when(pl.program_id(1) == 0)
    def _():
        x = x_ref[...]
        xn_ref[...] = _rms(x, g_ref[...]).astype(BF16)
        o_ref[...] = x

    hcol = jnp.square(jnp.maximum(_dot(xn_ref[...], wup_ref[...]), 0.0)).astype(BF16)
    o_ref[...] += _dot(hcol, wdn_ref[...])


def _mlp(x, g, w_up, w_down, layer, tm, tf):
    T, D = x.shape
    dff = w_up.shape[2]
    return pl.pallas_call(
        _mlp_kernel,
        grid=(T // tm, dff // tf),
        in_specs=[pl.BlockSpec((tm, D), lambda i, f: (i, 0)), pl.BlockSpec((1, D), lambda i, f: (0, 0)),
                  pl.BlockSpec((None, D, tf), lambda i, f: (layer, 0, f)),
                  pl.BlockSpec((None, tf, D), lambda i, f: (layer, f, 0))],
        out_specs=pl.BlockSpec((tm, D), lambda i, f: (i, 0)),
        out_shape=jax.ShapeDtypeStruct((T, D), F32),
        scratch_shapes=[pltpu.VMEM((tm, D), BF16)],
        compiler_params=_params(("parallel", "arbitrary"),
                                2 * (2 * tm * D * 4 + 2 * D * tf * 2) + tm * D * 2 + 3 * tm * tf * 4),
    )(x, g, w_up, w_down)


def _dup_rope_cols(w):
    return jnp.concatenate([w, w], axis=-1)


def _pad_head_vec(g):
    return jnp.concatenate([g[:QK_NOPE_DIM], _dup_rope_cols(g[QK_NOPE_DIM:])])[None, :].astype(F32)


def _rope_tables(n_seq, seq_len):
    inv_freq = ROPE_THETA ** (-jnp.arange(0, QK_ROPE_DIM, 2, dtype=F32) / QK_ROPE_DIM)
    ang = jnp.arange(seq_len, dtype=F32)[:, None] * inv_freq[None, :]
    c, s = jnp.cos(ang), jnp.sin(ang)
    z = jnp.zeros_like(c)
    tile = lambda t: jnp.tile(t, (n_seq, 1))
    return (tile(c).T, tile(s).T,
            tile(jnp.concatenate([c, c, z, z], axis=1)), tile(jnp.concatenate([-s, s, z, z], axis=1)))


def _pick_tile(n, target):
    t = min(n, target)
    while n % t:
        t //= 2
    return t


def _trunk(x, seq_len, p):
    T = x.shape[0]
    rope = _rope_tables(T // seq_len, seq_len)
    tm = _pick_tile(T, 512)
    tf = _pick_tile(p["w_up"].shape[2], 2048)
    ts = _pick_tile(seq_len, 512)
    for i in range(p["w_up"].shape[0]):
        j = i // 2
        if i % 2 == 0:
            qt, k, vt = _mla_project(x, p["attn_norm"][j], p["w_q_a"][j], p["q_a_norm"][j], p["w_q_b"][j],
                                     p["q_norm"][j], p["w_kv_a"][j], p["kv_a_norm"][j], p["w_k_b"][j], p["w_v_b_t"][j],
                                     p["k_norm"][j], rope, tm)
            a = _attention(qt, k, vt, seq_len, ts, ts)
        else:
            a = _fnet_mix(x, p["fnet_norm"][j], seq_len)
        x = _oproj_residual(x, a, p["w_mix_o"][i], tm)
        x = _mlp(x, p["mlp_norm"][i], p["w_up"], p["w_down"], i, tm, tf)
    return x


def kernel(x_prompt, x_sample, attn_norm, w_q_a, q_a_norm, w_q_b, w_kv_a, kv_a_norm, w_kv_b, q_norm, k_norm,
           w_attn_o, fnet_norm, w_fnet_o, mlp_norm, w_up, w_down):
    depth = mlp_norm.shape[0]
    n_attn = attn_norm.shape[0]
    kv_rank = kv_a_norm.shape[1]
    q_rank = q_a_norm.shape[1]
    row2 = lambda v: v[None, :].astype(F32)
    wqbt = jnp.swapaxes(w_q_b.reshape(n_attn, q_rank, N_HEADS, QK_HEAD_DIM), 1, 3)
    wqbt = jnp.pad(jnp.swapaxes(wqbt, 1, 2), ((0, 0), (0, 0), (0, HEAD_PAD - QK_HEAD_DIM), (0, 0)))
    wqbt = wqbt.reshape(n_attn, N_HEADS * HEAD_PAD, q_rank)
    wkvb = w_kv_b.reshape(n_attn, kv_rank, N_HEADS, QK_NOPE_DIM + V_HEAD_DIM)
    wk = wkvb[..., :QK_NOPE_DIM].reshape(n_attn, kv_rank, N_HEADS * QK_NOPE_DIM)
    wvt = jnp.swapaxes(wkvb[..., QK_NOPE_DIM:].reshape(n_attn, kv_rank, N_HEADS * V_HEAD_DIM), 1, 2)
    q_gain = lambda g: jnp.pad(g, (0, HEAD_PAD - QK_HEAD_DIM))[:, None].astype(F32)
    wkva = jnp.concatenate([w_kv_a[..., :kv_rank], _dup_rope_cols(w_kv_a[..., kv_rank:])], axis=-1)
    layers = lambda w: [w[j].astype(BF16) for j in range(w.shape[0])]
    p = {
        "attn_norm": [row2(g) for g in attn_norm], "w_q_a": layers(w_q_a),
        "q_a_norm": [row2(g) for g in q_a_norm], "w_q_b": layers(wqbt),
        "q_norm": [q_gain(g) for g in q_norm], "w_kv_a": layers(wkva),
        "kv_a_norm": [row2(g) for g in kv_a_norm], "w_k_b": layers(wk), "w_v_b_t": layers(wvt),
        "k_norm": [_pad_head_vec(g) for g in k_norm], "fnet_norm": [row2(g) for g in fnet_norm],
        "w_mix_o": [(w_attn_o if i % 2 == 0 else w_fnet_o)[i // 2].astype(BF16) for i in range(depth)],
        "mlp_norm": [row2(g) for g in mlp_norm], "w_up": w_up.astype(BF16), "w_down": w_down.astype(BF16),
    }
    outs = []
    for xg in (x_prompt, x_sample):
        b, s, d = xg.shape
        outs.append(_trunk(xg.reshape(b * s, d), s, p).reshape(b, s, d))
    return tuple(outs)
```

```python
import functools
import math

import numpy as np
import jax
import jax.numpy as jnp
from jax import lax
from jax.experimental import pallas as pl
from jax.experimental.pallas import tpu as pltpu

F32 = jnp.float32
BF16 = jnp.bfloat16

N_HEADS = 16
QK_NOPE_DIM = 128
QK_ROPE_DIM = 64
QK_HEAD_DIM = QK_NOPE_DIM + QK_ROPE_DIM
V_HEAD_DIM = 128
HEAD_PAD = 256
BF16_ROWS = 16
V_ROWS = V_HEAD_DIM + BF16_ROWS
ROPE_THETA = 10000.0
FNET_GROUPS = 8
EPS = 1e-6
FFT_N1 = 128
LANES = 128
DMA_SLOTS = 8
SCORE_PAD_ROWS = 8
VMEM_CAP = 60 * 1024 * 1024


def _vmem_limit(nbytes):
    return int(min(max(nbytes, 16 * 1024 * 1024), VMEM_CAP))


def _params(sem, nbytes):
    return pltpu.CompilerParams(dimension_semantics=sem, vmem_limit_bytes=_vmem_limit(nbytes))


def _rms(xf, g):
    return xf * lax.rsqrt(jnp.mean(xf * xf, axis=-1, keepdims=True) + EPS) * g


def _dot(a, b):
    return jnp.dot(a, b, preferred_element_type=F32)


def _rope_dup(pe, cos_t, sin_t):
    return pe * cos_t + pltpu.roll(pe, 32, axis=1) * sin_t


def _q_heads(h, wqa_ref, gqa_ref, wqbt_ref, gq_ref, cos_ref, sin_ref, q_ref, qscale):
    cq_t = _rms(_dot(h, wqa_ref[...]), gqa_ref[...]).T.astype(BF16)
    cos_t = cos_ref[...]
    sin_t = sin_ref[...]
    half = QK_ROPE_DIM // 2
    for hh in range(N_HEADS):
        qh = _dot(wqbt_ref[hh * HEAD_PAD:(hh + 1) * HEAD_PAD, :], cq_t)
        ss = jnp.sum(qh * qh, axis=0, keepdims=True)
        qh = qh * (lax.rsqrt(ss * (1.0 / QK_HEAD_DIM) + EPS) * qscale) * gq_ref[...]
        x1 = qh[QK_NOPE_DIM:QK_NOPE_DIM + half]
        x2 = qh[QK_NOPE_DIM + half:QK_HEAD_DIM]
        q_ref[hh, :QK_NOPE_DIM, :] = qh[:QK_NOPE_DIM].astype(BF16)
        q_ref[hh, QK_NOPE_DIM:QK_NOPE_DIM + half, :] = (x1 * cos_t - x2 * sin_t).astype(BF16)
        q_ref[hh, QK_NOPE_DIM + half:QK_HEAD_DIM, :] = (x2 * cos_t + x1 * sin_t).astype(BF16)
        q_ref[hh, QK_HEAD_DIM:, :] = jnp.zeros((HEAD_PAD - QK_HEAD_DIM, qh.shape[1]), BF16)


def _qkvproj_kernel(x_ref, ga_ref, wqa_ref, gqa_ref, wqbt_ref, gq_ref, cos_ref, sin_ref,
                    wkva_ref, gkva_ref, wk_ref, wvt_ref, gk_ref, cosd_ref, sind_ref, q_ref, k_ref, v_ref,
                    *, qscale, kv_rank):
    h = _rms(x_ref[...], ga_ref[...]).astype(BF16)
    _q_heads(h, wqa_ref, gqa_ref, wqbt_ref, gq_ref, cos_ref, sin_ref, q_ref, qscale)
    _kv_heads(h, wkva_ref, gkva_ref, wk_ref, wvt_ref, gk_ref, cosd_ref, sind_ref, k_ref, v_ref, kv_rank)


def _kv_heads(h, wkva_ref, gkva_ref, wk_ref, wvt_ref, gk_ref, cos_ref, sin_ref, k_ref, v_ref, kv_rank):
    ckv = _dot(h, wkva_ref[...])
    c = _rms(ckv[:, :kv_rank], gkva_ref[...])
    pe = ckv[:, kv_rank:]
    pe_ss = 0.5 * jnp.sum(pe * pe, axis=-1, keepdims=True)
    g_nope = gk_ref[:, :QK_NOPE_DIM]
    r = _rope_dup(pe * gk_ref[:, QK_NOPE_DIM:], cos_ref[...], sin_ref[...])
    vt = _dot(wvt_ref[...], c.T.astype(BF16))
    ones_row = (lax.broadcasted_iota(jnp.int32, (BF16_ROWS, vt.shape[1]), 0) == 0).astype(BF16)
    c = c.astype(BF16)
    for pair in range(N_HEADS // 2):
        k_pair = _dot(c, wk_ref[:, pair * 2 * QK_NOPE_DIM:(pair + 1) * 2 * QK_NOPE_DIM])
        for hh in (2 * pair, 2 * pair + 1):
            k_nope = k_pair[:, (hh % 2) * QK_NOPE_DIM:(hh % 2 + 1) * QK_NOPE_DIM]
            ss = jnp.sum(k_nope * k_nope, axis=-1, keepdims=True) + pe_ss
            inv = lax.rsqrt(ss * (1.0 / QK_HEAD_DIM) + EPS)
            k_ref[hh, :, :QK_NOPE_DIM] = (k_nope * inv * g_nope).astype(BF16)
            k_ref[hh, :, QK_NOPE_DIM:] = (r * inv).astype(BF16)
            v_ref[hh, :V_HEAD_DIM, :] = vt[hh * V_HEAD_DIM:(hh + 1) * V_HEAD_DIM].astype(BF16)
            v_ref[hh, V_HEAD_DIM:, :] = ones_row


def _const_spec(shape):
    return pl.BlockSpec(shape, lambda i: (0,) * len(shape))


def _mla_project(x, ga, wqa, gqa, wqbt, gq, wkva, gkva, wk, wvt, gk, rope, tm):
    T, D = x.shape
    q_rank = wqa.shape[1]
    kv_rank = gkva.shape[1]
    qscale = QK_HEAD_DIM ** -0.5 * math.log2(math.e)
    row = lambda i: (i, 0)
    head_rows = lambda i: (0, i, 0)
    head_cols = lambda i: (0, 0, i)
    tab = pl.BlockSpec((tm, LANES), row)
    tab_t = pl.BlockSpec((QK_ROPE_DIM // 2, tm), lambda i: (0, i))
    cos_t, sin_t, cos_dup, sin_dup = rope
    resident = lambda w: pl.BlockSpec(w.shape, lambda i: (0, 0), pipeline_mode=pl.Buffered(1))
    weight_bytes = 2 * (wqa.size + wqbt.size + wkva.size + wk.size + wvt.size)
    qt, k, vt = pl.pallas_call(
        functools.partial(_qkvproj_kernel, qscale=qscale, kv_rank=kv_rank),
        grid=(T // tm,),
        in_specs=[pl.BlockSpec((tm, D), row), _const_spec((1, D)), resident(wqa),
                  _const_spec((1, q_rank)), resident(wqbt), _const_spec((HEAD_PAD, 1)), tab_t, tab_t,
                  resident(wkva), _const_spec((1, kv_rank)), resident(wk), resident(wvt),
                  _const_spec((1, HEAD_PAD)), tab, tab],
        out_specs=[pl.BlockSpec((N_HEADS, HEAD_PAD, tm), head_cols),
                   pl.BlockSpec((N_HEADS, tm, HEAD_PAD), head_rows),
                   pl.BlockSpec((N_HEADS, V_ROWS, tm), head_cols)],
        out_shape=[jax.ShapeDtypeStruct((N_HEADS, HEAD_PAD, T), BF16),
                   jax.ShapeDtypeStruct((N_HEADS, T, HEAD_PAD), BF16),
                   jax.ShapeDtypeStruct((N_HEADS, V_ROWS, T), BF16)],
        compiler_params=_params(("parallel",), weight_bytes + 2 * (tm * D * 4 + N_HEADS * tm
                                                                   * (2 * HEAD_PAD + V_ROWS) * 2)
                                + 8 * tm * D * 4),
    )(x, ga, wqa, gqa, wqbt, gq, cos_t, sin_t, wkva, gkva, wk, wvt, gk, cos_dup, sin_dup)
    return qt, k, vt


def _flash_kernel(qt_ref, k_ref, vt_ref, o_ref, s_ref, *, tk, chunks_per_step):
    qt = qt_ref[0]
    tq = qt.shape[1]
    n_chunks = k_ref.shape[1] // tk

    def scores(j, slot):
        s = _dot(k_ref[0, pl.ds(pl.multiple_of(j * tk, tk), tk), :], qt)
        s_ref[slot, :tk] = s
        return jnp.max(s, axis=0, keepdims=True)

    def absorb(j, slot, s_max, m, acc):
        vt = vt_ref[0, :, pl.ds(pl.multiple_of(j * tk, tk), tk)]
        m_new = jnp.maximum(m, s_max)
        p = jnp.exp2(s_ref[slot, :tk] - m_new)
        acc = jnp.exp2(m - m_new) * acc + _dot(vt, p.astype(BF16))
        return m_new, acc

    def body(step, carry):
        s_max, m, acc = carry
        for c in range(chunks_per_step):
            j = step * chunks_per_step + c
            s_max_next = scores(jnp.minimum(j + 1, n_chunks - 1), (c + 1) % 2)
            m, acc = absorb(j, c % 2, s_max, m, acc)
            s_max = s_max_next
        return s_max, m, acc

    m0 = jnp.full((1, tq), -jnp.inf, F32)
    acc0 = jnp.zeros((V_ROWS, tq), F32)
    _, _, acc = lax.fori_loop(0, n_chunks // chunks_per_step, body, (scores(0, 0), m0, acc0))
    o_ref[...] = (acc[:V_HEAD_DIM] / acc[V_HEAD_DIM:V_HEAD_DIM + 1]).T.astype(BF16)


def _attention(qt, k, vt, seq_len, tq, tk):
    n_seq = k.shape[1] // seq_len
    nq = seq_len // tq
    n_chunks = seq_len // tk
    chunks_per_step = math.gcd(n_chunks, 16)
    if chunks_per_step == n_chunks and n_chunks % 4 == 0:
        chunks_per_step //= 2
    assert chunks_per_step % 2 == 0
    return pl.pallas_call(
        functools.partial(_flash_kernel, tk=tk, chunks_per_step=chunks_per_step),
        grid=(n_seq, N_HEADS, nq),
        in_specs=[pl.BlockSpec((1, HEAD_PAD, tq), lambda b, h, i: (h, 0, b * nq + i)),
                  pl.BlockSpec((1, seq_len, HEAD_PAD), lambda b, h, i: (h, b, 0)),
                  pl.BlockSpec((1, V_ROWS, seq_len), lambda b, h, i: (h, 0, b))],
        out_specs=pl.BlockSpec((tq, V_HEAD_DIM), lambda b, h, i: (b * nq + i, h)),
        out_shape=jax.ShapeDtypeStruct((n_seq * seq_len, N_HEADS * V_HEAD_DIM), BF16),
        scratch_shapes=[pltpu.VMEM((2, tk + SCORE_PAD_ROWS, tq), F32)],
        compiler_params=_params(("parallel", "parallel", "arbitrary"),
                                2 * seq_len * (HEAD_PAD + V_ROWS) * 2 + 8 * tq * tk * 4
                                + 4 * tq * HEAD_PAD * 2),
    )(qt, k, vt)


def _pack_bf16_pair(hi, lo):
    bits = lambda v: lax.bitcast_convert_type(v.astype(BF16).astype(F32), jnp.uint32)
    return bits(hi) | (bits(lo) >> 16)


def _unpack_bf16_pair(w):
    hi = lax.bitcast_convert_type(w & jnp.uint32(0xFFFF0000), F32)
    lo = lax.bitcast_convert_type(w << 16, F32)
    return hi.astype(BF16), lo.astype(BF16)


def _fnet_stage1_kernel(x_hbm, g_ref, wc_ref, w1_ref, tw_ref, a_ref, xbuf, sem, u_ref, *, n1, n2, gdim):
    i = pl.program_id(0)
    slot = i % DMA_SLOTS

    def gather(step):
        rows = pl.ds((step // n2) * n1, n1)
        to_slot = step % DMA_SLOTS
        return pltpu.make_async_copy(x_hbm.at[rows, step % n2, :], xbuf.at[to_slot], sem.at[to_slot])

    @pl.when(i == 0)
    def _():
        for ahead in range(DMA_SLOTS - 1):
            gather(ahead).start()

    @pl.when(i + DMA_SLOTS - 1 < pl.num_programs(0))
    def _():
        gather(i + DMA_SLOTS - 1).start()

    gather(i).wait()
    h = _rms(xbuf[slot], g_ref[...]).astype(BF16)
    wc = wc_ref[...]
    for gi in range(FNET_GROUPS):
        cols = slice(gi * gdim, (gi + 1) * gdim)
        u = _dot(h[:, cols], wc)
        u_ref[:n1, cols] = u[:, :gdim]
        u_ref[n1:, cols] = u[:, gdim:]
    a = _dot(w1_ref[...], u_ref[...].astype(BF16))
    a_re = a[:n1]
    a_im = a[n1:]
    tw_c = tw_ref[0, :, 0:1]
    tw_s = tw_ref[0, :, 1:2]
    a_ref[...] = _pack_bf16_pair(a_re * tw_c + a_im * tw_s, a_im * tw_c - a_re * tw_s)


def _fnet_stage2_kernel(a_hbm, w2_ref, z_hbm, abuf, zbuf, in_sem, out_sem, *, n1, n2):
    i = pl.program_id(0)
    n_steps = pl.num_programs(0)
    slot = i % DMA_SLOTS

    def gather(step):
        rows = pl.ds((step // n1) * n2, n2)
        to_slot = step % DMA_SLOTS
        return pltpu.make_async_copy(a_hbm.at[rows, step % n1, :], abuf.at[to_slot], in_sem.at[to_slot])

    def scatter(step):
        rows = pl.ds((step // n1) * n2, n2)
        from_slot = step % DMA_SLOTS
        return pltpu.make_async_copy(zbuf.at[from_slot], z_hbm.at[rows, step % n1, :], out_sem.at[from_slot])

    @pl.when(i == 0)
    def _():
        for ahead in range(DMA_SLOTS - 1):
            gather(ahead).start()

    @pl.when(i + DMA_SLOTS - 1 < n_steps)
    def _():
        gather(i + DMA_SLOTS - 1).start()

    gather(i).wait()
    a_re, a_im = _unpack_bf16_pair(abuf[slot])
    z = _dot(w2_ref[...], jnp.concatenate([a_re, a_im], axis=0))

    @pl.when(i >= DMA_SLOTS)
    def _():
        scatter(i - DMA_SLOTS).wait()

    zbuf[slot] = z
    scatter(i).start()

    @pl.when(i == n_steps - 1)
    def _():
        for back in range(DMA_SLOTS):
            scatter(i - back).wait()


def _dft_tables(n1, n2, gdim):
    def cs(rows, cols, period):
        ang = 2.0 * np.pi * ((np.arange(rows)[:, None] * np.arange(cols)[None, :]) % period) / period
        return np.cos(ang), np.sin(ang)
    cc, sc = cs(gdim, gdim, gdim)
    w_ch = np.concatenate([cc, -sc], axis=1) / np.sqrt(gdim)
    c1, s1 = cs(n1, n1, n1)
    w1 = np.block([[c1, s1], [-s1, c1]]) / np.sqrt(n1)
    c2, s2 = cs(n2, n2, n2)
    w2 = np.concatenate([c2, s2], axis=1) / np.sqrt(n2)
    twc, tws = cs(n2, n1, n1 * n2)
    tw = np.zeros((n2, n1, LANES), np.float64)
    tw[:, :, 0] = twc
    tw[:, :, 1] = tws
    return [jnp.asarray(t, F32) for t in (w_ch, w1, w2, tw)]


def _fnet_mix(x, g, seq_len):
    T, D = x.shape
    n_seq = T // seq_len
    n1 = FFT_N1
    n2 = seq_len // n1
    assert n_seq * min(n1, n2) >= DMA_SLOTS
    gdim = D // FNET_GROUPS
    w_ch, w1, w2, tw = _dft_tables(n1, n2, gdim)
    w_ch, w1, w2 = (t.astype(BF16) for t in (w_ch, w1, w2))
    const2 = lambda shape: pl.BlockSpec(shape, lambda i: (0, 0))
    any_spec = pl.BlockSpec(memory_space=pl.ANY)
    slab = pl.BlockSpec((None, n1, D), lambda i: (i, 0, 0))
    a = pl.pallas_call(
        functools.partial(_fnet_stage1_kernel, n1=n1, n2=n2, gdim=gdim),
        grid=(n_seq * n2,),
        in_specs=[any_spec, const2((1, D)), const2(w_ch.shape), const2(w1.shape),
                  pl.BlockSpec((1, n1, LANES), lambda i: (i % n2, 0, 0))],
        out_specs=slab,
        out_shape=jax.ShapeDtypeStruct((n_seq * n2, n1, D), jnp.uint32),
        scratch_shapes=[pltpu.VMEM((DMA_SLOTS, n1, D), F32), pltpu.SemaphoreType.DMA((DMA_SLOTS,)),
                        pltpu.VMEM((2 * n1, D), F32)],
        compiler_params=_params(("arbitrary",), 24 * n1 * D * 4 + 16 * 1024 * 1024),
    )(x.reshape(T // n2, n2, D), g, w_ch, w1, tw)
    z = pl.pallas_call(
        functools.partial(_fnet_stage2_kernel, n1=n1, n2=n2),
        grid=(n_seq * n1,),
        in_specs=[any_spec, const2(w2.shape)],
        out_specs=any_spec,
        out_shape=jax.ShapeDtypeStruct((n_seq * n2, n1, D), F32),
        scratch_shapes=[pltpu.VMEM((DMA_SLOTS, n2, D), jnp.uint32), pltpu.VMEM((DMA_SLOTS, n2, D), F32),
                        pltpu.SemaphoreType.DMA((DMA_SLOTS,)), pltpu.SemaphoreType.DMA((DMA_SLOTS,))],
        compiler_params=_params(("arbitrary",), 24 * n2 * D * 4 + 16 * 1024 * 1024),
    )(a, w2)
    return z.reshape(T, D)


def _oproj_kernel(x_ref, a_ref, w_ref, o_ref):
    o_ref[...] = x_ref[...] + _dot(a_ref[...].astype(BF16), w_ref[...])


def _oproj_residual(x, a, w, tm):
    T, D = x.shape
    row = lambda i: (i, 0)
    return pl.pallas_call(
        _oproj_kernel,
        grid=(T // tm,),
        in_specs=[pl.BlockSpec((tm, D), row), pl.BlockSpec((tm, a.shape[1]), row), _const_spec(w.shape)],
        out_specs=pl.BlockSpec((tm, D), row),
        out_shape=jax.ShapeDtypeStruct((T, D), F32),
        compiler_params=_params(("parallel",), 2 * (2 * tm * D * 4 + tm * a.shape[1] * a.dtype.itemsize
                                                    + w.size * 2) + 2 * tm * D * 4),
    )(x, a, w)


def _mlp_kernel(x_ref, g_ref, wup_ref, wdn_ref, o_ref, xn_ref):
    @pl.when(pl.program_id(1) == 0)
    def _():
        x = x_ref[...]
        xn_ref[...] = _rms(x, g_ref[...]).astype(BF16)
        o_ref[...] = x

    hcol = jnp.square(jnp.maximum(_dot(xn_ref[...], wup_ref[...]), 0.0)).astype(BF16)
    o_ref[...] += _dot(hcol, wdn_ref[...])


def _mlp(x, g, w_up, w_down, layer, tm, tf):
    T, D = x.shape
    dff = w_up.shape[2]
    return pl.pallas_call(
        _mlp_kernel,
        grid=(T // tm, dff // tf),
        in_specs=[pl.BlockSpec((tm, D), lambda i, f: (i, 0)), pl.BlockSpec((1, D), lambda i, f: (0, 0)),
                  pl.BlockSpec((None, D, tf), lambda i, f: (layer, 0, f)),
                  pl.BlockSpec((None, tf, D), lambda i, f: (layer, f, 0))],
        out_specs=pl.BlockSpec((tm, D), lambda i, f: (i, 0)),
        out_shape=jax.ShapeDtypeStruct((T, D), F32),
        scratch_shapes=[pltpu.VMEM((tm, D), BF16)],
        compiler_params=_params(("parallel", "arbitrary"),
                                2 * (2 * tm * D * 4 + 2 * D * tf * 2) + tm * D * 2 + 3 * tm * tf * 4),
    )(x, g, w_up, w_down)


def _dup_rope_cols(w):
    return jnp.concatenate([w, w], axis=-1)


def _pad_head_vec(g):
    return jnp.concatenate([g[:QK_NOPE_DIM], _dup_rope_cols(g[QK_NOPE_DIM:])])[None, :].astype(F32)


def _rope_tables(n_seq, seq_len):
    inv_freq = ROPE_THETA ** (-jnp.arange(0, QK_ROPE_DIM, 2, dtype=F32) / QK_ROPE_DIM)
    ang = jnp.arange(seq_len, dtype=F32)[:, None] * inv_freq[None, :]
    c, s = jnp.cos(ang), jnp.sin(ang)
    z = jnp.zeros_like(c)
    tile = lambda t: jnp.tile(t, (n_seq, 1))
    return (tile(c).T, tile(s).T,
            tile(jnp.concatenate([c, c, z, z], axis=1)), tile(jnp.concatenate([-s, s, z, z], axis=1)))


def _pick_tile(n, target):
    t = min(n, target)
    while n % t:
        t //= 2
    return t


def _trunk(x, seq_len, p):
    T = x.shape[0]
    rope = _rope_tables(T // seq_len, seq_len)
    tm = _pick_tile(T, 512)
    tf = _pick_tile(p["w_up"].shape[2], 2048)
    ts = _pick_tile(seq_len, 512)
    for i in range(p["w_up"].shape[0]):
        j = i // 2
        if i % 2 == 0:
            qt, k, vt = _mla_project(x, p["attn_norm"][j], p["w_q_a"][j], p["q_a_norm"][j], p["w_q_b"][j],
                                     p["q_norm"][j], p["w_kv_a"][j], p["kv_a_norm"][j], p["w_k_b"][j], p["w_v_b_t"][j],
                                     p["k_norm"][j], rope, tm)
            a = _attention(qt, k, vt, seq_len, ts, ts)
        else:
            a = _fnet_mix(x, p["fnet_norm"][j], seq_len)
        x = _oproj_residual(x, a, p["w_mix_o"][i], tm)
        x = _mlp(x, p["mlp_norm"][i], p["w_up"], p["w_down"], i, tm, tf)
    return x


def kernel(x_prompt, x_sample, attn_norm, w_q_a, q_a_norm, w_q_b, w_kv_a, kv_a_norm, w_kv_b, q_norm, k_norm,
           w_attn_o, fnet_norm, w_fnet_o, mlp_norm, w_up, w_down):
    depth = mlp_norm.shape[0]
    n_attn = attn_norm.shape[0]
    kv_rank = kv_a_norm.shape[1]
    q_rank = q_a_norm.shape[1]
    row2 = lambda v: v[None, :].astype(F32)
    wqbt = jnp.swapaxes(w_q_b.reshape(n_attn, q_rank, N_HEADS, QK_HEAD_DIM), 1, 3)
    wqbt = jnp.pad(jnp.swapaxes(wqbt, 1, 2), ((0, 0), (0, 0), (0, HEAD_PAD - QK_HEAD_DIM), (0, 0)))
    wqbt = wqbt.reshape(n_attn, N_HEADS * HEAD_PAD, q_rank)
    wkvb = w_kv_b.reshape(n_attn, kv_rank, N_HEADS, QK_NOPE_DIM + V_HEAD_DIM)
    wk = wkvb[..., :QK_NOPE_DIM].reshape(n_attn, kv_rank, N_HEADS * QK_NOPE_DIM)
    wvt = jnp.swapaxes(wkvb[..., QK_NOPE_DIM:].reshape(n_attn, kv_rank, N_HEADS * V_HEAD_DIM), 1, 2)
    q_gain = lambda g: jnp.pad(g, (0, HEAD_PAD - QK_HEAD_DIM))[:, None].astype(F32)
    wkva = jnp.concatenate([w_kv_a[..., :kv_rank], _dup_rope_cols(w_kv_a[..., kv_rank:])], axis=-1)
    layers = lambda w: [w[j].astype(BF16) for j in range(w.shape[0])]
    p = {
        "attn_norm": [row2(g) for g in attn_norm], "w_q_a": layers(w_q_a),
        "q_a_norm": [row2(g) for g in q_a_norm], "w_q_b": layers(wqbt),
        "q_norm": [q_gain(g) for g in q_norm], "w_kv_a": layers(wkva),
        "kv_a_norm": [row2(g) for g in kv_a_norm], "w_k_b": layers(wk), "w_v_b_t": layers(wvt),
        "k_norm": [_pad_head_vec(g) for g in k_norm], "fnet_norm": [row2(g) for g in fnet_norm],
        "w_mix_o": [(w_attn_o if i % 2 == 0 else w_fnet_o)[i // 2].astype(BF16) for i in range(depth)],
        "mlp_norm": [row2(g) for g in mlp_norm], "w_up": w_up.astype(BF16), "w_down": w_down.astype(BF16),
    }
    outs = []
    for xg in (x_prompt, x_sample):
        b, s, d = xg.shape
        outs.append(_trunk(xg.reshape(b * s, d), s, p).reshape(b, s, d))
    return tuple(outs)
```
